```python
import math
import jax
import jax.numpy as jnp
from jax import lax
import numpy as np


D_MODEL = 1024
BATCH = 16
SEQ = 2048
DEPTH = 1

CTX_LEN = 256
GRID_W = 64
EPS = 1e-6

N_ATTN_HEADS = 8
ATTN_DH = 64
ATTN_DV = 2 * ATTN_DH
ATTN_QK_DIM = N_ATTN_HEADS * 2 * ATTN_DH
ATTN_V_DIM = N_ATTN_HEADS * ATTN_DV
ATTN_SCALE = ATTN_DH ** -0.5
ATTN_QBLOCK = 128
ROPE_THETA = 10000.0
ROPE_AXIS_DIM = ATTN_DH // 2

D_SSD = D_MODEL
SSD_HEADDIM = 64
SSD_HEADS = D_SSD // SSD_HEADDIM
SSD_GROUPS = 4
SSD_HPG = SSD_HEADS // SSD_GROUPS
SSD_STATE = 128
SSD_CONV_W = 3
SSD_CHUNK = 64
XBC_DIM = D_SSD + 2 * SSD_GROUPS * SSD_STATE

IN_SPLITS = (ATTN_QK_DIM, 2 * ATTN_QK_DIM, 2 * ATTN_QK_DIM + ATTN_V_DIM, 2 * ATTN_QK_DIM + ATTN_V_DIM + D_SSD, 2 * ATTN_QK_DIM + ATTN_V_DIM + D_SSD + XBC_DIM, 2 * ATTN_QK_DIM + ATTN_V_DIM + D_SSD + XBC_DIM + 2 * SSD_HEADS)
IN_COLS = IN_SPLITS[-1] + 2 * D_MODEL

N_EXPERTS = 64
EXPERT_FF = 256
SHARED_FF = 256
TOP_K = 8
N_EXPERT_GROUPS = 8
TOPK_GROUPS = 4
ROUTED_SCALE = 2.5
MOE_BLOCK = 128

kernel_name = 'hybrid_diffattn_ssd_moe_dit_layer'


def rmsnorm(u, g):
    uf = u.astype(jnp.float32)
    y = uf * lax.rsqrt(jnp.mean(uf * uf, axis=-1, keepdims=True) + EPS)
    return (y * g.astype(jnp.float32)).astype(u.dtype)


def modulate(u, shift, scale):
    return u * (1 + scale) + shift


def rev(t):
    return jnp.flip(t, axis=1)


def axial_rope_tables(n_tok):
    rows = n_tok // GRID_W
    row = jnp.repeat(jnp.arange(rows, dtype=jnp.float32), GRID_W)
    col = jnp.broadcast_to(jnp.arange(GRID_W, dtype=jnp.float32)[None, :], (rows, GRID_W)).reshape(-1)
    inv_freq = ROPE_THETA ** (-jnp.arange(0, ROPE_AXIS_DIM, 2, dtype=jnp.float32) / ROPE_AXIS_DIM)
    ang = jnp.concatenate([row[:, None] * inv_freq, col[:, None] * inv_freq], axis=-1)
    ang = jnp.concatenate([ang, ang], axis=-1)
    return jnp.cos(ang), jnp.sin(ang)


def apply_rope(u, cos, sin):
    half = ATTN_DH // 2
    uf = u.astype(jnp.float32)
    rot = jnp.concatenate([-uf[..., half:], uf[..., :half]], axis=-1)
    return (uf * cos[:, None, None, :] + rot * sin[:, None, None, :]).astype(u.dtype)


def diff_attend(q, k, v, lam, g_subln, lam_init):
    s = jnp.einsum('bqhcd,bkhcd->bchqk', q.astype(jnp.float32), k.astype(jnp.float32)) * ATTN_SCALE
    p = jax.nn.softmax(s, axis=-1)
    w = p[:, 0] - lam * p[:, 1]
    o = jnp.einsum('bhqk,bkhd->bqhd', w, v.astype(jnp.float32))
    o = o * lax.rsqrt(jnp.mean(o * o, axis=-1, keepdims=True) + EPS) * g_subln.astype(jnp.float32)
    return (o * (1.0 - lam_init)).astype(v.dtype)


def centred_dwconv(u, w, b):
    pad = SSD_CONV_W // 2
    n = u.shape[1]
    up = jnp.pad(u, ((0, 0), (pad, pad), (0, 0)))
    out = b
    for i in range(SSD_CONV_W):
        out = out + up[:, i:i + n] * w[i]
    return out


def ssd_inputs(xbc, dt_raw, conv_w, conv_b, dt_bias):
    nb, nl, _ = xbc.shape
    u = jax.nn.silu(centred_dwconv(xbc, conv_w, conv_b)).astype(jnp.float32)
    xs, bm, cm = jnp.split(u, (D_SSD, D_SSD + SSD_GROUPS * SSD_STATE), axis=-1)
    xh = xs.reshape(nb, nl, SSD_GROUPS, SSD_HPG, SSD_HEADDIM)
    bm = bm.reshape(nb, nl, SSD_GROUPS, SSD_STATE)
    cm = cm.reshape(nb, nl, SSD_GROUPS, SSD_STATE)
    dt = jax.nn.softplus(dt_raw.astype(jnp.float32).reshape(nb, nl, 2, SSD_GROUPS, SSD_HPG) + dt_bias.astype(jnp.float32).reshape(2, SSD_GROUPS, SSD_HPG))
    return xh, bm, cm, dt[:, :, 0], dt[:, :, 1]


def ssd_chunked(xh, dt, a, bm, cm, h0, with_output):
    nb, nl, ng, nr, npd = xh.shape
    nn = bm.shape[-1]
    nc = nl // SSD_CHUNK
    xc = xh.reshape(nb, nc, SSD_CHUNK, ng, nr, npd)
    dtc = dt.reshape(nb, nc, SSD_CHUNK, ng, nr)
    bc = bm.reshape(nb, nc, SSD_CHUNK, ng, nn)
    acum = jnp.cumsum(dtc * a, axis=2)
    to_end = jnp.exp(acum[:, :, -1:] - acum)
    states = jnp.einsum('bcqgn,bcqgr,bcqgrp->bcgrpn', bc, to_end * dtc, xc)
    chunk_decay = jnp.exp(acum[:, :, -1])

    def step(h, inp):
        st, dec = inp
        return h * dec[..., None, None] + st, (h if with_output else None)

    h_last, h_prev = lax.scan(step, h0, (jnp.moveaxis(states, 1, 0), jnp.moveaxis(chunk_decay, 1, 0)))
    if not with_output:
        return None, h_last
    cc = cm.reshape(nb, nc, SSD_CHUNK, ng, nn)
    seg = acum[:, :, :, None] - acum[:, :, None]
    lower = (jnp.arange(SSD_CHUNK)[:, None] >= jnp.arange(SSD_CHUNK)[None, :])[:, :, None, None]
    decay = jnp.where(lower, jnp.exp(jnp.where(lower, seg, 0.0)), 0.0)
    cb = jnp.einsum('bcign,bcjgn->bcijg', cc, bc)
    y_diag = jnp.einsum('bcijg,bcijgr,bcjgr,bcjgrp->bcigrp', cb, decay, dtc, xc)
    y_off = jnp.einsum('bcign,cbgrpn,bcigr->bcigrp', cc, h_prev, jnp.exp(acum))
    y = (y_diag + y_off).reshape(nb, nl, ng, nr, npd)
    return y, h_last


def merge_branches(o_attn, y_ssd, z, gates, g_ssd_norm, w_branch_attn, w_branch_ssd, w_out):
    out_dtype = z.dtype
    y = y_ssd * jax.nn.silu(z.astype(jnp.float32))
    shp = y.shape
    yg = y.reshape(shp[:-1] + (SSD_GROUPS, D_SSD // SSD_GROUPS))
    yg = yg * lax.rsqrt(jnp.mean(yg * yg, axis=-1, keepdims=True) + EPS)
    y = (yg.reshape(shp) * g_ssd_norm.astype(jnp.float32)).astype(out_dtype)
    ya = o_attn @ w_branch_attn
    ys = y @ w_branch_ssd
    g_a, g_s = jnp.split(jax.nn.sigmoid(gates), 2, axis=-1)
    return (g_a * ya + g_s * ys) @ w_out


def token_mixer(h, hc, update_ctx, lam_init, w_in, lam_q1, lam_k1, lam_q2, lam_k2, g_attn_subln, conv_w, conv_b, dt_bias, a_log, d_skip, g_ssd_norm, w_branch_attn, w_branch_ssd, w_out):
    nb, ns, _ = h.shape
    nl = hc.shape[1]
    w_q, w_k, w_v, w_z, w_xbc, w_dt, w_g = jnp.split(w_in, IN_SPLITS, axis=1)
    q, k, v, z, xbc, dt_raw, gates = jnp.split(h @ w_in, IN_SPLITS, axis=-1)

    lam = (jnp.exp(jnp.sum(lam_q1.astype(jnp.float32) * lam_k1.astype(jnp.float32)))
           - jnp.exp(jnp.sum(lam_q2.astype(jnp.float32) * lam_k2.astype(jnp.float32))) + lam_init)
    cos, sin = axial_rope_tables(ns)
    q = apply_rope(q.reshape(nb, ns, N_ATTN_HEADS, 2, ATTN_DH), cos, sin)
    k = apply_rope(k.reshape(nb, ns, N_ATTN_HEADS, 2, ATTN_DH), cos, sin)
    v = v.reshape(nb, ns, N_ATTN_HEADS, ATTN_DV)
    kc = (hc @ w_k).reshape(nb, nl, N_ATTN_HEADS, 2, ATTN_DH)
    vc = (hc @ w_v).reshape(nb, nl, N_ATTN_HEADS, ATTN_DV)
    k_all = jnp.concatenate([kc, k], axis=1)
    v_all = jnp.concatenate([vc, v], axis=1)
    n_qb = ns // ATTN_QBLOCK
    q_blocks = jnp.swapaxes(q.reshape(nb, n_qb, ATTN_QBLOCK, N_ATTN_HEADS, 2, ATTN_DH), 0, 1)
    o = lax.map(lambda qb: diff_attend(qb, k_all, v_all, lam, g_attn_subln, lam_init), q_blocks)
    o_lat = jnp.swapaxes(o, 0, 1).reshape(nb, ns, ATTN_V_DIM)

    a = -jnp.exp(a_log.astype(jnp.float32)).reshape(2, SSD_GROUPS, SSD_HPG)
    d = d_skip.astype(jnp.float32).reshape(SSD_GROUPS, SSD_HPG, 1)
    xh_c, bm_c, cm_c, dtf_c, dtb_c = ssd_inputs(hc @ w_xbc, hc @ w_dt, conv_w, conv_b, dt_bias)
    h0 = jnp.zeros((nb, SSD_GROUPS, SSD_HPG, SSD_HEADDIM, SSD_STATE), jnp.float32)
    yf_c, hf = ssd_chunked(xh_c, dtf_c, a[0], bm_c, cm_c, h0, update_ctx)
    yb_c, hb = ssd_chunked(rev(xh_c), rev(dtb_c), a[1], rev(bm_c), rev(cm_c), h0, update_ctx)
    xh, bm, cm, dtf, dtb = ssd_inputs(xbc, dt_raw, conv_w, conv_b, dt_bias)
    yf, _ = ssd_chunked(xh, dtf, a[0], bm, cm, hf, True)
    yb, _ = ssd_chunked(rev(xh), rev(dtb), a[1], rev(bm), rev(cm), hb, True)
    y_lat = (yf + rev(yb) + d * xh).reshape(nb, ns, D_SSD)

    out_lat = merge_branches(o_lat, y_lat, z, gates, g_ssd_norm, w_branch_attn, w_branch_ssd, w_out)
    if not update_ctx:
        return out_lat, None
    qc = (hc @ w_q).reshape(nb, nl, N_ATTN_HEADS, 2, ATTN_DH)
    o_ctx = diff_attend(qc, kc, vc, lam, g_attn_subln, lam_init).reshape(nb, nl, ATTN_V_DIM)
    y_ctx = (yf_c + rev(yb_c) + d * xh_c).reshape(nb, nl, D_SSD)
    out_ctx = merge_branches(o_ctx, y_ctx, hc @ w_z, hc @ w_g, g_ssd_norm, w_branch_attn, w_branch_ssd, w_out)
    return out_lat, out_ctx


def moe_ffn(h, w_router, router_bias, w_e_gate, w_e_up, w_e_down, w_sh_gate, w_sh_up, w_sh_down):
    n_tok, d = h.shape
    scores = jax.nn.sigmoid((h @ w_router).astype(jnp.float32))
    sel = scores + router_bias.astype(jnp.float32)
    grp = sel.reshape(n_tok, N_EXPERT_GROUPS, N_EXPERTS // N_EXPERT_GROUPS)
    grp_score = lax.top_k(grp, 2)[0].sum(-1)
    top_g = lax.top_k(grp_score, TOPK_GROUPS)[1]
    gmask = jax.nn.one_hot(top_g, N_EXPERT_GROUPS, dtype=jnp.float32).sum(1) > 0
    emask = jnp.repeat(gmask, N_EXPERTS // N_EXPERT_GROUPS, axis=1)
    top_e = lax.top_k(jnp.where(emask, sel, -jnp.inf), TOP_K)[1]
    wts = jnp.take_along_axis(scores, top_e, axis=1)
    wts = wts / jnp.sum(wts, axis=-1, keepdims=True) * ROUTED_SCALE

    n_assign = n_tok * TOP_K
    n_blocks = -(-n_assign // MOE_BLOCK) + N_EXPERTS
    n_pad = n_blocks * MOE_BLOCK
    e_flat = top_e.reshape(-1)
    order = jnp.argsort(e_flat)
    e_sorted = e_flat[order]
    tok_sorted = (jnp.arange(n_assign, dtype=jnp.int32) // TOP_K)[order]
    w_sorted = wts.reshape(-1)[order]
    counts = jnp.bincount(e_flat, length=N_EXPERTS)
    start = jnp.cumsum(counts) - counts
    padded = (counts + MOE_BLOCK - 1) // MOE_BLOCK * MOE_BLOCK
    pend = jnp.cumsum(padded)
    pstart = pend - padded
    dest = pstart[e_sorted] + jnp.arange(n_assign, dtype=jnp.int32) - start[e_sorted]
    buf_tok = jnp.zeros((n_pad,), jnp.int32).at[dest].set(tok_sorted)
    buf_w = jnp.zeros((n_pad,), jnp.float32).at[dest].set(w_sorted)
    block_e = jnp.minimum(jnp.searchsorted(pend, jnp.arange(n_blocks, dtype=jnp.int32) * MOE_BLOCK, side='right'), N_EXPERTS - 1)

    def expert_block(args):
        tok_b, w_b, e = args
        xb = h[tok_b]
        hid = jax.nn.silu(xb @ w_e_gate[e]) * (xb @ w_e_up[e])
        return (hid @ w_e_down[e]) * w_b[:, None].astype(h.dtype)

    out = lax.map(expert_block, (buf_tok.reshape(n_blocks, MOE_BLOCK), buf_w.reshape(n_blocks, MOE_BLOCK), block_e))
    routed = jax.ops.segment_sum(out.reshape(n_pad, d), buf_tok, num_segments=n_tok)
    shared = (jax.nn.silu(h @ w_sh_gate) * (h @ w_sh_up)) @ w_sh_down
    return routed + shared


def setup_inputs(seed: int = 0) -> dict:
    key = jax.random.key(seed)
    ks = jax.random.split(key, 40)
    f32 = jnp.float32

    def nrm(k, shape, scale):
        return jax.random.normal(k, shape, f32) * scale

    def gain(k, shape):
        return 1.0 + 0.05 * jax.random.normal(k, shape, f32)

    u_dt = jax.random.uniform(ks[18], (DEPTH, 2, SSD_HEADS), f32)
    dt0 = jnp.exp(u_dt * (math.log(0.1) - math.log(0.001)) + math.log(0.001))
    return {
        'x': nrm(ks[0], (BATCH, SEQ, D_MODEL), 1.0),
        'c': nrm(ks[1], (BATCH, D_MODEL), 1.0),
        'ctx': nrm(ks[2], (BATCH, CTX_LEN, D_MODEL), 1.0),
        'c_ctx': nrm(ks[3], (D_MODEL,), 1.0),
        'w_ada': nrm(ks[4], (DEPTH, D_MODEL, 6 * D_MODEL), 0.5 * D_MODEL ** -0.5),
        'b_ada': nrm(ks[5], (DEPTH, 6 * D_MODEL), 0.02),
        'g_pre_mix': gain(ks[6], (DEPTH, D_MODEL)),
        'g_post_mix': gain(ks[7], (DEPTH, D_MODEL)),
        'g_pre_ffn': gain(ks[8], (DEPTH, D_MODEL)),
        'g_post_ffn': gain(ks[9], (DEPTH, D_MODEL)),
        'w_in': nrm(ks[10], (DEPTH, D_MODEL, IN_COLS), D_MODEL ** -0.5),
        'lam_q1': nrm(ks[11], (DEPTH, ATTN_DH), 0.1),
        'lam_k1': nrm(ks[12], (DEPTH, ATTN_DH), 0.1),
        'lam_q2': nrm(ks[13], (DEPTH, ATTN_DH), 0.1),
        'lam_k2': nrm(ks[14], (DEPTH, ATTN_DH), 0.1),
        'g_attn_subln': gain(ks[15], (DEPTH, ATTN_DV)),
        'conv_w': nrm(ks[16], (DEPTH, SSD_CONV_W, XBC_DIM), SSD_CONV_W ** -0.5),
        'conv_b': nrm(ks[17], (DEPTH, XBC_DIM), 0.02),
        'dt_bias': dt0 + jnp.log(-jnp.expm1(-dt0)),
        'a_log': jnp.log(jax.random.uniform(ks[19], (DEPTH, 2, SSD_HEADS), f32, 1.0, 16.0)),
        'd_skip': gain(ks[20], (DEPTH, SSD_HEADS)),
        'g_ssd_norm': gain(ks[21], (DEPTH, D_SSD)),
        'w_branch_attn': nrm(ks[22], (DEPTH, ATTN_V_DIM, D_MODEL), ATTN_V_DIM ** -0.5),
        'w_branch_ssd': nrm(ks[23], (DEPTH, D_SSD, D_MODEL), D_SSD ** -0.5),
        'w_out': nrm(ks[24], (DEPTH, D_MODEL, D_MODEL), D_MODEL ** -0.5),
        'w_router': nrm(ks[25], (DEPTH, D_MODEL, N_EXPERTS), D_MODEL ** -0.5),
        'router_bias': nrm(ks[26], (DEPTH, N_EXPERTS), 0.01),
        'w_e_gate': nrm(ks[27], (DEPTH, N_EXPERTS, D_MODEL, EXPERT_FF), D_MODEL ** -0.5),
        'w_e_up': nrm(ks[28], (DEPTH, N_EXPERTS, D_MODEL, EXPERT_FF), D_MODEL ** -0.5),
        'w_e_down': nrm(ks[29], (DEPTH, N_EXPERTS, EXPERT_FF, D_MODEL), EXPERT_FF ** -0.5),
        'w_sh_gate': nrm(ks[30], (DEPTH, D_MODEL, SHARED_FF), D_MODEL ** -0.5),
        'w_sh_up': nrm(ks[31], (DEPTH, D_MODEL, SHARED_FF), D_MODEL ** -0.5),
        'w_sh_down': nrm(ks[32], (DEPTH, SHARED_FF, D_MODEL), SHARED_FF ** -0.5),
    }


def reference(x, c, ctx, c_ctx, w_ada, b_ada, g_pre_mix, g_post_mix, g_pre_ffn, g_post_ffn, w_in, lam_q1, lam_k1, lam_q2, lam_k2, g_attn_subln, conv_w, conv_b, dt_bias, a_log, d_skip, g_ssd_norm, w_branch_attn, w_branch_ssd, w_out, w_router, router_bias, w_e_gate, w_e_up, w_e_down, w_sh_gate, w_sh_up, w_sh_down):
    nb, ns, d = x.shape
    for li in range(DEPTH):
        update_ctx = li < DEPTH - 1
        lam_init = 0.8 - 0.6 * math.exp(-0.3 * li)
        mod = jax.nn.silu(c) @ w_ada[li] + b_ada[li]
        sh1, sc1, gt1, sh2, sc2, gt2 = jnp.split(mod[:, None, :], 6, axis=-1)
        mod_c = jax.nn.silu(c_ctx) @ w_ada[li] + b_ada[li]
        sh1c, sc1c, gt1c, sh2c, sc2c, gt2c = jnp.split(mod_c, 6)

        h = modulate(rmsnorm(x, g_pre_mix[li]), sh1, sc1)
        hc = modulate(rmsnorm(ctx, g_pre_mix[li]), sh1c, sc1c)
        mix, mix_c = token_mixer(h, hc, update_ctx, lam_init, w_in[li], lam_q1[li], lam_k1[li], lam_q2[li], lam_k2[li], g_attn_subln[li], conv_w[li], conv_b[li], dt_bias[li], a_log[li], d_skip[li], g_ssd_norm[li], w_branch_attn[li], w_branch_ssd[li], w_out[li])
        x = x + gt1 * rmsnorm(mix, g_post_mix[li])

        h = modulate(rmsnorm(x, g_pre_ffn[li]), sh2, sc2)
        moe_w = (w_router[li], router_bias[li], w_e_gate[li], w_e_up[li], w_e_down[li], w_sh_gate[li], w_sh_up[li], w_sh_down[li])
        if update_ctx:
            ctx = ctx + gt1c * rmsnorm(mix_c, g_post_mix[li])
            hc = modulate(rmsnorm(ctx, g_pre_ffn[li]), sh2c, sc2c)
            f = moe_ffn(jnp.concatenate([h.reshape(-1, d), hc.reshape(-1, d)], axis=0), *moe_w)
            f_lat = f[: nb * ns]
            ctx = ctx + gt2c * rmsnorm(f[nb * ns:].reshape(ctx.shape), g_post_ffn[li])
        else:
            f_lat = moe_ffn(h.reshape(-1, d), *moe_w)
        x = x + gt2 * rmsnorm(f_lat.reshape(x.shape), g_post_ffn[li])
    return x
```

```python
import functools
import math

import jax
import jax.numpy as jnp
from jax import lax
from jax.experimental import pallas as pl
from jax.experimental.pallas import tpu as pltpu

F32 = jnp.float32
BF16 = jnp.bfloat16
HIGHEST = lax.Precision.HIGHEST

D_MODEL = 1024
CTX_LEN = 256
GRID_W = 64
EPS = 1e-6

N_HEADS = 8
ATTN_DH = 64
ATTN_DV = 128
ATTN_SCALE = ATTN_DH ** -0.5
ROPE_THETA = 10000.0
ROPE_AXIS_DIM = ATTN_DH // 2
LAM_INIT = 0.8 - 0.6 * math.exp(-0.3 * 0)

D_SSD = 1024
SSD_HEADDIM = 64
SSD_HEADS = 16
SSD_GROUPS = 4
SSD_HPG = 4
SSD_STATE = 128
XBC_DIM = D_SSD + 2 * SSD_GROUPS * SSD_STATE
GROUP_W = SSD_HPG * SSD_HEADDIM

N_EXPERTS = 64
EXPERT_FF = 256
TOP_K = 8
N_EXPERT_GROUPS = 8
EXPERTS_PER_GROUP = N_EXPERTS // N_EXPERT_GROUPS
TOPK_GROUPS = 4
ROUTED_SCALE = 2.5

LANES = 128
SUBLANES = 8
VMEM_LIMIT = 52 * 1024 * 1024

PROJ_TM = 1024
PROJ_TN = 1024
ATTN_TQ = 256
SSD_Q = 256
MERGE_TM = 256
ROUTE_TL = 512
EXPERT_TM = 512
FINAL_TM = 512

NEG_BIG = -1e30

_NT = (((1,), (1,)), ((), ()))
_TN = (((0,), (0,)), ((), ()))


def _cparams(*sem):
    return pltpu.CompilerParams(dimension_semantics=sem, vmem_limit_bytes=VMEM_LIMIT)


def _silu(v):
    return v * jax.nn.sigmoid(v)


def _softplus(v):
    return jnp.maximum(v, 0.0) + jnp.log1p(jnp.exp(-jnp.abs(v)))


def _rms(v, g):
    return v * lax.rsqrt(jnp.mean(v * v, axis=-1, keepdims=True) + EPS) * g


def _mod_kernel(c_ref, w_ref, b_ref, o_ref):
    o_ref[...] = jnp.dot(_silu(c_ref[...]), w_ref[...], preferred_element_type=F32,
                         precision=HIGHEST) + b_ref[...]


def _modulation(c_all, w_ada, b_ada):
    rows, d = c_all.shape
    n = w_ada.shape[1]
    tn = 1024
    return pl.pallas_call(
        _mod_kernel,
        out_shape=jax.ShapeDtypeStruct((rows, n), F32),
        grid=(n // tn,),
        in_specs=[pl.BlockSpec((rows, d), lambda j: (0, 0)),
                  pl.BlockSpec((d, tn), lambda j: (0, j)),
                  pl.BlockSpec((1, tn), lambda j: (0, j))],
        out_specs=pl.BlockSpec((rows, tn), lambda j: (0, j)),
        compiler_params=_cparams("arbitrary"),
    )(c_all, w_ada, b_ada.reshape(1, n))


def _proj_kernel(*refs, rope, has_dt):
    it = iter(refs)
    x_ref, sh_ref, sc_ref, g_ref, w_ref = (next(it) for _ in range(5))
    if rope:
        cos_ref, sin_ref = next(it), next(it)
    if has_dt:
        wdt_ref, wdtT_ref = next(it), next(it)
    o_ref = next(it)
    if has_dt:
        dt_ref, dtT_ref = next(it), next(it)
    h_scr = next(it)
    j = pl.program_id(1)

    @pl.when(j == 0)
    def _():
        h = (_rms(x_ref[...], g_ref[...]) * (1.0 + sc_ref[0]) + sh_ref[0]).astype(BF16)
        h_scr[...] = h
        if has_dt:
            dt_ref[...] = jnp.dot(h, wdt_ref[...], preferred_element_type=F32)
            dtT_ref[...] = lax.dot_general(wdtT_ref[...], h, _NT, preferred_element_type=F32)

    acc = jnp.dot(h_scr[...], w_ref[...], preferred_element_type=F32)
    if rope:
        @pl.when(j < 2)
        def _():
            qs = jnp.where(j == 0, ATTN_SCALE, 1.0).astype(F32)
            cos = cos_ref[...] * qs
            sin = sin_ref[...] * qs
            lane = lax.broadcasted_iota(jnp.int32, cos.shape, 1)
            first = (lane % ATTN_DH) < (ATTN_DH // 2)
            for s in range(acc.shape[1] // LANES):
                u = acc[:, s * LANES:(s + 1) * LANES]
                rot = jnp.where(first, pltpu.roll(u, LANES - ATTN_DH // 2, 1), pltpu.roll(u, ATTN_DH // 2, 1))
                o_ref[:, s * LANES:(s + 1) * LANES] = (u * cos + rot * sin).astype(o_ref.dtype)

        @pl.when(j >= 2)
        def _():
            o_ref[...] = acc.astype(o_ref.dtype)
    else:
        o_ref[...] = acc.astype(o_ref.dtype)


def _project(x2d, shift, scale, gain, w, out_dtype, *, rows_per_mod, rope_tables=None, w_dt=None):
    m, d = x2d.shape
    n = w.shape[1]
    tm, tn = min(PROJ_TM, rows_per_mod), PROJ_TN
    tiles_per_mod = rows_per_mod // tm
    rope = rope_tables is not None
    has_dt = w_dt is not None
    in_specs = [pl.BlockSpec((tm, d), lambda i, j: (i, 0)),
                pl.BlockSpec((1, 1, d), lambda i, j: (i // tiles_per_mod, 0, 0)),
                pl.BlockSpec((1, 1, d), lambda i, j: (i // tiles_per_mod, 0, 0)),
                pl.BlockSpec((1, d), lambda i, j: (0, 0)),
                pl.BlockSpec((d, tn), lambda i, j: (0, j))]
    args = [x2d, shift, scale, gain, w]
    out_shape = [jax.ShapeDtypeStruct((m, n), out_dtype)]
    out_specs = [pl.BlockSpec((tm, tn), lambda i, j: (i, j))]
    if rope:
        cos, sin = rope_tables
        pos_tiles = cos.shape[0] // tm
        in_specs += [pl.BlockSpec((tm, LANES), lambda i, j: (i % pos_tiles, 0))] * 2
        args += [cos, sin]
    if has_dt:
        in_specs += [pl.BlockSpec((d, LANES), lambda i, j: (0, 0)),
                     pl.BlockSpec((LANES, d), lambda i, j: (0, 0))]
        args += [w_dt, w_dt.T]
        out_shape += [jax.ShapeDtypeStruct((m, LANES), F32), jax.ShapeDtypeStruct((LANES, m), F32)]
        out_specs += [pl.BlockSpec((tm, LANES), lambda i, j: (i, 0)),
                      pl.BlockSpec((LANES, tm), lambda i, j: (0, i))]
    res = pl.pallas_call(
        functools.partial(_proj_kernel, rope=rope, has_dt=has_dt),
        out_shape=out_shape,
        grid=(m // tm, n // tn),
        in_specs=in_specs,
        out_specs=out_specs,
        scratch_shapes=[pltpu.VMEM((tm, d), BF16)],
        compiler_params=_cparams("parallel", "arbitrary"),
    )(*args)
    return res if has_dt else res[0]


def _rope_tables(n_tok):
    rows = n_tok // GRID_W
    row = jnp.repeat(jnp.arange(rows, dtype=F32), GRID_W)
    col = jnp.broadcast_to(jnp.arange(GRID_W, dtype=F32)[None, :], (rows, GRID_W)).reshape(-1)
    inv_freq = ROPE_THETA ** (-jnp.arange(0, ROPE_AXIS_DIM, 2, dtype=F32) / ROPE_AXIS_DIM)
    ang = jnp.concatenate([row[:, None] * inv_freq, col[:, None] * inv_freq], axis=-1)
    ang = jnp.concatenate([ang, ang], axis=-1)
    cos, sin = jnp.cos(ang), jnp.sin(ang)
    half = ATTN_DH // 2
    sin = jnp.concatenate([-sin[:, :half], sin[:, half:]], axis=-1)
    return jnp.concatenate([cos, cos], axis=-1), jnp.concatenate([sin, sin], axis=-1)


def _attn_kernel(lam_ref, q_ref, kc_ref, k_ref, vc_ref, v_ref, g_ref, o_ref):
    tq = q_ref.shape[0]
    q = q_ref[...]
    lane = lax.broadcasted_iota(jnp.int32, q.shape, 1)
    zero = jnp.zeros_like(q)
    qq = jnp.concatenate([jnp.where(lane < ATTN_DH, q, zero), jnp.where(lane >= ATTN_DH, q, zero)], axis=0)
    s_c = lax.dot_general(qq, kc_ref[...], _NT, preferred_element_type=F32)
    s_l = lax.dot_general(qq, k_ref[...], _NT, preferred_element_type=F32)
    m = jnp.maximum(jnp.max(s_c, axis=-1, keepdims=True), jnp.max(s_l, axis=-1, keepdims=True))
    e_c = jnp.exp(s_c - m)
    e_l = jnp.exp(s_l - m)
    inv = 1.0 / (jnp.sum(e_c, axis=-1, keepdims=True) + jnp.sum(e_l, axis=-1, keepdims=True))
    a0 = inv[:tq]
    a1 = inv[tq:] * lam_ref[0]
    w_c = (e_c[:tq] * a0 - e_c[tq:] * a1).astype(BF16)
    w_l = (e_l[:tq] * a0 - e_l[tq:] * a1).astype(BF16)
    o = (jnp.dot(w_c, vc_ref[...], preferred_element_type=F32)
         + jnp.dot(w_l, v_ref[...], preferred_element_type=F32))
    o = _rms(o, g_ref[...]) * (1.0 - LAM_INIT)
    o_ref[...] = o.astype(o_ref.dtype)


def _diff_attention(lam, qkv, kv_ctx, g_subln, nb, ns):
    tq = ATTN_TQ
    nq = ns // tq
    nl = kv_ctx.shape[0] // nb
    return pl.pallas_call(
        _attn_kernel,
        out_shape=jax.ShapeDtypeStruct((nb * ns, N_HEADS * ATTN_DV), BF16),
        grid=(nb, N_HEADS, nq),
        in_specs=[pl.BlockSpec(memory_space=pltpu.SMEM),
                  pl.BlockSpec((tq, LANES), lambda b, h, i: (b * nq + i, h)),
                  pl.BlockSpec((nl, LANES), lambda b, h, i: (b, h)),
                  pl.BlockSpec((ns, LANES), lambda b, h, i: (b, N_HEADS + h)),
                  pl.BlockSpec((nl, LANES), lambda b, h, i: (b, N_HEADS + h)),
                  pl.BlockSpec((ns, LANES), lambda b, h, i: (b, 2 * N_HEADS + h)),
                  pl.BlockSpec((1, LANES), lambda b, h, i: (0, 0))],
        out_specs=pl.BlockSpec((tq, LANES), lambda b, h, i: (b * nq + i, h)),
        compiler_params=_cparams("parallel", "parallel", "arbitrary"),
    )(lam, qkv, kv_ctx, qkv, kv_ctx, qkv, g_subln)


def _ssd_conv(xin, prev_row, next_row, cw_ref, cb_ref):
    q = xin.shape[0]
    rid = lax.broadcasted_iota(jnp.int32, xin.shape, 0)
    up = jnp.where(rid == 0, prev_row, pltpu.roll(xin, 1, 0))
    dn = jnp.where(rid == q - 1, next_row, pltpu.roll(xin, q - 1, 0))
    return _silu(cb_ref[...] + up * cw_ref[0:1, :] + xin * cw_ref[1:2, :] + dn * cw_ref[2:3, :])


def _split_dot(v, e_bf16):
    hi = v.astype(BF16)
    lo = (v - hi.astype(F32)).astype(BF16)
    return (jnp.dot(hi, e_bf16, preferred_element_type=F32)
            + jnp.dot(lo, e_bf16, preferred_element_type=F32))


def _ssd_chunk(u, dt_blk, dtT_blk, bias_row, bias_col, alog_row, alog_col, is_f, states, want_y):
    q = u.shape[0]
    nh = SSD_HEADS
    xs = u[:, :D_SSD]
    bm = u[:, D_SSD:D_SSD + SSD_GROUPS * SSD_STATE]
    cm = u[:, D_SSD + SSD_GROUPS * SSD_STATE:]

    dt_all = _softplus(dt_blk + bias_row)
    dtT_all = _softplus(dtT_blk + bias_col)
    dt = jnp.where(is_f, dt_all[:, 0:nh], dt_all[:, nh:2 * nh])
    dtT = jnp.where(is_f, dtT_all[0:nh, :], dtT_all[nh:2 * nh, :])
    a_r = -jnp.exp(jnp.where(is_f, alog_row[0:1, :], alog_row[1:2, :]))
    a_c = -jnp.exp(jnp.where(is_f, alog_col[:, 0:1], alog_col[:, 1:2]))

    ri = lax.broadcasted_iota(jnp.int32, (q, q), 0)
    ci = lax.broadcasted_iota(jnp.int32, (q, q), 1)
    lower = (ri >= ci).astype(F32)
    upper = (ri <= ci).astype(F32)
    tmat = jnp.where(is_f, lower, upper)
    tmat_t = jnp.where(is_f, upper, lower)
    causal = tmat > 0.5
    acum = jnp.dot(tmat, dt * a_r, preferred_element_type=F32, precision=HIGHEST)
    acum_t = jnp.dot(dtT * a_c, tmat_t, preferred_element_type=F32, precision=HIGHEST)
    tot = jnp.where(is_f, acum[q - 1:q, :], acum[0:1, :])
    wgt = jnp.exp(tot - acum) * dt

    hid = lax.broadcasted_iota(jnp.int32, (nh, D_SSD), 0)
    cid = lax.broadcasted_iota(jnp.int32, (nh, D_SSD), 1)
    expand = (cid // SSD_HEADDIM == hid).astype(BF16)
    pieces = [wgt, jnp.broadcast_to(jnp.exp(tot), (SUBLANES, nh))]
    if want_y:
        pieces += [dt, jnp.exp(acum)]
    wide = _split_dot(jnp.concatenate(pieces, axis=0), expand)
    wgt_x = wide[0:q]
    dec_x = wide[q:q + 1]
    if want_y:
        dt_x = wide[q + SUBLANES:2 * q + SUBLANES]
        ea_x = wide[2 * q + SUBLANES:3 * q + SUBLANES]

    ys, new_states = [], []
    for g in range(SSD_GROUPS):
        gsl = slice(g * GROUP_W, (g + 1) * GROUP_W)
        nsl = slice(g * SSD_STATE, (g + 1) * SSD_STATE)
        xg = xs[:, gsl]
        bm_g = bm[:, nsl].astype(BF16)
        upd = lax.dot_general(bm_g, (xg * wgt_x[:, gsl]).astype(BF16), _TN, preferred_element_type=F32)
        new_states.append(upd if states is None else states[g] * dec_x[:, gsl] + upd)
        if not want_y:
            continue
        cm_g = cm[:, nsl].astype(BF16)
        cb = lax.dot_general(cm_g, bm_g, _NT, preferred_element_type=F32)
        xdt = (xg * dt_x[:, gsl]).astype(BF16)
        y_heads = []
        for r in range(SSD_HPG):
            h = g * SSD_HPG + r
            seg = acum[:, h:h + 1] - acum_t[h:h + 1, :]
            m_h = (cb * jnp.exp(jnp.where(causal, seg, NEG_BIG))).astype(BF16)
            y_heads.append(jnp.dot(m_h, xdt[:, r * SSD_HEADDIM:(r + 1) * SSD_HEADDIM], preferred_element_type=F32))
        y_g = jnp.concatenate(y_heads, axis=1)
        if states is not None:
            y_g = y_g + jnp.dot(cm_g, states[g].astype(BF16), preferred_element_type=F32) * ea_x[:, gsl]
        ys.append(y_g)
    return (jnp.concatenate(ys, axis=1) if want_y else None), new_states


def _ssd_ctx_kernel(x_ref, dt_ref, dtT_ref, cw_ref, cb_ref, brow_ref, bcol_ref, arow_ref, acol_ref, st_ref):
    is_f = pl.program_id(1) == 0
    xin = x_ref[...]
    zero_row = jnp.zeros((1, xin.shape[1]), F32)
    u = _ssd_conv(xin, zero_row, zero_row, cw_ref, cb_ref)
    _, new_states = _ssd_chunk(u, dt_ref[...], dtT_ref[...], brow_ref[...], bcol_ref[...], arow_ref[...],
                               acol_ref[...], is_f, None, False)
    for g in range(SSD_GROUPS):
        st_ref[0, 0, g] = new_states[g]


def _ssd_lat_kernel(x_ref, prev_ref, next_ref, dt_ref, dtT_ref, h0_ref, cw_ref, cb_ref, brow_ref, bcol_ref,
                    arow_ref, acol_ref, dskip_ref, o_ref, st_scr, *, n_chunks):
    d = pl.program_id(1)
    s = pl.program_id(2)
    is_f = d == 0
    chunk = jnp.where(is_f, s, n_chunks - 1 - s)
    q = x_ref.shape[0]

    @pl.when(s == 0)
    def _():
        st_scr[...] = h0_ref[0, 0]

    xin = x_ref[...]
    prev_row = jnp.where(chunk == 0, 0.0, prev_ref[SUBLANES - 1:SUBLANES, :])
    next_row = jnp.where(chunk == n_chunks - 1, 0.0, next_ref[0:1, :])
    u = _ssd_conv(xin, prev_row, next_row, cw_ref, cb_ref)
    states = [st_scr[g] for g in range(SSD_GROUPS)]
    y, new_states = _ssd_chunk(u, dt_ref[...], dtT_ref[...], brow_ref[...], bcol_ref[...], arow_ref[...],
                               acol_ref[...], is_f, states, True)
    for g in range(SSD_GROUPS):
        st_scr[g] = new_states[g]
    row0 = pl.multiple_of(chunk * q, q)

    @pl.when(is_f)
    def _():
        o_ref[pl.ds(row0, q), :] = y + dskip_ref[...] * u[:, :D_SSD]

    @pl.when(jnp.logical_not(is_f))
    def _():
        o_ref[pl.ds(row0, q), :] += y


def _ssd(xbc_lat_src, lat_col_block, dt_lat, dtT_lat, xbc_ctx, dt_ctx, dtT_ctx, conv_w, conv_b, dt_bias, a_log,
         d_skip, nb, ns):
    q = SSD_Q
    nl = xbc_ctx.shape[0] // nb
    nh = SSD_HEADS
    brow = jnp.zeros((1, LANES), F32).at[0, :2 * nh].set(dt_bias.reshape(-1))
    bcol = brow.reshape(LANES, 1)
    arow = a_log.reshape(2, nh)
    acol = arow.T
    cb2 = conv_b.reshape(1, XBC_DIM)
    const2 = lambda *_: (0, 0)
    param_specs = [pl.BlockSpec((3, XBC_DIM), const2), pl.BlockSpec((1, XBC_DIM), const2),
                   pl.BlockSpec((1, LANES), const2), pl.BlockSpec((LANES, 1), const2),
                   pl.BlockSpec((2, nh), const2), pl.BlockSpec((nh, 2), const2)]
    params = [conv_w, cb2, brow, bcol, arow, acol]
    state_shape = (nb, 2, SSD_GROUPS, SSD_STATE, GROUP_W)

    h0 = pl.pallas_call(
        _ssd_ctx_kernel,
        out_shape=jax.ShapeDtypeStruct(state_shape, F32),
        grid=(nb, 2),
        in_specs=[pl.BlockSpec((nl, XBC_DIM), lambda b, d: (b, 0)),
                  pl.BlockSpec((nl, LANES), lambda b, d: (b, 0)),
                  pl.BlockSpec((LANES, nl), lambda b, d: (0, b))] + param_specs,
        out_specs=pl.BlockSpec((1, 1, SSD_GROUPS, SSD_STATE, GROUP_W), lambda b, d: (b, d, 0, 0, 0)),
        compiler_params=_cparams("parallel", "arbitrary"),
    )(xbc_ctx, dt_ctx, dtT_ctx, *params)

    nc = ns // q
    halo_per_chunk = q // SUBLANES
    n_halo = nb * ns // SUBLANES

    def chunk_of(d, s):
        return jnp.where(d == 0, s, nc - 1 - s)

    def prev_map(b, d, s):
        return (jnp.maximum((b * nc + chunk_of(d, s)) * halo_per_chunk - 1, 0), lat_col_block)

    def next_map(b, d, s):
        return (jnp.minimum((b * nc + chunk_of(d, s) + 1) * halo_per_chunk, n_halo - 1), lat_col_block)

    return pl.pallas_call(
        functools.partial(_ssd_lat_kernel, n_chunks=nc),
        out_shape=jax.ShapeDtypeStruct((nb * ns, D_SSD), F32),
        grid=(nb, 2, nc),
        in_specs=[pl.BlockSpec((q, XBC_DIM), lambda b, d, s: (b * nc + chunk_of(d, s), lat_col_block)),
                  pl.BlockSpec((SUBLANES, XBC_DIM), prev_map),
                  pl.BlockSpec((SUBLANES, XBC_DIM), next_map),
                  pl.BlockSpec((q, LANES), lambda b, d, s: (b * nc + chunk_of(d, s), 0)),
                  pl.BlockSpec((LANES, q), lambda b, d, s: (0, b * nc + chunk_of(d, s))),
                  pl.BlockSpec((1, 1, SSD_GROUPS, SSD_STATE, GROUP_W), lambda b, d, s: (b, d, 0, 0, 0))]
        + param_specs + [pl.BlockSpec((1, D_SSD), const2)],
        out_specs=pl.BlockSpec((ns, D_SSD), lambda b, d, s: (b, 0)),
        scratch_shapes=[pltpu.VMEM((SSD_GROUPS, SSD_STATE, GROUP_W), F32)],
        compiler_params=_cparams("parallel", "arbitrary", "arbitrary"),
    )(xbc_lat_src, xbc_lat_src, xbc_lat_src, dt_lat, dtT_lat, h0, *params,
      jnp.repeat(d_skip, SSD_HEADDIM).reshape(1, D_SSD))


def _merge_kernel(y_ref, z_ref, ga_ref, gs_ref, oa_ref, x_ref, gt1_ref, sh2_ref, sc2_ref, gssd_ref, gpost_ref,
                  gpre_ref, wba_ref, wbs_ref, wout_ref, wrt_ref, x1_ref, h2_ref, lg_ref):
    y = y_ref[...] * _silu(z_ref[...])
    gw = D_SSD // SSD_GROUPS
    y = jnp.concatenate(
        [y[:, g * gw:(g + 1) * gw]
         * lax.rsqrt(jnp.mean(y[:, g * gw:(g + 1) * gw] * y[:, g * gw:(g + 1) * gw], axis=-1, keepdims=True) + EPS)
         for g in range(SSD_GROUPS)], axis=1)
    y = (y * gssd_ref[...]).astype(BF16)
    ys = jnp.dot(y, wbs_ref[...], preferred_element_type=F32)
    ya = jnp.dot(oa_ref[...], wba_ref[...], preferred_element_type=F32)
    mixed = (jax.nn.sigmoid(ga_ref[...]) * ya + jax.nn.sigmoid(gs_ref[...]) * ys).astype(BF16)
    mix = jnp.dot(mixed, wout_ref[...], preferred_element_type=F32)
    x1 = x_ref[...] + gt1_ref[0] * _rms(mix, gpost_ref[...])
    x1_ref[...] = x1
    h2 = _rms(x1, gpre_ref[...]) * (1.0 + sc2_ref[0]) + sh2_ref[0]
    h2_ref[...] = h2.astype(BF16)
    lg_ref[...] = lax.dot_general(wrt_ref[...], h2, _NT, preferred_element_type=F32, precision=HIGHEST)


def _merge(y_ssd, rest, oattn, x2d, gt1, sh2, sc2, g_ssd_norm, g_post_mix, g_pre_ffn, wba, wbs, wout, w_router_t,
           rows_per_batch):
    m, d = x2d.shape
    tm = MERGE_TM
    tpb = rows_per_batch // tm
    row = lambda c: pl.BlockSpec((tm, d), lambda i: (i, c))
    per_batch = pl.BlockSpec((1, 1, d), lambda i: (i // tpb, 0, 0))
    vec = pl.BlockSpec((1, d), lambda i: (0, 0))
    mat = pl.BlockSpec((d, d), lambda i: (0, 0))
    return pl.pallas_call(
        _merge_kernel,
        out_shape=[jax.ShapeDtypeStruct((m, d), F32), jax.ShapeDtypeStruct((m, d), BF16),
                   jax.ShapeDtypeStruct((N_EXPERTS, m), F32)],
        grid=(m // tm,),
        in_specs=[row(0), row(2), row(3), row(4), row(0), row(0), per_batch, per_batch, per_batch, vec, vec, vec,
                  mat, mat, mat, pl.BlockSpec((N_EXPERTS, d), lambda i: (0, 0))],
        out_specs=[row(0), row(0), pl.BlockSpec((N_EXPERTS, tm), lambda i: (0, i))],
        compiler_params=_cparams("parallel"),
    )(y_ssd, rest, rest, rest, oattn, x2d, gt1, sh2, sc2, g_ssd_norm, g_post_mix, g_pre_ffn, wba, wbs, wout,
      w_router_t)


def _first_index(hit, iota, limit):
    return jnp.min(jnp.where(hit, iota, limit), axis=0, keepdims=True)


def _route_kernel(lg_ref, bias_ref, eidx_ref, wts_ref, rank_ref, cnt_ref, carry):
    i = pl.program_id(0)
    tl = lg_ref.shape[1]

    @pl.when(i == 0)
    def _():
        carry[...] = jnp.zeros_like(carry)

    scores = jax.nn.sigmoid(lg_ref[...])
    sel = scores + bias_ref[...]
    epg = EXPERTS_PER_GROUP
    io8 = lax.broadcasted_iota(jnp.int32, (epg, tl), 0)
    neg_inf = -jnp.inf

    grp = []
    for g in range(N_EXPERT_GROUPS):
        blk = sel[g * epg:(g + 1) * epg]
        m1 = jnp.max(blk, axis=0, keepdims=True)
        rest = jnp.where(io8 == _first_index(blk == m1, io8, epg), neg_inf, blk)
        grp.append(m1 + jnp.max(rest, axis=0, keepdims=True))
    gsc = jnp.concatenate(grp, axis=0)
    gkeep = jnp.zeros_like(gsc)
    for _ in range(TOPK_GROUPS):
        pick = io8 == _first_index(gsc == jnp.max(gsc, axis=0, keepdims=True), io8, N_EXPERT_GROUPS)
        gkeep = jnp.where(pick, 1.0, gkeep)
        gsc = jnp.where(pick, neg_inf, gsc)
    cand = jnp.concatenate(
        [jnp.where(gkeep[g:g + 1] > 0.5, sel[g * epg:(g + 1) * epg], neg_inf) for g in range(N_EXPERT_GROUPS)], axis=0)

    ioe = lax.broadcasted_iota(jnp.int32, (N_EXPERTS, tl), 0)
    picks, pscore = [], []
    chosen = jnp.zeros_like(cand)
    for _ in range(TOP_K):
        e_k = _first_index(cand == jnp.max(cand, axis=0, keepdims=True), ioe, N_EXPERTS)
        hit = ioe == e_k
        picks.append(e_k)
        pscore.append(jnp.sum(jnp.where(hit, scores, 0.0), axis=0, keepdims=True))
        chosen = jnp.where(hit, 1.0, chosen)
        cand = jnp.where(hit, neg_inf, cand)
    wsum = pscore[0]
    for k in range(1, TOP_K):
        wsum = wsum + pscore[k]

    ti = lax.broadcasted_iota(jnp.int32, (tl, tl), 0)
    tj = lax.broadcasted_iota(jnp.int32, (tl, tl), 1)
    before = (ti < tj).astype(BF16)
    rank = jnp.dot(chosen.astype(BF16), before, preferred_element_type=F32) + carry[...]
    total = carry[...] + jnp.sum(chosen, axis=1, keepdims=True)
    carry[...] = total
    cnt_ref[...] = jnp.broadcast_to(total, cnt_ref.shape)

    eidx_ref[...] = jnp.concatenate(picks, axis=0)
    wts_ref[...] = jnp.concatenate([p / wsum * ROUTED_SCALE for p in pscore], axis=0)
    rank_ref[...] = jnp.concatenate(
        [jnp.sum(jnp.where(ioe == e_k, rank, 0.0), axis=0, keepdims=True) for e_k in picks], axis=0).astype(jnp.int32)


def _route(logits_t, router_bias):
    ne, m = logits_t.shape
    tl = ROUTE_TL
    tok = lambda dt: jax.ShapeDtypeStruct((TOP_K, m), dt)
    tok_spec = pl.BlockSpec((TOP_K, tl), lambda i: (0, i))
    return pl.pallas_call(
        _route_kernel,
        out_shape=[tok(jnp.int32), tok(F32), tok(jnp.int32), jax.ShapeDtypeStruct((ne, LANES), F32)],
        grid=(m // tl,),
        in_specs=[pl.BlockSpec((ne, tl), lambda i: (0, i)), pl.BlockSpec((ne, 1), lambda i: (0, 0))],
        out_specs=[tok_spec, tok_spec, tok_spec, pl.BlockSpec((ne, LANES), lambda i: (0, 0))],
        scratch_shapes=[pltpu.VMEM((ne, 1), F32)],
        compiler_params=_cparams("arbitrary"),
    )(logits_t, router_bias.reshape(ne, 1))


def _expert_kernel(tile_e_ref, nused_ref, x_ref, wgu_ref, wdn_ref, o_ref):
    @pl.when(pl.program_id(0) < nused_ref[0])
    def _():
        gu = jnp.dot(x_ref[...], wgu_ref[0], preferred_element_type=F32)
        hid = (_silu(gu[:, :EXPERT_FF]) * gu[:, EXPERT_FF:]).astype(BF16)
        o_ref[...] = jnp.dot(hid, wdn_ref[0], preferred_element_type=F32)


def _experts(tile_e, n_used, xs, w_gu, w_dn):
    n_pad, d = xs.shape
    tm = EXPERT_TM
    n_tiles = n_pad // tm
    row_map = lambda t, te, nu: (jnp.minimum(t, nu[0] - 1), 0)
    return pl.pallas_call(
        _expert_kernel,
        out_shape=jax.ShapeDtypeStruct((n_pad, d), F32),
        grid_spec=pltpu.PrefetchScalarGridSpec(
            num_scalar_prefetch=2,
            grid=(n_tiles,),
            in_specs=[pl.BlockSpec((tm, d), row_map),
                      pl.BlockSpec((1, d, 2 * EXPERT_FF), lambda t, te, nu: (te[t], 0, 0)),
                      pl.BlockSpec((1, EXPERT_FF, d), lambda t, te, nu: (te[t], 0, 0))],
            out_specs=pl.BlockSpec((tm, d), row_map)),
        compiler_params=_cparams("arbitrary"),
    )(tile_e, n_used, xs, w_gu, w_dn)


def _final_kernel(x1_ref, h2_ref, routed_ref, gt2_ref, gpost_ref, wgu_ref, wdn_ref, o_ref):
    gu = jnp.dot(h2_ref[...], wgu_ref[...], preferred_element_type=F32)
    ff = gu.shape[1] // 2
    hid = (_silu(gu[:, :ff]) * gu[:, ff:]).astype(BF16)
    f = routed_ref[...] + jnp.dot(hid, wdn_ref[...], preferred_element_type=F32)
    o_ref[...] = x1_ref[...] + gt2_ref[0] * _rms(f, gpost_ref[...])


def _final(x1, h2, routed, gt2, g_post_ffn, w_sh_gu, w_sh_dn, rows_per_batch):
    m, d = x1.shape
    tm = FINAL_TM
    tpb = rows_per_batch // tm
    row = pl.BlockSpec((tm, d), lambda i: (i, 0))
    ff2 = w_sh_gu.shape[1]
    return pl.pallas_call(
        _final_kernel,
        out_shape=jax.ShapeDtypeStruct((m, d), F32),
        grid=(m // tm,),
        in_specs=[row, row, row, pl.BlockSpec((1, 1, d), lambda i: (i // tpb, 0, 0)),
                  pl.BlockSpec((1, d), lambda i: (0, 0)),
                  pl.BlockSpec((d, ff2), lambda i: (0, 0)), pl.BlockSpec((ff2 // 2, d), lambda i: (0, 0))],
        out_specs=row,
        compiler_params=_cparams("parallel"),
    )(x1, h2, routed, gt2, g_post_ffn, w_sh_gu, w_sh_dn)


def kernel(x, c, ctx, c_ctx, w_ada, b_ada, g_pre_mix, g_post_mix, g_pre_ffn, g_post_ffn, w_in, lam_q1, lam_k1, lam_q2, lam_k2, g_attn_subln, conv_w, conv_b, dt_bias, a_log, d_skip, g_ssd_norm, w_branch_attn, w_branch_ssd, w_out, w_router, router_bias, w_e_gate, w_e_up, w_e_down, w_sh_gate, w_sh_up, w_sh_down):
    nb, ns, d = x.shape
    nl = ctx.shape[1]
    m = nb * ns
    li = 0

    c_all = jnp.zeros((3 * SUBLANES, d), F32).at[:nb].set(c).at[nb].set(c_ctx)
    mod = _modulation(c_all, w_ada[li], b_ada[li])
    sh1, sc1, gt1, sh2, sc2, gt2 = (mod[:nb, k * d:(k + 1) * d].reshape(nb, 1, d) for k in range(6))
    sh1c, sc1c = (mod[nb:nb + 1, k * d:(k + 1) * d].reshape(1, 1, d) for k in range(2))

    qk_dim = N_HEADS * 2 * ATTN_DH
    v_dim = N_HEADS * ATTN_DV
    o_q, o_k, o_v, o_z = 0, qk_dim, 2 * qk_dim, 2 * qk_dim + v_dim
    o_xbc = o_z + D_SSD
    o_dt = o_xbc + XBC_DIM
    o_g = o_dt + 2 * SSD_HEADS
    wi = w_in[li]
    w_qkv = wi[:, o_q:o_z].astype(BF16)
    w_kv = wi[:, o_k:o_z].astype(BF16)
    w_xbc = wi[:, o_xbc:o_dt].astype(BF16)
    w_rest = jnp.concatenate([wi[:, o_xbc:o_dt], wi[:, o_z:o_xbc], wi[:, o_g:]], axis=1).astype(BF16)
    w_dt = jnp.zeros((d, LANES), BF16).at[:, :2 * SSD_HEADS].set(wi[:, o_dt:o_g].astype(BF16))

    x2d = x.reshape(m, d)
    c2d = ctx.reshape(nb * nl, d)
    g1 = g_pre_mix[li].reshape(1, d)
    qkv = _project(x2d, sh1, sc1, g1, w_qkv, BF16, rows_per_mod=ns, rope_tables=_rope_tables(ns))
    rest, dt_lat, dtT_lat = _project(x2d, sh1, sc1, g1, w_rest, F32, rows_per_mod=ns, w_dt=w_dt)
    kv_ctx = _project(c2d, sh1c, sc1c, g1, w_kv, BF16, rows_per_mod=nb * nl)
    xbc_ctx, dt_ctx, dtT_ctx = _project(c2d, sh1c, sc1c, g1, w_xbc, F32, rows_per_mod=nb * nl, w_dt=w_dt)

    lam = (jnp.exp(jnp.sum(lam_q1[li] * lam_k1[li])) - jnp.exp(jnp.sum(lam_q2[li] * lam_k2[li])) + LAM_INIT)
    oattn = _diff_attention(lam.reshape(1).astype(F32), qkv, kv_ctx, g_attn_subln[li].reshape(1, ATTN_DV), nb, ns)

    y_ssd = _ssd(rest, 0, dt_lat, dtT_lat, xbc_ctx, dt_ctx, dtT_ctx, conv_w[li], conv_b[li], dt_bias[li], a_log[li],
                 d_skip[li], nb, ns)

    x1, h2, logits_t = _merge(
        y_ssd, rest, oattn, x2d, gt1, sh2, sc2, g_ssd_norm[li].reshape(1, d), g_post_mix[li].reshape(1, d),
        g_pre_ffn[li].reshape(1, d), w_branch_attn[li].astype(BF16), w_branch_ssd[li].astype(BF16),
        w_out[li].astype(BF16), w_router[li].T, ns)

    eidx, wts, rank, counts = _route(logits_t, router_bias[li])

    tm = EXPERT_TM
    n_tiles = (m * TOP_K) // tm + N_EXPERTS
    n_pad = n_tiles * tm
    counts = counts[:, 0].astype(jnp.int32)
    padded = (counts + tm - 1) // tm * tm
    pend = jnp.cumsum(padded)
    pstart = pend - padded
    dest = pstart[eidx] + rank
    tile_e = jnp.minimum(jnp.searchsorted(pend, jnp.arange(n_tiles, dtype=jnp.int32) * tm, side='right'),
                         N_EXPERTS - 1).astype(jnp.int32)
    n_used = (pend[-1:] // tm).astype(jnp.int32)
    tok = jnp.broadcast_to(jnp.arange(m, dtype=jnp.int32)[None, :], (TOP_K, m))
    slot_tok = jnp.zeros((n_pad,), jnp.int32).at[dest.reshape(-1)].set(tok.reshape(-1))
    xs = jnp.take(h2, slot_tok, axis=0)

    w_gu = jnp.concatenate([w_e_gate[li], w_e_up[li]], axis=-1).astype(BF16)
    y_sorted = _experts(tile_e, n_used, xs, w_gu, w_e_down[li].astype(BF16))
    routed = jnp.zeros((m, d), F32)
    for k in range(TOP_K):
        routed = routed + jnp.take(y_sorted, dest[k], axis=0) * wts[k][:, None]

    w_sh_gu = jnp.concatenate([w_sh_gate[li], w_sh_up[li]], axis=-1).astype(BF16)
    out = _final(x1, h2, routed, gt2, g_post_ffn[li].reshape(1, d), w_sh_gu, w_sh_down[li].astype(BF16), ns)
    return out.reshape(nb, ns, d)
```

```python
import functools
import math

import jax
import jax.numpy as jnp
from jax import lax
from jax.experimental import pallas as pl
from jax.experimental.pallas import tpu as pltpu

F32 = jnp.float32
BF16 = jnp.bfloat16
HIGHEST = lax.Precision.HIGHEST

D_MODEL = 1024
CTX_LEN = 256
GRID_W = 64
EPS = 1e-6

N_HEADS = 8
ATTN_DH = 64
ATTN_DV = 128
ATTN_SCALE = ATTN_DH ** -0.5
ROPE_THETA = 10000.0
ROPE_AXIS_DIM = ATTN_DH // 2
LAM_INIT = 0.8 - 0.6 * math.exp(-0.3 * 0)

D_SSD = 1024
SSD_HEADDIM = 64
SSD_HEADS = 16
SSD_GROUPS = 4
SSD_HPG = 4
SSD_STATE = 128
XBC_DIM = D_SSD + 2 * SSD_GROUPS * SSD_STATE
GROUP_W = SSD_HPG * SSD_HEADDIM

N_EXPERTS = 64
EXPERT_FF = 256
TOP_K = 8
N_EXPERT_GROUPS = 8
EXPERTS_PER_GROUP = N_EXPERTS // N_EXPERT_GROUPS
TOPK_GROUPS = 4
ROUTED_SCALE = 2.5

LANES = 128
SUBLANES = 8
VMEM_LIMIT = 52 * 1024 * 1024

PROJ_TM = 1024
PROJ_TN = 1024
ATTN_TQ = 256
SSD_Q = 256
MERGE_TM = 256
ROUTE_TL = 512
EXPERT_TM = 512
FINAL_TM = 512

NEG_BIG = -1e30

_NT = (((1,), (1,)), ((), ()))
_TN = (((0,), (0,)), ((), ()))


def _cparams(*sem):
    return pltpu.CompilerParams(dimension_semantics=sem, vmem_limit_bytes=VMEM_LIMIT)


def _silu(v):
    return v * jax.nn.sigmoid(v)


def _softplus(v):
    return jnp.maximum(v, 0.0) + jnp.log1p(jnp.exp(-jnp.abs(v)))


def _rms(v, g):
    return v * lax.rsqrt(jnp.mean(v * v, axis=-1, keepdims=True) + EPS) * g


def _mod_kernel(c_ref, w_ref, b_ref, o_ref):
    o_ref[...] = jnp.dot(_silu(c_ref[...]), w_ref[...], preferred_element_type=F32,
                         precision=HIGHEST) + b_ref[...]


def _modulation(c_all, w_ada, b_ada):
    rows, d = c_all.shape
    n = w_ada.shape[1]
    tn = 1024
    return pl.pallas_call(
        _mod_kernel,
        out_shape=jax.ShapeDtypeStruct((rows, n), F32),
        grid=(n // tn,),
        in_specs=[pl.BlockSpec((rows, d), lambda j: (0, 0)),
                  pl.BlockSpec((d, tn), lambda j: (0, j)),
                  pl.BlockSpec((1, tn), lambda j: (0, j))],
        out_specs=pl.BlockSpec((rows, tn), lambda j: (0, j)),
        compiler_params=_cparams("arbitrary"),
    )(c_all, w_ada, b_ada.reshape(1, n))


def _proj_kernel(*refs, rope, has_dt):
    it = iter(refs)
    x_ref, sh_ref, sc_ref, g_ref, w_ref = (next(it) for _ in range(5))
    if rope:
        cos_ref, sin_ref = next(it), next(it)
    if has_dt:
        wdt_ref, wdtT_ref = next(it), next(it)
    o_ref = next(it)
    if has_dt:
        dt_ref, dtT_ref = next(it), next(it)
    h_scr = next(it)
    j = pl.program_id(1)

    @pl.when(j == 0)
    def _():
        h = (_rms(x_ref[...], g_ref[...]) * (1.0 + sc_ref[0]) + sh_ref[0]).astype(BF16)
        h_scr[...] = h
        if has_dt:
            dt_ref[...] = jnp.dot(h, wdt_ref[...], preferred_element_type=F32)
            dtT_ref[...] = lax.dot_general(wdtT_ref[...], h, _NT, preferred_element_type=F32)

    acc = jnp.dot(h_scr[...], w_ref[...], preferred_element_type=F32)
    if rope:
        @pl.when(j < 2)
        def _():
            qs = jnp.where(j == 0, ATTN_SCALE, 1.0).astype(F32)
            cos = cos_ref[...] * qs
            sin = sin_ref[...] * qs
            lane = lax.broadcasted_iota(jnp.int32, cos.shape, 1)
            first = (lane % ATTN_DH) < (ATTN_DH // 2)
            for s in range(acc.shape[1] // LANES):
                u = acc[:, s * LANES:(s + 1) * LANES]
                rot = jnp.where(first, pltpu.roll(u, LANES - ATTN_DH // 2, 1), pltpu.roll(u, ATTN_DH // 2, 1))
                o_ref[:, s * LANES:(s + 1) * LANES] = (u * cos + rot * sin).astype(o_ref.dtype)

        @pl.when(j >= 2)
        def _():
            o_ref[...] = acc.astype(o_ref.dtype)
    else:
        o_ref[...] = acc.astype(o_ref.dtype)


def _project(x2d, shift, scale, gain, w, out_dtype, *, rows_per_mod, rope_tables=None, w_dt=None):
    m, d = x2d.shape
    n = w.shape[1]
    tm, tn = min(PROJ_TM, rows_per_mod), PROJ_TN
    tiles_per_mod = rows_per_mod // tm
    rope = rope_tables is not None
    has_dt = w_dt is not None
    in_specs = [pl.BlockSpec((tm, d), lambda i, j: (i, 0)),
                pl.BlockSpec((1, 1, d), lambda i, j: (i // tiles_per_mod, 0, 0)),
                pl.BlockSpec((1, 1, d), lambda i, j: (i // tiles_per_mod, 0, 0)),
                pl.BlockSpec((1, d), lambda i, j: (0, 0)),
                pl.BlockSpec((d, tn), lambda i, j: (0, j))]
    args = [x2d, shift, scale, gain, w]
    out_shape = [jax.ShapeDtypeStruct((m, n), out_dtype)]
    out_specs = [pl.BlockSpec((tm, tn), lambda i, j: (i, j))]
    if rope:
        cos, sin = rope_tables
        pos_tiles = cos.shape[0] // tm
        in_specs += [pl.BlockSpec((tm, LANES), lambda i, j: (i % pos_tiles, 0))] * 2
        args += [cos, sin]
    if has_dt:
        in_specs += [pl.BlockSpec((d, LANES), lambda i, j: (0, 0)),
                     pl.BlockSpec((LANES, d), lambda i, j: (0, 0))]
        args += [w_dt, w_dt.T]
        out_shape += [jax.ShapeDtypeStruct((m, LANES), F32), jax.ShapeDtypeStruct((LANES, m), F32)]
        out_specs += [pl.BlockSpec((tm, LANES), lambda i, j: (i, 0)),
                      pl.BlockSpec((LANES, tm), lambda i, j: (0, i))]
    res = pl.pallas_call(
        functools.partial(_proj_kernel, rope=rope, has_dt=has_dt),
        out_shape=out_shape,
        grid=(m // tm, n // tn),
        in_specs=in_specs,
        out_specs=out_specs,
        scratch_shapes=[pltpu.VMEM((tm, d), BF16)],
        compiler_params=_cparams("parallel", "arbitrary"),
    )(*args)
    return res if has_dt else res[0]


def _rope_tables(n_tok):
    rows = n_tok // GRID_W
    row = jnp.repeat(jnp.arange(rows, dtype=F32), GRID_W)
    col = jnp.broadcast_to(jnp.arange(GRID_W, dtype=F32)[None, :], (rows, GRID_W)).reshape(-1)
    inv_freq = ROPE_THETA ** (-jnp.arange(0, ROPE_AXIS_DIM, 2, dtype=F32) / ROPE_AXIS_DIM)
    ang = jnp.concatenate([row[:, None] * inv_freq, col[:, None] * inv_freq], axis=-1)
    ang = jnp.concatenate([ang, ang], axis=-1)
    cos, sin = jnp.cos(ang), jnp.sin(ang)
    half = ATTN_DH // 2
    sin = jnp.concatenate([-sin[:, :half], sin[:, half:]], axis=-1)
    return jnp.concatenate([cos, cos], axis=-1), jnp.concatenate([sin, sin], axis=-1)


def _attn_kernel(lam_ref, q_ref, kc_ref, k_ref, vc_ref, v_ref, g_ref, o_ref):
    tq = q_ref.shape[0]
    q = q_ref[...]
    lane = lax.broadcasted_iota(jnp.int32, q.shape, 1)
    zero = jnp.zeros_like(q)
    qq = jnp.concatenate([jnp.where(lane < ATTN_DH, q, zero), jnp.where(lane >= ATTN_DH, q, zero)], axis=0)
    s_c = lax.dot_general(qq, kc_ref[...], _NT, preferred_element_type=F32)
    s_l = lax.dot_general(qq, k_ref[...], _NT, preferred_element_type=F32)
    m = jnp.maximum(jnp.max(s_c, axis=-1, keepdims=True), jnp.max(s_l, axis=-1, keepdims=True))
    e_c = jnp.exp(s_c - m)
    e_l = jnp.exp(s_l - m)
    inv = 1.0 / (jnp.sum(e_c, axis=-1, keepdims=True) + jnp.sum(e_l, axis=-1, keepdims=True))
    a0 = inv[:tq]
    a1 = inv[tq:] * lam_ref[0]
    w_c = (e_c[:tq] * a0 - e_c[tq:] * a1).astype(BF16)
    w_l = (e_l[:tq] * a0 - e_l[tq:] * a1).astype(BF16)
    o = (jnp.dot(w_c, vc_ref[...], preferred_element_type=F32)
         + jnp.dot(w_l, v_ref[...], preferred_element_type=F32))
    o = _rms(o, g_ref[...]) * (1.0 - LAM_INIT)
    o_ref[...] = o.astype(o_ref.dtype)


def _diff_attention(lam, qkv, kv_ctx, g_subln, nb, ns):
    tq = ATTN_TQ
    nq = ns // tq
    nl = kv_ctx.shape[0] // nb
    return pl.pallas_call(
        _attn_kernel,
        out_shape=jax.ShapeDtypeStruct((nb * ns, N_HEADS * ATTN_DV), BF16),
        grid=(nb, N_HEADS, nq),
        in_specs=[pl.BlockSpec(memory_space=pltpu.SMEM),
                  pl.BlockSpec((tq, LANES), lambda b, h, i: (b * nq + i, h)),
                  pl.BlockSpec((nl, LANES), lambda b, h, i: (b, h)),
                  pl.BlockSpec((ns, LANES), lambda b, h, i: (b, N_HEADS + h)),
                  pl.BlockSpec((nl, LANES), lambda b, h, i: (b, N_HEADS + h)),
                  pl.BlockSpec((ns, LANES), lambda b, h, i: (b, 2 * N_HEADS + h)),
                  pl.BlockSpec((1, LANES), lambda b, h, i: (0, 0))],
        out_specs=pl.BlockSpec((tq, LANES), lambda b, h, i: (b * nq + i, h)),
        compiler_params=_cparams("parallel", "parallel", "arbitrary"),
    )(lam, qkv, kv_ctx, qkv, kv_ctx, qkv, g_subln)


def _ssd_conv(xin, prev_row, next_row, cw_ref, cb_ref):
    q = xin.shape[0]
    rid = lax.broadcasted_iota(jnp.int32, xin.shape, 0)
    up = jnp.where(rid == 0, prev_row, pltpu.roll(xin, 1, 0))
    dn = jnp.where(rid == q - 1, next_row, pltpu.roll(xin, q - 1, 0))
    return _silu(cb_ref[...] + up * cw_ref[0:1, :] + xin * cw_ref[1:2, :] + dn * cw_ref[2:3, :])


def _split_dot(v, e_bf16):
    hi = v.astype(BF16)
    lo = (v - hi.astype(F32)).astype(BF16)
    return (jnp.dot(hi, e_bf16, preferred_element_type=F32)
            + jnp.dot(lo, e_bf16, preferred_element_type=F32))


def _ssd_chunk(u, dt_blk, dtT_blk, bias_row, bias_col, alog_row, alog_col, is_f, states, want_y):
    q = u.shape[0]
    nh = SSD_HEADS
    xs = u[:, :D_SSD]
    bm = u[:, D_SSD:D_SSD + SSD_GROUPS * SSD_STATE]
    cm = u[:, D_SSD + SSD_GROUPS * SSD_STATE:]

    dt_all = _softplus(dt_blk + bias_row)
    dtT_all = _softplus(dtT_blk + bias_col)
    dt = jnp.where(is_f, dt_all[:, 0:nh], dt_all[:, nh:2 * nh])
    dtT = jnp.where(is_f, dtT_all[0:nh, :], dtT_all[nh:2 * nh, :])
    a_r = -jnp.exp(jnp.where(is_f, alog_row[0:1, :], alog_row[1:2, :]))
    a_c = -jnp.exp(jnp.where(is_f, alog_col[:, 0:1], alog_col[:, 1:2]))

    ri = lax.broadcasted_iota(jnp.int32, (q, q), 0)
    ci = lax.broadcasted_iota(jnp.int32, (q, q), 1)
    lower = (ri >= ci).astype(F32)
    upper = (ri <= ci).astype(F32)
    tmat = jnp.where(is_f, lower, upper)
    tmat_t = jnp.where(is_f, upper, lower)
    causal = tmat > 0.5
    acum = jnp.dot(tmat, dt * a_r, preferred_element_type=F32, precision=HIGHEST)
    acum_t = jnp.dot(dtT * a_c, tmat_t, preferred_element_type=F32, precision=HIGHEST)
    tot = jnp.where(is_f, acum[q - 1:q, :], acum[0:1, :])
    wgt = jnp.exp(tot - acum) * dt

    hid = lax.broadcasted_iota(jnp.int32, (nh, D_SSD), 0)
    cid = lax.broadcasted_iota(jnp.int32, (nh, D_SSD), 1)
    expand = (cid // SSD_HEADDIM == hid).astype(BF16)
    pieces = [wgt, jnp.broadcast_to(jnp.exp(tot), (SUBLANES, nh))]
    if want_y:
        pieces += [dt, jnp.exp(acum)]
    wide = _split_dot(jnp.concatenate(pieces, axis=0), expand)
    wgt_x = wide[0:q]
    dec_x = wide[q:q + 1]
    if want_y:
        dt_x = wide[q + SUBLANES:2 * q + SUBLANES]
        ea_x = wide[2 * q + SUBLANES:3 * q + SUBLANES]

    ys, new_states = [], []
    for g in range(SSD_GROUPS):
        gsl = slice(g * GROUP_W, (g + 1) * GROUP_W)
        nsl = slice(g * SSD_STATE, (g + 1) * SSD_STATE)
        xg = xs[:, gsl]
        bm_g = bm[:, nsl].astype(BF16)
        upd = lax.dot_general(bm_g, (xg * wgt_x[:, gsl]).astype(BF16), _TN, preferred_element_type=F32)
        new_states.append(upd if states is None else states[g] * dec_x[:, gsl] + upd)
        if not want_y:
            continue
        cm_g = cm[:, nsl].astype(BF16)
        cb = lax.dot_general(cm_g, bm_g, _NT, preferred_element_type=F32)
        xdt = (xg * dt_x[:, gsl]).astype(BF16)
        y_heads = []
        for r in range(SSD_HPG):
            h = g * SSD_HPG + r
            seg = acum[:, h:h + 1] - acum_t[h:h + 1, :]
            m_h = (cb * jnp.exp(jnp.where(causal, seg, NEG_BIG))).astype(BF16)
            y_heads.append(jnp.dot(m_h, xdt[:, r * SSD_HEADDIM:(r + 1) * SSD_HEADDIM], preferred_element_type=F32))
        y_g = jnp.concatenate(y_heads, axis=1)
        if states is not None:
            y_g = y_g + jnp.dot(cm_g, states[g].astype(BF16), preferred_element_type=F32) * ea_x[:, gsl]
        ys.append(y_g)
    return (jnp.concatenate(ys, axis=1) if want_y else None), new_states


def _ssd_ctx_kernel(x_ref, dt_ref, dtT_ref, cw_ref, cb_ref, brow_ref, bcol_ref, arow_ref, acol_ref, st_ref):
    is_f = pl.program_id(1) == 0
    xin = x_ref[...]
    zero_row = jnp.zeros((1, xin.shape[1]), F32)
    u = _ssd_conv(xin, zero_row, zero_row, cw_ref, cb_ref)
    _, new_states = _ssd_chunk(u, dt_ref[...], dtT_ref[...], brow_ref[...], bcol_ref[...], arow_ref[...],
                               acol_ref[...], is_f, None, False)
    for g in range(SSD_GROUPS):
        st_ref[0, 0, g] = new_states[g]


def _ssd_lat_kernel(x_ref, prev_ref, next_ref, dt_ref, dtT_ref, h0_ref, cw_ref, cb_ref, brow_ref, bcol_ref,
                    arow_ref, acol_ref, dskip_ref, o_ref, st_scr, *, n_chunks):
    d = pl.program_id(1)
    s = pl.program_id(2)
    is_f = d == 0
    chunk = jnp.where(is_f, s, n_chunks - 1 - s)
    q = x_ref.shape[0]

    @pl.when(s == 0)
    def _():
        st_scr[...] = h0_ref[0, 0]

    xin = x_ref[...]
    prev_row = jnp.where(chunk == 0, 0.0, prev_ref[SUBLANES - 1:SUBLANES, :])
    next_row = jnp.where(chunk == n_chunks - 1, 0.0, next_ref[0:1, :])
    u = _ssd_conv(xin, prev_row, next_row, cw_ref, cb_ref)
    states = [st_scr[g] for g in range(SSD_GROUPS)]
    y, new_states = _ssd_chunk(u, dt_ref[...], dtT_ref[...], brow_ref[...], bcol_ref[...], arow_ref[...],
                               acol_ref[...], is_f, states, True)
    for g in range(SSD_GROUPS):
        st_scr[g] = new_states[g]
    row0 = pl.multiple_of(chunk * q, q)

    @pl.when(is_f)
    def _():
        o_ref[pl.ds(row0, q), :] = y + dskip_ref[...] * u[:, :D_SSD]

    @pl.when(jnp.logical_not(is_f))
    def _():
        o_ref[pl.ds(row0, q), :] += y


def _ssd(xbc_lat_src, lat_col_block, dt_lat, dtT_lat, xbc_ctx, dt_ctx, dtT_ctx, conv_w, conv_b, dt_bias, a_log,
         d_skip, nb, ns):
    q = SSD_Q
    nl = xbc_ctx.shape[0] // nb
    nh = SSD_HEADS
    brow = jnp.zeros((1, LANES), F32).at[0, :2 * nh].set(dt_bias.reshape(-1))
    bcol = brow.reshape(LANES, 1)
    arow = a_log.reshape(2, nh)
    acol = arow.T
    cb2 = conv_b.reshape(1, XBC_DIM)
    const2 = lambda *_: (0, 0)
    param_specs = [pl.BlockSpec((3, XBC_DIM), const2), pl.BlockSpec((1, XBC_DIM), const2),
                   pl.BlockSpec((1, LANES), const2), pl.BlockSpec((LANES, 1), const2),
                   pl.BlockSpec((2, nh), const2), pl.BlockSpec((nh, 2), const2)]
    params = [conv_w, cb2, brow, bcol, arow, acol]
    state_shape = (nb, 2, SSD_GROUPS, SSD_STATE, GROUP_W)

    h0 = pl.pallas_call(
        _ssd_ctx_kernel,
        out_shape=jax.ShapeDtypeStruct(state_shape, F32),
        grid=(nb, 2),
        in_specs=[pl.BlockSpec((nl, XBC_DIM), lambda b, d: (b, 0)),
                  pl.BlockSpec((nl, LANES), lambda b, d: (b, 0)),
                  pl.BlockSpec((LANES, nl), lambda b, d: (0, b))] + param_specs,
        out_specs=pl.BlockSpec((1, 1, SSD_GROUPS, SSD_STATE, GROUP_W), lambda b, d: (b, d, 0, 0, 0)),
        compiler_params=_cparams("parallel", "arbitrary"),
    )(xbc_ctx, dt_ctx, dtT_ctx, *params)

    nc = ns // q
    halo_per_chunk = q // SUBLANES
    n_halo = nb * ns // SUBLANES

    def chunk_of(d, s):
        return jnp.where(d == 0, s, nc - 1 - s)

    def prev_map(b, d, s):
        return (jnp.maximum((b * nc + chunk_of(d, s)) * halo_per_chunk - 1, 0), lat_col_block)

    def next_map(b, d, s):
        return (jnp.minimum((b * nc + chunk_of(d, s) + 1) * halo_per_chunk, n_halo - 1), lat_col_block)

    return pl.pallas_call(
        functools.partial(_ssd_lat_kernel, n_chunks=nc),
        out_shape=jax.ShapeDtypeStruct((nb * ns, D_SSD), F32),
        grid=(nb, 2, nc),
        in_specs=[pl.BlockSpec((q, XBC_DIM), lambda b, d, s: (b * nc + chunk_of(d, s), lat_col_block)),
                  pl.BlockSpec((SUBLANES, XBC_DIM), prev_map),
                  pl.BlockSpec((SUBLANES, XBC_DIM), next_map),
                  pl.BlockSpec((q, LANES), lambda b, d, s: (b * nc + chunk_of(d, s), 0)),
                  pl.BlockSpec((LANES, q), lambda b, d, s: (0, b * nc + chunk_of(d, s))),
                  pl.BlockSpec((1, 1, SSD_GROUPS, SSD_STATE, GROUP_W), lambda b, d, s: (b, d, 0, 0, 0))]
        + param_specs + [pl.BlockSpec((1, D_SSD), const2)],
        out_specs=pl.BlockSpec((ns, D_SSD), lambda b, d, s: (b, 0)),
        scratch_shapes=[pltpu.VMEM((SSD_GROUPS, SSD_STATE, GROUP_W), F32)],
        compiler_params=_cparams("parallel", "arbitrary", "arbitrary"),
    )(xbc_lat_src, xbc_lat_src, xbc_lat_src, dt_lat, dtT_lat, h0, *params,
      jnp.repeat(d_skip, SSD_HEADDIM).reshape(1, D_SSD))


def _merge_kernel(y_ref, z_ref, ga_ref, gs_ref, oa_ref, x_ref, gt1_ref, sh2_ref, sc2_ref, gssd_ref, gpost_ref,
                  gpre_ref, wba_ref, wbs_ref, wout_ref, wrt_ref, x1_ref, h2_ref, lg_ref):
    y = y_ref[...] * _silu(z_ref[...])
    gw = D_SSD // SSD_GROUPS
    y = jnp.concatenate(
        [y[:, g * gw:(g + 1) * gw]
         * lax.rsqrt(jnp.mean(y[:, g * gw:(g + 1) * gw] * y[:, g * gw:(g + 1) * gw], axis=-1, keepdims=True) + EPS)
         for g in range(SSD_GROUPS)], axis=1)
    y = (y * gssd_ref[...]).astype(BF16)
    ys = jnp.dot(y, wbs_ref[...], preferred_element_type=F32)
    ya = jnp.dot(oa_ref[...], wba_ref[...], preferred_element_type=F32)
    mixed = (jax.nn.sigmoid(ga_ref[...]) * ya + jax.nn.sigmoid(gs_ref[...]) * ys).astype(BF16)
    mix = jnp.dot(mixed, wout_ref[...], preferred_element_type=F32)
    x1 = x_ref[...] + gt1_ref[0] * _rms(mix, gpost_ref[...])
    x1_ref[...] = x1
    h2 = _rms(x1, gpre_ref[...]) * (1.0 + sc2_ref[0]) + sh2_ref[0]
    h2_ref[...] = h2.astype(BF16)
    lg_ref[...] = lax.dot_general(wrt_ref[...], h2, _NT, preferred_element_type=F32, precision=HIGHEST)


def _merge(y_ssd, rest, oattn, x2d, gt1, sh2, sc2, g_ssd_norm, g_post_mix, g_pre_ffn, wba, wbs, wout, w_router_t,
           rows_per_batch):
    m, d = x2d.shape
    tm = MERGE_TM
    tpb = rows_per_batch // tm
    row = lambda c: pl.BlockSpec((tm, d), lambda i: (i, c))
    per_batch = pl.BlockSpec((1, 1, d), lambda i: (i // tpb, 0, 0))
    vec = pl.BlockSpec((1, d), lambda i: (0, 0))
    mat = pl.BlockSpec((d, d), lambda i: (0, 0))
    return pl.pallas_call(
        _merge_kernel,
        out_shape=[jax.ShapeDtypeStruct((m, d), F32), jax.ShapeDtypeStruct((m, d), BF16),
                   jax.ShapeDtypeStruct((N_EXPERTS, m), F32)],
        grid=(m // tm,),
        in_specs=[row(0), row(2), row(3), row(4), row(0), row(0), per_batch, per_batch, per_batch, vec, vec, vec,
                  mat, mat, mat, pl.BlockSpec((N_EXPERTS, d), lambda i: (0, 0))],
        out_specs=[row(0), row(0), pl.BlockSpec((N_EXPERTS, tm), lambda i: (0, i))],
        compiler_params=_cparams("parallel"),
    )(y_ssd, rest, rest, rest, oattn, x2d, gt1, sh2, sc2, g_ssd_norm, g_post_mix, g_pre_ffn, wba, wbs, wout,
      w_router_t)


def _first_index(hit, iota, limit):
    return jnp.min(jnp.where(hit, iota, limit), axis=0, keepdims=True)


def _route_kernel(lg_ref, bias_ref, eidx_ref, wts_ref, rank_ref, cnt_ref, carry):
    i = pl.program_id(0)
    tl = lg_ref.shape[1]

    @pl.when(i == 0)
    def _():
        carry[...] = jnp.zeros_like(carry)

    scores = jax.nn.sigmoid(lg_ref[...])
    sel = scores + bias_ref[...]
    epg = EXPERTS_PER_GROUP
    io8 = lax.broadcasted_iota(jnp.int32, (epg, tl), 0)
    neg_inf = -jnp.inf

    grp = []
    for g in range(N_EXPERT_GROUPS):
        blk = sel[g * epg:(g + 1) * epg]
        m1 = jnp.max(blk, axis=0, keepdims=True)
        rest = jnp.where(io8 == _first_index(blk == m1, io8, epg), neg_inf, blk)
        grp.append(m1 + jnp.max(rest, axis=0, keepdims=True))
    gsc = jnp.concatenate(grp, axis=0)
    gkeep = jnp.zeros_like(gsc)
    for _ in range(TOPK_GROUPS):
        pick = io8 == _first_index(gsc == jnp.max(gsc, axis=0, keepdims=True), io8, N_EXPERT_GROUPS)
        gkeep = jnp.where(pick, 1.0, gkeep)
        gsc = jnp.where(pick, neg_inf, gsc)
    cand = jnp.concatenate(
        [jnp.where(gkeep[g:g + 1] > 0.5, sel[g * epg:(g + 1) * epg], neg_inf) for g in range(N_EXPERT_GROUPS)], axis=0)

    ioe = lax.broadcasted_iota(jnp.int32, (N_EXPERTS, tl), 0)
    picks, pscore = [], []
    chosen = jnp.zeros_like(cand)
    for _ in range(TOP_K):
        e_k = _first_index(cand == jnp.max(cand, axis=0, keepdims=True), ioe, N_EXPERTS)
        hit = ioe == e_k
        picks.append(e_k)
        pscore.append(jnp.sum(jnp.where(hit, scores, 0.0), axis=0, keepdims=True))
        chosen = jnp.where(hit, 1.0, chosen)
        cand = jnp.where(hit, neg_inf, cand)
    wsum = pscore[0]
    for k in range(1, TOP_K):
        wsum = wsum + pscore[k]

    ti = lax.broadcasted_iota(jnp.int32, (tl, tl), 0)
    tj = lax.broadcasted_iota(jnp.int32, (tl, tl), 1)
    before = (ti < tj).astype(BF16)
    rank = jnp.dot(chosen.astype(BF16), before, preferred_element_type=F32) + carry[...]
    total = carry[...] + jnp.sum(chosen, axis=1, keepdims=True)
    carry[...] = total
    cnt_ref[...] = jnp.broadcast_to(total, cnt_ref.shape)

    eidx_ref[...] = jnp.concatenate(picks, axis=0)
    wts_ref[...] = jnp.concatenate([p / wsum * ROUTED_SCALE for p in pscore], axis=0)
    rank_ref[...] = jnp.concatenate(
        [jnp.sum(jnp.where(ioe == e_k, rank, 0.0), axis=0, keepdims=True) for e_k in picks], axis=0).astype(jnp.int32)


def _route(logits_t, router_bias):
    ne, m = logits_t.shape
    tl = ROUTE_TL
    tok = lambda dt: jax.ShapeDtypeStruct((TOP_K, m), dt)
    tok_spec = pl.BlockSpec((TOP_K, tl), lambda i: (0, i))
    return pl.pallas_call(
        _route_kernel,
        out_shape=[tok(jnp.int32), tok(F32), tok(jnp.int32), jax.ShapeDtypeStruct((ne, LANES), F32)],
        grid=(m // tl,),
        in_specs=[pl.BlockSpec((ne, tl), lambda i: (0, i)), pl.BlockSpec((ne, 1), lambda i: (0, 0))],
        out_specs=[tok_spec, tok_spec, tok_spec, pl.BlockSpec((ne, LANES), lambda i: (0, 0))],
        scratch_shapes=[pltpu.VMEM((ne, 1), F32)],
        compiler_params=_cparams("arbitrary"),
    )(logits_t, router_bias.reshape(ne, 1))


def _expert_kernel(tile_ref, e_ref, lo_ref, hi_ref, x_ref, wg_ref, wu_ref, wd_ref, o_ref):
    w = pl.program_id(0)
    lo, hi = lo_ref[w], hi_ref[w]
    tm = x_ref.shape[0]
    base = tile_ref[w] * tm

    @pl.when(hi > lo)
    def _():
        x = x_ref[...]
        gate = jnp.dot(x, wg_ref[0].astype(BF16), preferred_element_type=F32)
        up = jnp.dot(x, wu_ref[0].astype(BF16), preferred_element_type=F32)
        y = jnp.dot((_silu(gate) * up).astype(BF16), wd_ref[0].astype(BF16), preferred_element_type=F32)
        rid = base + lax.broadcasted_iota(jnp.int32, (tm, 1), 0)
        mine = (rid >= lo) & (rid < hi)

        @pl.when(lo == base)
        def _():
            o_ref[...] = jnp.where(mine, y, 0.0).astype(o_ref.dtype)

        @pl.when(lo != base)
        def _():
            o_ref[...] = jnp.where(mine, y.astype(o_ref.dtype), o_ref[...])


def _experts(item_tile, item_e, item_lo, item_hi, xs, w_gate, w_up, w_down):
    n_rows, d = xs.shape
    tm = EXPERT_TM
    ff = w_gate.shape[-1]
    row_map = lambda w, t, e, lo, hi: (t[w], 0)
    return pl.pallas_call(
        _expert_kernel,
        out_shape=jax.ShapeDtypeStruct((n_rows, d), BF16),
        grid_spec=pltpu.PrefetchScalarGridSpec(
            num_scalar_prefetch=4,
            grid=(item_tile.shape[0],),
            in_specs=[pl.BlockSpec((tm, d), row_map),
                      pl.BlockSpec((1, d, ff), lambda w, t, e, lo, hi: (e[w], 0, 0)),
                      pl.BlockSpec((1, d, ff), lambda w, t, e, lo, hi: (e[w], 0, 0)),
                      pl.BlockSpec((1, ff, d), lambda w, t, e, lo, hi: (e[w], 0, 0))],
            out_specs=pl.BlockSpec((tm, d), row_map)),
        compiler_params=_cparams("arbitrary"),
    )(item_tile, item_e, item_lo, item_hi, xs, w_gate, w_up, w_down)


def _final_kernel(x1_ref, h2_ref, yk_ref, wt_ref, gt2_ref, gpost_ref, wg_ref, wu_ref, wd_ref, o_ref):
    h2 = h2_ref[...]
    gate = jnp.dot(h2, wg_ref[...], preferred_element_type=F32)
    up = jnp.dot(h2, wu_ref[...], preferred_element_type=F32)
    f = jnp.dot((_silu(gate) * up).astype(BF16), wd_ref[...], preferred_element_type=F32)
    wt = wt_ref[...]
    for k in range(TOP_K):
        f = f + yk_ref[k].astype(F32) * wt[:, k:k + 1]
    o_ref[...] = x1_ref[...] + gt2_ref[0] * _rms(f, gpost_ref[...])


def _final(x1, h2, y_tok, wts_t, gt2, g_post_ffn, w_sh_gate, w_sh_up, w_sh_down, rows_per_batch):
    m, d = x1.shape
    tm = FINAL_TM
    tpb = rows_per_batch // tm
    row = pl.BlockSpec((tm, d), lambda i: (i, 0))
    ff = w_sh_gate.shape[1]
    return pl.pallas_call(
        _final_kernel,
        out_shape=jax.ShapeDtypeStruct((m, d), F32),
        grid=(m // tm,),
        in_specs=[row, row, pl.BlockSpec((TOP_K, tm, d), lambda i: (0, i, 0)),
                  pl.BlockSpec((tm, TOP_K), lambda i: (i, 0)),
                  pl.BlockSpec((1, 1, d), lambda i: (i // tpb, 0, 0)),
                  pl.BlockSpec((1, d), lambda i: (0, 0)),
                  pl.BlockSpec((d, ff), lambda i: (0, 0)), pl.BlockSpec((d, ff), lambda i: (0, 0)),
                  pl.BlockSpec((ff, d), lambda i: (0, 0))],
        out_specs=row,
        compiler_params=_cparams("parallel"),
    )(x1, h2, y_tok, wts_t, gt2, g_post_ffn, w_sh_gate, w_sh_up, w_sh_down)


def kernel(x, c, ctx, c_ctx, w_ada, b_ada, g_pre_mix, g_post_mix, g_pre_ffn, g_post_ffn, w_in, lam_q1, lam_k1, lam_q2, lam_k2, g_attn_subln, conv_w, conv_b, dt_bias, a_log, d_skip, g_ssd_norm, w_branch_attn, w_branch_ssd, w_out, w_router, router_bias, w_e_gate, w_e_up, w_e_down, w_sh_gate, w_sh_up, w_sh_down):
    nb, ns, d = x.shape
    nl = ctx.shape[1]
    m = nb * ns
    li = 0

    c_all = jnp.zeros((3 * SUBLANES, d), F32).at[:nb].set(c).at[nb].set(c_ctx)
    mod = _modulation(c_all, w_ada[li], b_ada[li])
    sh1, sc1, gt1, sh2, sc2, gt2 = (mod[:nb, k * d:(k + 1) * d].reshape(nb, 1, d) for k in range(6))
    sh1c, sc1c = (mod[nb:nb + 1, k * d:(k + 1) * d].reshape(1, 1, d) for k in range(2))

    qk_dim = N_HEADS * 2 * ATTN_DH
    v_dim = N_HEADS * ATTN_DV
    o_q, o_k, o_v, o_z = 0, qk_dim, 2 * qk_dim, 2 * qk_dim + v_dim
    o_xbc = o_z + D_SSD
    o_dt = o_xbc + XBC_DIM
    o_g = o_dt + 2 * SSD_HEADS
    wi = w_in[li]
    w_qkv = wi[:, o_q:o_z].astype(BF16)
    w_kv = wi[:, o_k:o_z].astype(BF16)
    w_xbc = wi[:, o_xbc:o_dt].astype(BF16)
    w_rest = jnp.concatenate([wi[:, o_xbc:o_dt], wi[:, o_z:o_xbc], wi[:, o_g:]], axis=1).astype(BF16)
    w_dt = jnp.zeros((d, LANES), BF16).at[:, :2 * SSD_HEADS].set(wi[:, o_dt:o_g].astype(BF16))

    x2d = x.reshape(m, d)
    c2d = ctx.reshape(nb * nl, d)
    g1 = g_pre_mix[li].reshape(1, d)
    qkv = _project(x2d, sh1, sc1, g1, w_qkv, BF16, rows_per_mod=ns, rope_tables=_rope_tables(ns))
    rest, dt_lat, dtT_lat = _project(x2d, sh1, sc1, g1, w_rest, F32, rows_per_mod=ns, w_dt=w_dt)
    kv_ctx = _project(c2d, sh1c, sc1c, g1, w_kv, BF16, rows_per_mod=nb * nl)
    xbc_ctx, dt_ctx, dtT_ctx = _project(c2d, sh1c, sc1c, g1, w_xbc, F32, rows_per_mod=nb * nl, w_dt=w_dt)

    lam = (jnp.exp(jnp.sum(lam_q1[li] * lam_k1[li])) - jnp.exp(jnp.sum(lam_q2[li] * lam_k2[li])) + LAM_INIT)
    oattn = _diff_attention(lam.reshape(1).astype(F32), qkv, kv_ctx, g_attn_subln[li].reshape(1, ATTN_DV), nb, ns)

    y_ssd = _ssd(rest, 0, dt_lat, dtT_lat, xbc_ctx, dt_ctx, dtT_ctx, conv_w[li], conv_b[li], dt_bias[li], a_log[li],
                 d_skip[li], nb, ns)

    x1, h2, logits_t = _merge(
        y_ssd, rest, oattn, x2d, gt1, sh2, sc2, g_ssd_norm[li].reshape(1, d), g_post_mix[li].reshape(1, d),
        g_pre_ffn[li].reshape(1, d), w_branch_attn[li].astype(BF16), w_branch_ssd[li].astype(BF16),
        w_out[li].astype(BF16), w_router[li].T, ns)

    eidx, wts, rank, counts = _route(logits_t, router_bias[li])

    tm = EXPERT_TM
    n_rows = m * TOP_K
    n_tiles = n_rows // tm
    counts = counts[:, 0].astype(jnp.int32)
    ends = jnp.cumsum(counts)
    start = ends - counts
    eid = jnp.arange(N_EXPERTS, dtype=jnp.int32)
    dest = jnp.sum(jnp.where(eidx[None] == eid[:, None, None], start[:, None, None], 0), axis=0) + rank
    tok = jnp.arange(m, dtype=jnp.int32)
    slot_tok = jnp.sort((eidx * m + tok[None, :]).reshape(-1)) % m
    xs = h2.at[slot_tok].get(mode="promise_in_bounds")

    cuts = jnp.sort(jnp.concatenate([jnp.arange(n_tiles, dtype=jnp.int32) * tm, start[1:]]))
    item_hi = jnp.concatenate([cuts[1:], jnp.full((1,), n_rows, jnp.int32)])
    item_tile = jnp.minimum(cuts // tm, n_tiles - 1)
    item_e = jnp.minimum(jnp.sum((ends[None, :] <= cuts[:, None]).astype(jnp.int32), axis=1), N_EXPERTS - 1)
    y_sorted = _experts(item_tile, item_e, cuts, item_hi, xs, w_e_gate[li], w_e_up[li], w_e_down[li])

    y_tok = y_sorted.at[dest.reshape(-1)].get(mode="promise_in_bounds").reshape(TOP_K, m, d)
    out = _final(x1, h2, y_tok, wts.T, gt2, g_post_ffn[li].reshape(1, d), w_sh_gate[li].astype(BF16),
                 w_sh_up[li].astype(BF16), w_sh_down[li].astype(BF16), ns)
    return out.reshape(nb, ns, d)
```

```python
import functools
import math

import jax
import jax.numpy as jnp
from jax import lax
from jax.experimental import pallas as pl
from jax.experimental.pallas import tpu as pltpu

F32 = jnp.float32
BF16 = jnp.bfloat16
HIGHEST = lax.Precision.HIGHEST

D_MODEL = 1024
CTX_LEN = 256
GRID_W = 64
EPS = 1e-6

N_HEADS = 8
ATTN_DH = 64
ATTN_DV = 128
ATTN_SCALE = ATTN_DH ** -0.5
LOG2E = math.log2(math.e)
ROPE_THETA = 10000.0
ROPE_AXIS_DIM = ATTN_DH // 2
LAM_INIT = 0.8 - 0.6 * math.exp(-0.3 * 0)

D_SSD = 1024
SSD_HEADDIM = 64
SSD_HEADS = 16
SSD_GROUPS = 4
SSD_HPG = 4
SSD_STATE = 128
XBC_DIM = D_SSD + 2 * SSD_GROUPS * SSD_STATE
GROUP_W = SSD_HPG * SSD_HEADDIM

N_EXPERTS = 64
EXPERT_FF = 256
TOP_K = 8
N_EXPERT_GROUPS = 8
EXPERTS_PER_GROUP = N_EXPERTS // N_EXPERT_GROUPS
TOPK_GROUPS = 4
ROUTED_SCALE = 2.5

LANES = 128
SUBLANES = 8
BF16_SUBLANES = 16
VMEM_LIMIT = 52 * 1024 * 1024

PROJ_TM = 512
PROJ_TN = 512
ATTN_TQ = 1024
ATTN_TH = 256
ATTN_KB = 256
SSD_Q = 256
MERGE_TM = 512
ROUTE_TL = 512
EXPERT_TM = 512
FINAL_TM = 512

NEG_BIG = -1e30

_NT = (((1,), (1,)), ((), ()))
_TN = (((0,), (0,)), ((), ()))


def _cparams(*sem):
    return pltpu.CompilerParams(dimension_semantics=sem, vmem_limit_bytes=VMEM_LIMIT)


def _sigmoid(v):
    return 0.5 * jnp.tanh(0.5 * v) + 0.5


def _silu(v):
    return v * _sigmoid(v)


def _softplus(v):
    return jnp.maximum(v, 0.0) + jnp.log1p(jnp.exp(-jnp.abs(v)))


def _rms(v, g):
    return v * lax.rsqrt(jnp.mean(v * v, axis=-1, keepdims=True) + EPS) * g


def _split_bf16(v):
    hi = v.astype(BF16)
    return hi, (v - hi.astype(F32)).astype(BF16)


def _mod_kernel(c_ref, w_ref, b_ref, o_ref):
    o_ref[...] = jnp.dot(_silu(c_ref[...]), w_ref[...], preferred_element_type=F32,
                         precision=HIGHEST) + b_ref[...]


def _modulation(c_all, w_ada, b_ada):
    rows, d = c_all.shape
    n = w_ada.shape[1]
    tn = 1024
    return pl.pallas_call(
        _mod_kernel,
        out_shape=jax.ShapeDtypeStruct((rows, n), F32),
        grid=(n // tn,),
        in_specs=[pl.BlockSpec((rows, d), lambda j: (0, 0)),
                  pl.BlockSpec((d, tn), lambda j: (0, j)),
                  pl.BlockSpec((1, tn), lambda j: (0, j))],
        out_specs=pl.BlockSpec((rows, tn), lambda j: (0, j)),
        compiler_params=_cparams("arbitrary"),
    )(c_all, w_ada, b_ada.reshape(1, n))


def _proj_kernel(*refs, n_rot, n_q, has_dt):
    it = iter(refs)
    x_ref, sh_ref, sc_ref, g_ref, w_ref = (next(it) for _ in range(5))
    if n_rot:
        wrot_ref, cos_ref, sin_ref = next(it), next(it), next(it)
    if has_dt:
        wdt_ref, wdtT_ref = next(it), next(it)
    o_ref = next(it)
    if has_dt:
        dt_ref, dtT_ref = next(it), next(it)

    h = (_rms(x_ref[...], g_ref[...]) * (1.0 + sc_ref[0]) + sh_ref[0]).astype(BF16)
    if has_dt:
        dt_ref[...] = jnp.dot(h, wdt_ref[...], preferred_element_type=F32)
        dtT_ref[...] = lax.dot_general(wdtT_ref[...], h, _NT, preferred_element_type=F32)
    tn = PROJ_TN
    for c in range(w_ref.shape[1] // tn):
        cs = slice(c * tn, (c + 1) * tn)
        u = jnp.dot(h, w_ref[:, cs], preferred_element_type=F32)
        if c * tn < n_rot:
            ur = jnp.dot(h, wrot_ref[:, cs], preferred_element_type=F32)
            scale = ATTN_SCALE * LOG2E if c * tn < n_q else 1.0
            cos = cos_ref[...] * scale
            sin = sin_ref[...] * scale
            for s in range(tn // LANES):
                sl = slice(s * LANES, (s + 1) * LANES)
                o_ref[:, c * tn + s * LANES:c * tn + (s + 1) * LANES] = (
                    u[:, sl] * cos + ur[:, sl] * sin).astype(o_ref.dtype)
        else:
            o_ref[:, cs] = u.astype(o_ref.dtype)


def _project(x2d, shift, scale, gain, w, *, rows_per_mod, rope=None, w_dt=None):
    m, d = x2d.shape
    n = w.shape[1]
    tm = min(PROJ_TM, rows_per_mod)
    tiles_per_mod = rows_per_mod // tm
    has_dt = w_dt is not None
    const = lambda i: (0, 0)
    in_specs = [pl.BlockSpec((tm, d), lambda i: (i, 0)),
                pl.BlockSpec((1, 1, d), lambda i: (i // tiles_per_mod, 0, 0)),
                pl.BlockSpec((1, 1, d), lambda i: (i // tiles_per_mod, 0, 0)),
                pl.BlockSpec((1, d), const),
                pl.BlockSpec((d, n), const)]
    args = [x2d, shift, scale, gain, w]
    out_shape = [jax.ShapeDtypeStruct((m, n), BF16)]
    out_specs = [pl.BlockSpec((tm, n), lambda i: (i, 0))]
    n_rot = n_q = 0
    if rope is not None:
        w_rot, cos, sin, n_q = rope
        n_rot = w_rot.shape[1]
        pos_tiles = cos.shape[0] // tm
        in_specs += [pl.BlockSpec((d, n_rot), const)] + [pl.BlockSpec((tm, LANES), lambda i: (i % pos_tiles, 0))] * 2
        args += [w_rot, cos, sin]
    if has_dt:
        in_specs += [pl.BlockSpec((d, LANES), const), pl.BlockSpec((LANES, d), const)]
        args += [w_dt, w_dt.T]
        out_shape += [jax.ShapeDtypeStruct((m, LANES), F32), jax.ShapeDtypeStruct((LANES, m), F32)]
        out_specs += [pl.BlockSpec((tm, LANES), lambda i: (i, 0)), pl.BlockSpec((LANES, tm), lambda i: (0, i))]
    res = pl.pallas_call(
        functools.partial(_proj_kernel, n_rot=n_rot, n_q=n_q, has_dt=has_dt),
        out_shape=out_shape,
        grid=(m // tm,),
        in_specs=in_specs,
        out_specs=out_specs,
        compiler_params=_cparams("parallel"),
    )(*args)
    return res if has_dt else res[0]


def _rope_tables(n_tok):
    rows = n_tok // GRID_W
    row = jnp.repeat(jnp.arange(rows, dtype=F32), GRID_W)
    col = jnp.broadcast_to(jnp.arange(GRID_W, dtype=F32)[None, :], (rows, GRID_W)).reshape(-1)
    inv_freq = ROPE_THETA ** (-jnp.arange(0, ROPE_AXIS_DIM, 2, dtype=F32) / ROPE_AXIS_DIM)
    ang = jnp.concatenate([row[:, None] * inv_freq, col[:, None] * inv_freq], axis=-1)
    ang = jnp.concatenate([ang, ang, ang, ang], axis=-1)
    return jnp.cos(ang), jnp.sin(ang)


def _rotate_half_columns(w):
    d, n = w.shape
    half = ATTN_DH // 2
    wc = w.reshape(d, n // ATTN_DH, 2, half)
    return jnp.stack([-wc[:, :, 1], wc[:, :, 0]], axis=2).reshape(d, n)


def _attn_kernel(lam_ref, q_ref, kc_ref, k_ref, vc_ref, v_ref, g_ref, o_ref, s_scr, e_scr):
    th, kb = ATTN_TH, ATTN_KB
    nsub = q_ref.shape[0] // th
    lam = lam_ref[0]
    blocks = ([(kc_ref, vc_ref, o) for o in range(0, kc_ref.shape[0], kb)]
              + [(k_ref, v_ref, o) for o in range(0, k_ref.shape[0], kb)])

    def stacked_q(t):
        q = q_ref[t * th:(t + 1) * th, :]
        lane = lax.broadcasted_iota(jnp.int32, q.shape, 1)
        zero = jnp.zeros_like(q)
        return jnp.concatenate([jnp.where(lane < ATTN_DH, q, zero), jnp.where(lane >= ATTN_DH, q, zero)], axis=0)

    def lane_fold(v, op):
        out = v[:, :LANES]
        for c in range(1, v.shape[1] // LANES):
            out = op(out, v[:, c * LANES:(c + 1) * LANES])
        return out

    def scores(slot, qq, j, mx):
        k_ref_j, _, off = blocks[j]
        s = lax.dot_general(qq, k_ref_j[off:off + kb, :], _NT, preferred_element_type=F32)
        s_scr[slot, :, j * kb:(j + 1) * kb] = s
        part = lane_fold(s, jnp.maximum)
        return part if mx is None else jnp.maximum(mx, part)

    def expo(slot, m, j, sm):
        e = jnp.exp2(s_scr[slot, :, j * kb:(j + 1) * kb] - m)
        e_scr[slot, :, j * kb:(j + 1) * kb] = e
        part = lane_fold(e, jnp.add)
        return part if sm is None else sm + part

    def pv(slot, a0, a1, j, acc):
        _, v_ref_j, off = blocks[j]
        e = e_scr[slot, :, j * kb:(j + 1) * kb]
        w = (e[:th] * a0 - e[th:] * a1).astype(BF16)
        o = jnp.dot(w, v_ref_j[off:off + kb, :], preferred_element_type=F32)
        return o if acc is None else acc + o

    st = [dict() for _ in range(nsub)]
    for phase in range(nsub + 2):
        ts, te, tp = phase, phase - 1, phase - 2
        if 0 <= ts < nsub:
            st[ts].update(qq=stacked_q(ts), mx=None)
        if 0 <= te < nsub:
            st[te].update(m=jnp.max(st[te]["mx"], axis=-1, keepdims=True), sm=None)
        if 0 <= tp < nsub:
            inv = 1.0 / jnp.sum(st[tp]["sm"], axis=-1, keepdims=True)
            st[tp].update(a0=inv[:th], a1=inv[th:] * lam, acc=None)
        for j in range(len(blocks)):
            if 0 <= ts < nsub:
                st[ts]["mx"] = scores(ts % 2, st[ts]["qq"], j, st[ts]["mx"])
            if 0 <= tp < nsub:
                st[tp]["acc"] = pv(tp % 2, st[tp]["a0"], st[tp]["a1"], j, st[tp]["acc"])
            if 0 <= te < nsub:
                st[te]["sm"] = expo(te % 2, st[te]["m"], j, st[te]["sm"])
        if 0 <= tp < nsub:
            o = _rms(st[tp]["acc"], g_ref[...]) * (1.0 - LAM_INIT)
            o_ref[tp * th:(tp + 1) * th, :] = o.astype(o_ref.dtype)


def _diff_attention(lam, qkv, kv_ctx, g_subln, nb, ns):
    tq = ATTN_TQ
    nq = ns // tq
    nl = kv_ctx.shape[0] // nb
    stage_buf = pltpu.VMEM((2, 2 * ATTN_TH, nl + ns), F32)
    return pl.pallas_call(
        _attn_kernel,
        out_shape=jax.ShapeDtypeStruct((nb * ns, N_HEADS * ATTN_DV), BF16),
        grid=(nb, N_HEADS, nq),
        in_specs=[pl.BlockSpec(memory_space=pltpu.SMEM),
                  pl.BlockSpec((tq, LANES), lambda b, h, i: (b * nq + i, h)),
                  pl.BlockSpec((nl, LANES), lambda b, h, i: (b, h)),
                  pl.BlockSpec((ns, LANES), lambda b, h, i: (b, N_HEADS + h)),
                  pl.BlockSpec((nl, LANES), lambda b, h, i: (b, N_HEADS + h)),
                  pl.BlockSpec((ns, LANES), lambda b, h, i: (b, 2 * N_HEADS + h)),
                  pl.BlockSpec((1, LANES), lambda b, h, i: (0, 0))],
        out_specs=pl.BlockSpec((tq, LANES), lambda b, h, i: (b * nq + i, h)),
        scratch_shapes=[stage_buf, stage_buf],
        compiler_params=_cparams("parallel", "parallel", "arbitrary"),
    )(lam, qkv, kv_ctx, qkv, kv_ctx, qkv, g_subln)


def _ssd_conv(xin, prev_row, next_row, cw_ref, cb_ref):
    q = xin.shape[0]
    rid = lax.broadcasted_iota(jnp.int32, xin.shape, 0)
    up = jnp.where(rid == 0, prev_row, pltpu.roll(xin, 1, 0))
    dn = jnp.where(rid == q - 1, next_row, pltpu.roll(xin, q - 1, 0))
    return _silu(cb_ref[...] + up * cw_ref[0:1, :] + xin * cw_ref[1:2, :] + dn * cw_ref[2:3, :])


def _split_dot(v, e_bf16):
    hi, lo = _split_bf16(v)
    return (jnp.dot(hi, e_bf16, preferred_element_type=F32)
            + jnp.dot(lo, e_bf16, preferred_element_type=F32))


def _ssd_chunk(u, dt_blk, dtT_blk, bias_row, bias_col, alog_row, alog_col, is_f, states, want_y):
    q = u.shape[0]
    nh = SSD_HEADS
    xs = u[:, :D_SSD]
    bm = u[:, D_SSD:D_SSD + SSD_GROUPS * SSD_STATE]
    cm = u[:, D_SSD + SSD_GROUPS * SSD_STATE:]

    dt_all = _softplus(dt_blk + bias_row)
    dtT_all = _softplus(dtT_blk + bias_col)
    dt = jnp.where(is_f, dt_all[:, 0:nh], dt_all[:, nh:2 * nh])
    dtT = jnp.where(is_f, dtT_all[0:nh, :], dtT_all[nh:2 * nh, :])
    a_r = -jnp.exp(jnp.where(is_f, alog_row[0:1, :], alog_row[1:2, :]))
    a_c = -jnp.exp(jnp.where(is_f, alog_col[:, 0:1], alog_col[:, 1:2]))

    ri = lax.broadcasted_iota(jnp.int32, (q, q), 0)
    ci = lax.broadcasted_iota(jnp.int32, (q, q), 1)
    lower = (ri >= ci).astype(F32)
    upper = (ri <= ci).astype(F32)
    tmat = jnp.where(is_f, lower, upper)
    tmat_t = jnp.where(is_f, upper, lower)
    causal = tmat > 0.5
    acum = jnp.dot(tmat, dt * a_r, preferred_element_type=F32, precision=HIGHEST)
    acum_t = jnp.dot(dtT * a_c, tmat_t, preferred_element_type=F32, precision=HIGHEST)
    tot = jnp.where(is_f, acum[q - 1:q, :], acum[0:1, :])
    wgt = jnp.exp(tot - acum) * dt

    hid = lax.broadcasted_iota(jnp.int32, (nh, D_SSD), 0)
    cid = lax.broadcasted_iota(jnp.int32, (nh, D_SSD), 1)
    expand = (cid // SSD_HEADDIM == hid).astype(BF16)
    pieces = [wgt, jnp.broadcast_to(jnp.exp(tot), (SUBLANES, nh))]
    if want_y:
        pieces += [dt, jnp.exp(acum)]
    wide = _split_dot(jnp.concatenate(pieces, axis=0), expand)
    wgt_x = wide[0:q]
    dec_x = wide[q:q + 1]
    if want_y:
        dt_x = wide[q + SUBLANES:2 * q + SUBLANES]
        ea_x = wide[2 * q + SUBLANES:3 * q + SUBLANES]

    ys, new_states = [], []
    for g in range(SSD_GROUPS):
        gsl = slice(g * GROUP_W, (g + 1) * GROUP_W)
        nsl = slice(g * SSD_STATE, (g + 1) * SSD_STATE)
        xg = xs[:, gsl]
        bm_g = bm[:, nsl].astype(BF16)
        upd = lax.dot_general(bm_g, (xg * wgt_x[:, gsl]).astype(BF16), _TN, preferred_element_type=F32)
        new_states.append(upd if states is None else states[g] * dec_x[:, gsl] + upd)
        if not want_y:
            continue
        cm_g = cm[:, nsl].astype(BF16)
        cb = lax.dot_general(cm_g, bm_g, _NT, preferred_element_type=F32)
        xdt = (xg * dt_x[:, gsl]).astype(BF16)
        y_heads = []
        for r in range(SSD_HPG):
            h = g * SSD_HPG + r
            seg = acum[:, h:h + 1] - acum_t[h:h + 1, :]
            m_h = (cb * jnp.exp(jnp.where(causal, seg, NEG_BIG))).astype(BF16)
            y_heads.append(jnp.dot(m_h, xdt[:, r * SSD_HEADDIM:(r + 1) * SSD_HEADDIM], preferred_element_type=F32))
        y_g = jnp.concatenate(y_heads, axis=1)
        if states is not None:
            y_g = y_g + jnp.dot(cm_g, states[g].astype(BF16), preferred_element_type=F32) * ea_x[:, gsl]
        ys.append(y_g)
    return (jnp.concatenate(ys, axis=1) if want_y else None), new_states


def _ssd_ctx_kernel(x_ref, dt_ref, dtT_ref, cw_ref, cb_ref, brow_ref, bcol_ref, arow_ref, acol_ref, st_ref):
    is_f = pl.program_id(1) == 0
    xin = x_ref[...].astype(F32)
    zero_row = jnp.zeros((1, xin.shape[1]), F32)
    u = _ssd_conv(xin, zero_row, zero_row, cw_ref, cb_ref)
    _, new_states = _ssd_chunk(u, dt_ref[...], dtT_ref[...], brow_ref[...], bcol_ref[...], arow_ref[...],
                               acol_ref[...], is_f, None, False)
    for g in range(SSD_GROUPS):
        st_ref[0, 0, g] = new_states[g]


def _ssd_lat_kernel(x_ref, prev_ref, next_ref, dt_ref, dtT_ref, h0_ref, cw_ref, cb_ref, brow_ref, bcol_ref,
                    arow_ref, acol_ref, dskip_ref, o_ref, st_scr, *, n_chunks):
    d = pl.program_id(1)
    s = pl.program_id(2)
    is_f = d == 0
    chunk = jnp.where(is_f, s, n_chunks - 1 - s)
    q = x_ref.shape[0]

    @pl.when(s == 0)
    def _():
        st_scr[...] = h0_ref[0, 0]

    xin = x_ref[...].astype(F32)
    halo = prev_ref.shape[0]
    prev_row = jnp.where(chunk == 0, 0.0, prev_ref[halo - 1:halo, :].astype(F32))
    next_row = jnp.where(chunk == n_chunks - 1, 0.0, next_ref[0:1, :].astype(F32))
    u = _ssd_conv(xin, prev_row, next_row, cw_ref, cb_ref)
    states = [st_scr[g] for g in range(SSD_GROUPS)]
    y, new_states = _ssd_chunk(u, dt_ref[...], dtT_ref[...], brow_ref[...], bcol_ref[...], arow_ref[...],
                               acol_ref[...], is_f, states, True)
    for g in range(SSD_GROUPS):
        st_scr[g] = new_states[g]
    row0 = pl.multiple_of(chunk * q, q)

    @pl.when(is_f)
    def _():
        o_ref[pl.ds(row0, q), :] = y + dskip_ref[...] * u[:, :D_SSD]

    @pl.when(jnp.logical_not(is_f))
    def _():
        o_ref[pl.ds(row0, q), :] += y


def _ssd(xbc_lat_src, lat_col_block, dt_lat, dtT_lat, xbc_ctx, dt_ctx, dtT_ctx, conv_w, conv_b, dt_bias, a_log,
         d_skip, nb, ns):
    q = SSD_Q
    nl = xbc_ctx.shape[0] // nb
    nh = SSD_HEADS
    brow = jnp.zeros((1, LANES), F32).at[0, :2 * nh].set(dt_bias.reshape(-1))
    bcol = brow.reshape(LANES, 1)
    arow = a_log.reshape(2, nh)
    acol = arow.T
    cb2 = conv_b.reshape(1, XBC_DIM)
    const2 = lambda *_: (0, 0)
    param_specs = [pl.BlockSpec((3, XBC_DIM), const2), pl.BlockSpec((1, XBC_DIM), const2),
                   pl.BlockSpec((1, LANES), const2), pl.BlockSpec((LANES, 1), const2),
                   pl.BlockSpec((2, nh), const2), pl.BlockSpec((nh, 2), const2)]
    params = [conv_w, cb2, brow, bcol, arow, acol]
    state_shape = (nb, 2, SSD_GROUPS, SSD_STATE, GROUP_W)

    h0 = pl.pallas_call(
        _ssd_ctx_kernel,
        out_shape=jax.ShapeDtypeStruct(state_shape, F32),
        grid=(nb, 2),
        in_specs=[pl.BlockSpec((nl, XBC_DIM), lambda b, d: (b, 0)),
                  pl.BlockSpec((nl, LANES), lambda b, d: (b, 0)),
                  pl.BlockSpec((LANES, nl), lambda b, d: (0, b))] + param_specs,
        out_specs=pl.BlockSpec((1, 1, SSD_GROUPS, SSD_STATE, GROUP_W), lambda b, d: (b, d, 0, 0, 0)),
        compiler_params=_cparams("parallel", "arbitrary"),
    )(xbc_ctx, dt_ctx, dtT_ctx, *params)

    nc = ns // q
    halo = BF16_SUBLANES
    halo_per_chunk = q // halo
    n_halo = nb * ns // halo

    def chunk_of(d, s):
        return jnp.where(d == 0, s, nc - 1 - s)

    def prev_map(b, d, s):
        return (jnp.maximum((b * nc + chunk_of(d, s)) * halo_per_chunk - 1, 0), lat_col_block)

    def next_map(b, d, s):
        return (jnp.minimum((b * nc + chunk_of(d, s) + 1) * halo_per_chunk, n_halo - 1), lat_col_block)

    return pl.pallas_call(
        functools.partial(_ssd_lat_kernel, n_chunks=nc),
        out_shape=jax.ShapeDtypeStruct((nb * ns, D_SSD), F32),
        grid=(nb, 2, nc),
        in_specs=[pl.BlockSpec((q, XBC_DIM), lambda b, d, s: (b * nc + chunk_of(d, s), lat_col_block)),
                  pl.BlockSpec((halo, XBC_DIM), prev_map),
                  pl.BlockSpec((halo, XBC_DIM), next_map),
                  pl.BlockSpec((q, LANES), lambda b, d, s: (b * nc + chunk_of(d, s), 0)),
                  pl.BlockSpec((LANES, q), lambda b, d, s: (0, b * nc + chunk_of(d, s))),
                  pl.BlockSpec((1, 1, SSD_GROUPS, SSD_STATE, GROUP_W), lambda b, d, s: (b, d, 0, 0, 0))]
        + param_specs + [pl.BlockSpec((1, D_SSD), const2)],
        out_specs=pl.BlockSpec((ns, D_SSD), lambda b, d, s: (b, 0)),
        scratch_shapes=[pltpu.VMEM((SSD_GROUPS, SSD_STATE, GROUP_W), F32)],
        compiler_params=_cparams("parallel", "arbitrary", "arbitrary"),
    )(xbc_lat_src, xbc_lat_src, xbc_lat_src, dt_lat, dtT_lat, h0, *params,
      jnp.repeat(d_skip, SSD_HEADDIM).reshape(1, D_SSD))


def _merge_kernel(y_ref, z_ref, ga_ref, gs_ref, oa_ref, x_ref, gt1_ref, sh2_ref, sc2_ref, gssd_ref, gpost_ref,
                  gpre_ref, wba_ref, wbs_ref, wout_ref, wrhi_ref, wrlo_ref, x1_ref, h2_ref, lg_ref):
    y = y_ref[...] * _silu(z_ref[...].astype(F32))
    gw = D_SSD // SSD_GROUPS
    y = jnp.concatenate(
        [y[:, g * gw:(g + 1) * gw]
         * lax.rsqrt(jnp.mean(y[:, g * gw:(g + 1) * gw] * y[:, g * gw:(g + 1) * gw], axis=-1, keepdims=True) + EPS)
         for g in range(SSD_GROUPS)], axis=1)
    y = (y * gssd_ref[...]).astype(BF16)
    ys = jnp.dot(y, wbs_ref[...], preferred_element_type=F32)
    ya = jnp.dot(oa_ref[...], wba_ref[...], preferred_element_type=F32)
    mixed = (_sigmoid(ga_ref[...].astype(F32)) * ya + _sigmoid(gs_ref[...].astype(F32)) * ys).astype(BF16)
    mix = jnp.dot(mixed, wout_ref[...], preferred_element_type=F32)
    x1 = x_ref[...] + gt1_ref[0] * _rms(mix, gpost_ref[...])
    x1_ref[...] = x1
    h2 = _rms(x1, gpre_ref[...]) * (1.0 + sc2_ref[0]) + sh2_ref[0]
    h2_hi, h2_lo = _split_bf16(h2)
    h2_ref[...] = h2_hi
    lg_ref[...] = (jnp.dot(h2_hi, wrhi_ref[...], preferred_element_type=F32)
                   + jnp.dot(h2_lo, wrhi_ref[...], preferred_element_type=F32)
                   + jnp.dot(h2_hi, wrlo_ref[...], preferred_element_type=F32))


def _merge(y_ssd, rest, oattn, x2d, gt1, sh2, sc2, g_ssd_norm, g_post_mix, g_pre_ffn, wba, wbs, wout, w_router,
           rows_per_batch):
    m, d = x2d.shape
    tm = MERGE_TM
    tpb = rows_per_batch // tm
    row = lambda c: pl.BlockSpec((tm, d), lambda i: (i, c))
    per_batch = pl.BlockSpec((1, 1, d), lambda i: (i // tpb, 0, 0))
    vec = pl.BlockSpec((1, d), lambda i: (0, 0))
    mat = pl.BlockSpec((d, d), lambda i: (0, 0))
    wr = jnp.zeros((d, LANES), F32).at[:, :N_EXPERTS].set(w_router)
    wr_hi, wr_lo = _split_bf16(wr)
    wr_spec = pl.BlockSpec((d, LANES), lambda i: (0, 0))
    return pl.pallas_call(
        _merge_kernel,
        out_shape=[jax.ShapeDtypeStruct((m, d), F32), jax.ShapeDtypeStruct((m, d), BF16),
                   jax.ShapeDtypeStruct((m, LANES), F32)],
        grid=(m // tm,),
        in_specs=[row(0), row(2), row(3), row(4), row(0), row(0), per_batch, per_batch, per_batch, vec, vec, vec,
                  mat, mat, mat, wr_spec, wr_spec],
        out_specs=[row(0), row(0), pl.BlockSpec((tm, LANES), lambda i: (i, 0))],
        compiler_params=_cparams("parallel"),
    )(y_ssd, rest, rest, rest, oattn, x2d, gt1, sh2, sc2, g_ssd_norm, g_post_mix, g_pre_ffn, wba, wbs, wout,
      wr_hi, wr_lo)


def _first_index(hit, iota, limit):
    return jnp.min(jnp.where(hit, iota, limit), axis=0, keepdims=True)


def _route_kernel(lg_ref, bias_ref, eidx_ref, wts_ref, rank_ref, cnt_ref, carry):
    i = pl.program_id(0)
    tl = lg_ref.shape[0]

    @pl.when(i == 0)
    def _():
        carry[...] = jnp.zeros_like(carry)

    scores = _sigmoid(lg_ref[...].T[:N_EXPERTS])
    sel = scores + bias_ref[...]
    epg = EXPERTS_PER_GROUP
    io8 = lax.broadcasted_iota(jnp.int32, (epg, tl), 0)
    neg_inf = -jnp.inf

    grp = []
    for g in range(N_EXPERT_GROUPS):
        blk = sel[g * epg:(g + 1) * epg]
        m1 = jnp.max(blk, axis=0, keepdims=True)
        rest = jnp.where(io8 == _first_index(blk == m1, io8, epg), neg_inf, blk)
        grp.append(m1 + jnp.max(rest, axis=0, keepdims=True))
    gsc = jnp.concatenate(grp, axis=0)
    gkeep = jnp.zeros_like(gsc)
    for _ in range(TOPK_GROUPS):
        pick = io8 == _first_index(gsc == jnp.max(gsc, axis=0, keepdims=True), io8, N_EXPERT_GROUPS)
        gkeep = jnp.where(pick, 1.0, gkeep)
        gsc = jnp.where(pick, neg_inf, gsc)
    cand = jnp.concatenate(
        [jnp.where(gkeep[g:g + 1] > 0.5, sel[g * epg:(g + 1) * epg], neg_inf) for g in range(N_EXPERT_GROUPS)], axis=0)

    ioe = lax.broadcasted_iota(jnp.int32, (N_EXPERTS, tl), 0)
    picks, pscore = [], []
    chosen = jnp.zeros_like(cand)
    for _ in range(TOP_K):
        e_k = _first_index(cand == jnp.max(cand, axis=0, keepdims=True), ioe, N_EXPERTS)
        hit = ioe == e_k
        picks.append(e_k)
        pscore.append(jnp.sum(jnp.where(hit, scores, 0.0), axis=0, keepdims=True))
        chosen = jnp.where(hit, 1.0, chosen)
        cand = jnp.where(hit, neg_inf, cand)
    wsum = pscore[0]
    for k in range(1, TOP_K):
        wsum = wsum + pscore[k]

    ti = lax.broadcasted_iota(jnp.int32, (tl, tl), 0)
    tj = lax.broadcasted_iota(jnp.int32, (tl, tl), 1)
    before = (ti < tj).astype(BF16)
    rank = jnp.dot(chosen.astype(BF16), before, preferred_element_type=F32) + carry[...]
    total = carry[...] + jnp.sum(chosen, axis=1, keepdims=True)
    carry[...] = total
    cnt_ref[...] = jnp.broadcast_to(total, cnt_ref.shape)

    eidx_ref[...] = jnp.concatenate(picks, axis=0)
    wts_ref[...] = jnp.concatenate([p / wsum * ROUTED_SCALE for p in pscore], axis=0)
    rank_ref[...] = jnp.concatenate(
        [jnp.sum(jnp.where(ioe == e_k, rank, 0.0), axis=0, keepdims=True) for e_k in picks], axis=0).astype(jnp.int32)


def _route(logits, router_bias):
    m = logits.shape[0]
    ne = N_EXPERTS
    tl = ROUTE_TL
    tok = lambda dt: jax.ShapeDtypeStruct((TOP_K, m), dt)
    tok_spec = pl.BlockSpec((TOP_K, tl), lambda i: (0, i))
    return pl.pallas_call(
        _route_kernel,
        out_shape=[tok(jnp.int32), tok(F32), tok(jnp.int32), jax.ShapeDtypeStruct((ne, LANES), F32)],
        grid=(m // tl,),
        in_specs=[pl.BlockSpec((tl, LANES), lambda i: (i, 0)), pl.BlockSpec((ne, 1), lambda i: (0, 0))],
        out_specs=[tok_spec, tok_spec, tok_spec, pl.BlockSpec((ne, LANES), lambda i: (0, 0))],
        scratch_shapes=[pltpu.VMEM((ne, 1), F32)],
        compiler_params=_cparams("arbitrary"),
    )(logits, router_bias.reshape(ne, 1))


def _expert_kernel(tile_ref, e_ref, lo_ref, hi_ref, x_ref, wg_ref, wu_ref, wd_ref, o_ref, wg_s, wu_s, wd_s):
    w = pl.program_id(0)
    lo, hi = lo_ref[w], hi_ref[w]
    tm = x_ref.shape[0]
    base = tile_ref[w] * tm

    @pl.when((w == 0) | (e_ref[w] != e_ref[jnp.maximum(w - 1, 0)]))
    def _():
        wg_s[...] = wg_ref[0].astype(BF16)
        wu_s[...] = wu_ref[0].astype(BF16)
        wd_s[...] = wd_ref[0].astype(BF16)

    @pl.when(hi > lo)
    def _():
        x = x_ref[...]
        gate = jnp.dot(x, wg_s[...], preferred_element_type=F32)
        up = jnp.dot(x, wu_s[...], preferred_element_type=F32)
        y = jnp.dot((_silu(gate) * up).astype(BF16), wd_s[...], preferred_element_type=F32).astype(o_ref.dtype)
        rid = base + lax.broadcasted_iota(jnp.int32, (tm, 1), 0)

        @pl.when(lo == base)
        def _():
            o_ref[...] = jnp.where(rid < hi, y, jnp.zeros_like(y))

        @pl.when(lo != base)
        def _():
            o_ref[...] = jnp.where((rid >= lo) & (rid < hi), y, o_ref[...])


def _experts(item_tile, item_e, item_lo, item_hi, xs, w_gate, w_up, w_down):
    n_rows, d = xs.shape
    tm = EXPERT_TM
    ff = w_gate.shape[-1]
    row_map = lambda w, t, e, lo, hi: (t[w], 0)
    return pl.pallas_call(
        _expert_kernel,
        out_shape=jax.ShapeDtypeStruct((n_rows, d), BF16),
        grid_spec=pltpu.PrefetchScalarGridSpec(
            num_scalar_prefetch=4,
            grid=(item_tile.shape[0],),
            in_specs=[pl.BlockSpec((tm, d), row_map),
                      pl.BlockSpec((1, d, ff), lambda w, t, e, lo, hi: (e[w], 0, 0)),
                      pl.BlockSpec((1, d, ff), lambda w, t, e, lo, hi: (e[w], 0, 0)),
                      pl.BlockSpec((1, ff, d), lambda w, t, e, lo, hi: (e[w], 0, 0))],
            out_specs=pl.BlockSpec((tm, d), row_map),
            scratch_shapes=[pltpu.VMEM((d, ff), BF16), pltpu.VMEM((d, ff), BF16), pltpu.VMEM((ff, d), BF16)]),
        compiler_params=_cparams("arbitrary"),
    )(item_tile, item_e, item_lo, item_hi, xs, w_gate, w_up, w_down)


def _final_kernel(x1_ref, h2_ref, yk_ref, wt_ref, gt2_ref, gpost_ref, wg_ref, wu_ref, wd_ref, o_ref):
    h2 = h2_ref[...]
    gate = jnp.dot(h2, wg_ref[...], preferred_element_type=F32)
    up = jnp.dot(h2, wu_ref[...], preferred_element_type=F32)
    f = jnp.dot((_silu(gate) * up).astype(BF16), wd_ref[...], preferred_element_type=F32)
    wt = wt_ref[...]
    for k in range(TOP_K):
        f = f + yk_ref[k].astype(F32) * wt[:, k:k + 1]
    o_ref[...] = x1_ref[...] + gt2_ref[0] * _rms(f, gpost_ref[...])


def _final(x1, h2, y_tok, wts_t, gt2, g_post_ffn, w_sh_gate, w_sh_up, w_sh_down, rows_per_batch):
    m, d = x1.shape
    tm = FINAL_TM
    tpb = rows_per_batch // tm
    row = pl.BlockSpec((tm, d), lambda i: (i, 0))
    ff = w_sh_gate.shape[1]
    return pl.pallas_call(
        _final_kernel,
        out_shape=jax.ShapeDtypeStruct((m, d), F32),
        grid=(m // tm,),
        in_specs=[row, row, pl.BlockSpec((TOP_K, tm, d), lambda i: (0, i, 0)),
                  pl.BlockSpec((tm, TOP_K), lambda i: (i, 0)),
                  pl.BlockSpec((1, 1, d), lambda i: (i // tpb, 0, 0)),
                  pl.BlockSpec((1, d), lambda i: (0, 0)),
                  pl.BlockSpec((d, ff), lambda i: (0, 0)), pl.BlockSpec((d, ff), lambda i: (0, 0)),
                  pl.BlockSpec((ff, d), lambda i: (0, 0))],
        out_specs=row,
        compiler_params=_cparams("parallel"),
    )(x1, h2, y_tok, wts_t, gt2, g_post_ffn, w_sh_gate, w_sh_up, w_sh_down)


def kernel(x, c, ctx, c_ctx, w_ada, b_ada, g_pre_mix, g_post_mix, g_pre_ffn, g_post_ffn, w_in, lam_q1, lam_k1, lam_q2, lam_k2, g_attn_subln, conv_w, conv_b, dt_bias, a_log, d_skip, g_ssd_norm, w_branch_attn, w_branch_ssd, w_out, w_router, router_bias, w_e_gate, w_e_up, w_e_down, w_sh_gate, w_sh_up, w_sh_down):
    nb, ns, d = x.shape
    nl = ctx.shape[1]
    m = nb * ns
    li = 0

    c_all = jnp.zeros((3 * SUBLANES, d), F32).at[:nb].set(c).at[nb].set(c_ctx)
    mod = _modulation(c_all, w_ada[li], b_ada[li])
    sh1, sc1, gt1, sh2, sc2, gt2 = (mod[:nb, k * d:(k + 1) * d].reshape(nb, 1, d) for k in range(6))
    sh1c, sc1c = (mod[nb:nb + 1, k * d:(k + 1) * d].reshape(1, 1, d) for k in range(2))

    qk_dim = N_HEADS * 2 * ATTN_DH
    v_dim = N_HEADS * ATTN_DV
    o_q, o_k, o_v, o_z = 0, qk_dim, 2 * qk_dim, 2 * qk_dim + v_dim
    o_xbc = o_z + D_SSD
    o_dt = o_xbc + XBC_DIM
    o_g = o_dt + 2 * SSD_HEADS
    wi = w_in[li]
    w_qkv = wi[:, o_q:o_z].astype(BF16)
    w_qk_rot = _rotate_half_columns(wi[:, o_q:o_v]).astype(BF16)
    w_kv = wi[:, o_k:o_z].astype(BF16)
    w_xbc = wi[:, o_xbc:o_dt].astype(BF16)
    w_rest = jnp.concatenate([wi[:, o_xbc:o_dt], wi[:, o_z:o_xbc], wi[:, o_g:]], axis=1).astype(BF16)
    w_dt = jnp.zeros((d, LANES), BF16).at[:, :2 * SSD_HEADS].set(wi[:, o_dt:o_g].astype(BF16))

    x2d = x.reshape(m, d)
    c2d = ctx.reshape(nb * nl, d)
    g1 = g_pre_mix[li].reshape(1, d)
    cos, sin = _rope_tables(ns)
    qkv = _project(x2d, sh1, sc1, g1, w_qkv, rows_per_mod=ns, rope=(w_qk_rot, cos, sin, qk_dim))
    rest, dt_lat, dtT_lat = _project(x2d, sh1, sc1, g1, w_rest, rows_per_mod=ns, w_dt=w_dt)
    kv_ctx = _project(c2d, sh1c, sc1c, g1, w_kv, rows_per_mod=nb * nl)
    xbc_ctx, dt_ctx, dtT_ctx = _project(c2d, sh1c, sc1c, g1, w_xbc, rows_per_mod=nb * nl, w_dt=w_dt)

    lam = (jnp.exp(jnp.sum(lam_q1[li] * lam_k1[li])) - jnp.exp(jnp.sum(lam_q2[li] * lam_k2[li])) + LAM_INIT)
    oattn = _diff_attention(lam.reshape(1).astype(F32), qkv, kv_ctx, g_attn_subln[li].reshape(1, ATTN_DV), nb, ns)

    y_ssd = _ssd(rest, 0, dt_lat, dtT_lat, xbc_ctx, dt_ctx, dtT_ctx, conv_w[li], conv_b[li], dt_bias[li], a_log[li],
                 d_skip[li], nb, ns)

    x1, h2, logits = _merge(
        y_ssd, rest, oattn, x2d, gt1, sh2, sc2, g_ssd_norm[li].reshape(1, d), g_post_mix[li].reshape(1, d),
        g_pre_ffn[li].reshape(1, d), w_branch_attn[li].astype(BF16), w_branch_ssd[li].astype(BF16),
        w_out[li].astype(BF16), w_router[li], ns)

    eidx, wts, rank, counts = _route(logits, router_bias[li])

    tm = EXPERT_TM
    n_rows = m * TOP_K
    n_tiles = n_rows // tm
    counts = counts[:, 0].astype(jnp.int32)
    ends = jnp.cumsum(counts)
    start = ends - counts
    eid = jnp.arange(N_EXPERTS, dtype=jnp.int32)
    dest = jnp.sum(jnp.where(eidx[None] == eid[:, None, None], start[:, None, None], 0), axis=0) + rank
    tok = jnp.arange(m, dtype=jnp.int32)
    slot_tok = jnp.sort((eidx * m + tok[None, :]).reshape(-1)) % m
    xs = h2.at[slot_tok].get(mode="promise_in_bounds")

    cuts = jnp.sort(jnp.concatenate([jnp.arange(n_tiles, dtype=jnp.int32) * tm, start[1:]]))
    item_hi = jnp.concatenate([cuts[1:], jnp.full((1,), n_rows, jnp.int32)])
    item_tile = jnp.minimum(cuts // tm, n_tiles - 1)
    item_e = jnp.minimum(jnp.sum((ends[None, :] <= cuts[:, None]).astype(jnp.int32), axis=1), N_EXPERTS - 1)
    y_sorted = _experts(item_tile, item_e, cuts, item_hi, xs, w_e_gate[li], w_e_up[li], w_e_down[li])

    y_tok = y_sorted.at[dest.reshape(-1)].get(mode="promise_in_bounds").reshape(TOP_K, m, d)
    out = _final(x1, h2, y_tok, wts.T, gt2, g_post_ffn[li].reshape(1, d), w_sh_gate[li].astype(BF16),
                 w_sh_up[li].astype(BF16), w_sh_down[li].astype(BF16), ns)
    return out.reshape(nb, ns, d)
```

```python
import functools
import math

import jax
import jax.numpy as jnp
from jax import lax
from jax.experimental import pallas as pl
from jax.experimental.pallas import tpu as pltpu

F32 = jnp.float32
BF16 = jnp.bfloat16
HIGHEST = lax.Precision.HIGHEST

D_MODEL = 1024
CTX_LEN = 256
GRID_W = 64
EPS = 1e-6

N_HEADS = 8
ATTN_DH = 64
ATTN_DV = 128
ATTN_SCALE = ATTN_DH ** -0.5
LOG2E = math.log2(math.e)
ROPE_THETA = 10000.0
ROPE_AXIS_DIM = ATTN_DH // 2
LAM_INIT = 0.8 - 0.6 * math.exp(-0.3 * 0)

D_SSD = 1024
SSD_HEADDIM = 64
SSD_HEADS = 16
SSD_GROUPS = 4
SSD_HPG = 4
SSD_STATE = 128
XBC_DIM = D_SSD + 2 * SSD_GROUPS * SSD_STATE
GROUP_W = SSD_HPG * SSD_HEADDIM

N_EXPERTS = 64
EXPERT_FF = 256
TOP_K = 8
N_EXPERT_GROUPS = 8
EXPERTS_PER_GROUP = N_EXPERTS // N_EXPERT_GROUPS
TOPK_GROUPS = 4
ROUTED_SCALE = 2.5

LANES = 128
SUBLANES = 8
BF16_SUBLANES = 16
VMEM_LIMIT = 52 * 1024 * 1024

PROJ_TM = 512
PROJ_TN = 512
ATTN_TQ = 1024
ATTN_TH = 256
ATTN_KB = 256
SSD_Q = 256
MERGE_TM = 512
ROUTE_TL = 512
EXPERT_TM = 512
FINAL_TM = 512
MOE_CHUNKS = 4

NEG_BIG = -1e30

_NT = (((1,), (1,)), ((), ()))
_TN = (((0,), (0,)), ((), ()))


def _cparams(*sem):
    return pltpu.CompilerParams(dimension_semantics=sem, vmem_limit_bytes=VMEM_LIMIT)


def _sigmoid(v):
    return 0.5 * jnp.tanh(0.5 * v) + 0.5


def _silu(v):
    return v * _sigmoid(v)


def _softplus(v):
    return jnp.maximum(v, 0.0) + jnp.log1p(jnp.exp(-jnp.abs(v)))


def _rms(v, g):
    return v * lax.rsqrt(jnp.mean(v * v, axis=-1, keepdims=True) + EPS) * g


def _split_bf16(v):
    hi = v.astype(BF16)
    return hi, (v - hi.astype(F32)).astype(BF16)


def _mod_kernel(c_ref, w_ref, b_ref, o_ref):
    o_ref[...] = jnp.dot(_silu(c_ref[...]), w_ref[...], preferred_element_type=F32,
                         precision=HIGHEST) + b_ref[...]


def _modulation(c_all, w_ada, b_ada):
    rows, d = c_all.shape
    n = w_ada.shape[1]
    tn = 1024
    return pl.pallas_call(
        _mod_kernel,
        out_shape=jax.ShapeDtypeStruct((rows, n), F32),
        grid=(n // tn,),
        in_specs=[pl.BlockSpec((rows, d), lambda j: (0, 0)),
                  pl.BlockSpec((d, tn), lambda j: (0, j)),
                  pl.BlockSpec((1, tn), lambda j: (0, j))],
        out_specs=pl.BlockSpec((rows, tn), lambda j: (0, j)),
        compiler_params=_cparams("arbitrary"),
    )(c_all, w_ada, b_ada.reshape(1, n))


def _proj_kernel(*refs, n_rot, n_q, has_dt):
    it = iter(refs)
    x_ref, sh_ref, sc_ref, g_ref, w_ref = (next(it) for _ in range(5))
    if n_rot:
        wrot_ref, cos_ref, sin_ref = next(it), next(it), next(it)
    if has_dt:
        wdt_ref, wdtT_ref = next(it), next(it)
    o_ref = next(it)
    if has_dt:
        dt_ref, dtT_ref = next(it), next(it)

    h = (_rms(x_ref[...], g_ref[...]) * (1.0 + sc_ref[0]) + sh_ref[0]).astype(BF16)
    if has_dt:
        dt_ref[...] = jnp.dot(h, wdt_ref[...], preferred_element_type=F32)
        dtT_ref[...] = lax.dot_general(wdtT_ref[...], h, _NT, preferred_element_type=F32)
    tn = PROJ_TN
    for c in range(w_ref.shape[1] // tn):
        cs = slice(c * tn, (c + 1) * tn)
        u = jnp.dot(h, w_ref[:, cs], preferred_element_type=F32)
        if c * tn < n_rot:
            ur = jnp.dot(h, wrot_ref[:, cs], preferred_element_type=F32)
            scale = ATTN_SCALE * LOG2E if c * tn < n_q else 1.0
            cos = cos_ref[...] * scale
            sin = sin_ref[...] * scale
            for s in range(tn // LANES):
                sl = slice(s * LANES, (s + 1) * LANES)
                o_ref[:, c * tn + s * LANES:c * tn + (s + 1) * LANES] = (
                    u[:, sl] * cos + ur[:, sl] * sin).astype(o_ref.dtype)
        else:
            o_ref[:, cs] = u.astype(o_ref.dtype)


def _project(x2d, shift, scale, gain, w, *, rows_per_mod, rope=None, w_dt=None):
    m, d = x2d.shape
    n = w.shape[1]
    tm = min(PROJ_TM, rows_per_mod)
    tiles_per_mod = rows_per_mod // tm
    has_dt = w_dt is not None
    const = lambda i: (0, 0)
    in_specs = [pl.BlockSpec((tm, d), lambda i: (i, 0)),
                pl.BlockSpec((1, 1, d), lambda i: (i // tiles_per_mod, 0, 0)),
                pl.BlockSpec((1, 1, d), lambda i: (i // tiles_per_mod, 0, 0)),
                pl.BlockSpec((1, d), const),
                pl.BlockSpec((d, n), const)]
    args = [x2d, shift, scale, gain, w]
    out_shape = [jax.ShapeDtypeStruct((m, n), BF16)]
    out_specs = [pl.BlockSpec((tm, n), lambda i: (i, 0))]
    n_rot = n_q = 0
    if rope is not None:
        w_rot, cos, sin, n_q = rope
        n_rot = w_rot.shape[1]
        pos_tiles = cos.shape[0] // tm
        in_specs += [pl.BlockSpec((d, n_rot), const)] + [pl.BlockSpec((tm, LANES), lambda i: (i % pos_tiles, 0))] * 2
        args += [w_rot, cos, sin]
    if has_dt:
        in_specs += [pl.BlockSpec((d, LANES), const), pl.BlockSpec((LANES, d), const)]
        args += [w_dt, w_dt.T]
        out_shape += [jax.ShapeDtypeStruct((m, LANES), F32), jax.ShapeDtypeStruct((LANES, m), F32)]
        out_specs += [pl.BlockSpec((tm, LANES), lambda i: (i, 0)), pl.BlockSpec((LANES, tm), lambda i: (0, i))]
    res = pl.pallas_call(
        functools.partial(_proj_kernel, n_rot=n_rot, n_q=n_q, has_dt=has_dt),
        out_shape=out_shape,
        grid=(m // tm,),
        in_specs=in_specs,
        out_specs=out_specs,
        compiler_params=_cparams("parallel"),
    )(*args)
    return res if has_dt else res[0]


def _rope_tables(n_tok):
    rows = n_tok // GRID_W
    row = jnp.repeat(jnp.arange(rows, dtype=F32), GRID_W)
    col = jnp.broadcast_to(jnp.arange(GRID_W, dtype=F32)[None, :], (rows, GRID_W)).reshape(-1)
    inv_freq = ROPE_THETA ** (-jnp.arange(0, ROPE_AXIS_DIM, 2, dtype=F32) / ROPE_AXIS_DIM)
    ang = jnp.concatenate([row[:, None] * inv_freq, col[:, None] * inv_freq], axis=-1)
    ang = jnp.concatenate([ang, ang, ang, ang], axis=-1)
    return jnp.cos(ang), jnp.sin(ang)


def _rotate_half_columns(w):
    d, n = w.shape
    half = ATTN_DH // 2
    wc = w.reshape(d, n // ATTN_DH, 2, half)
    return jnp.stack([-wc[:, :, 1], wc[:, :, 0]], axis=2).reshape(d, n)


def _attn_kernel(lam_ref, q_ref, kc_ref, k_ref, vc_ref, v_ref, g_ref, o_ref, s_scr, e_scr):
    th, kb = ATTN_TH, ATTN_KB
    nsub = q_ref.shape[0] // th
    lam = lam_ref[0]
    blocks = ([(kc_ref, vc_ref, o) for o in range(0, kc_ref.shape[0], kb)]
              + [(k_ref, v_ref, o) for o in range(0, k_ref.shape[0], kb)])

    def stacked_q(t):
        q = q_ref[t * th:(t + 1) * th, :]
        lane = lax.broadcasted_iota(jnp.int32, q.shape, 1)
        zero = jnp.zeros_like(q)
        return jnp.concatenate([jnp.where(lane < ATTN_DH, q, zero), jnp.where(lane >= ATTN_DH, q, zero)], axis=0)

    def lane_fold(v, op):
        out = v[:, :LANES]
        for c in range(1, v.shape[1] // LANES):
            out = op(out, v[:, c * LANES:(c + 1) * LANES])
        return out

    def scores(slot, qq, j, mx):
        k_ref_j, _, off = blocks[j]
        s = lax.dot_general(qq, k_ref_j[off:off + kb, :], _NT, preferred_element_type=F32)
        s_scr[slot, :, j * kb:(j + 1) * kb] = s
        part = lane_fold(s, jnp.maximum)
        return part if mx is None else jnp.maximum(mx, part)

    def expo(slot, m, j, sm):
        e = jnp.exp2(s_scr[slot, :, j * kb:(j + 1) * kb] - m)
        e_scr[slot, :, j * kb:(j + 1) * kb] = e
        part = lane_fold(e, jnp.add)
        return part if sm is None else sm + part

    def pv(slot, a0, a1, j, acc):
        _, v_ref_j, off = blocks[j]
        e = e_scr[slot, :, j * kb:(j + 1) * kb]
        w = (e[:th] * a0 - e[th:] * a1).astype(BF16)
        o = jnp.dot(w, v_ref_j[off:off + kb, :], preferred_element_type=F32)
        return o if acc is None else acc + o

    st = [dict() for _ in range(nsub)]
    for phase in range(nsub + 2):
        ts, te, tp = phase, phase - 1, phase - 2
        if 0 <= ts < nsub:
            st[ts].update(qq=stacked_q(ts), mx=None)
        if 0 <= te < nsub:
            st[te].update(m=jnp.max(st[te]["mx"], axis=-1, keepdims=True), sm=None)
        if 0 <= tp < nsub:
            inv = 1.0 / jnp.sum(st[tp]["sm"], axis=-1, keepdims=True)
            st[tp].update(a0=inv[:th], a1=inv[th:] * lam, acc=None)
        for j in range(len(blocks)):
            if 0 <= ts < nsub:
                st[ts]["mx"] = scores(ts % 2, st[ts]["qq"], j, st[ts]["mx"])
            if 0 <= tp < nsub:
                st[tp]["acc"] = pv(tp % 2, st[tp]["a0"], st[tp]["a1"], j, st[tp]["acc"])
            if 0 <= te < nsub:
                st[te]["sm"] = expo(te % 2, st[te]["m"], j, st[te]["sm"])
        if 0 <= tp < nsub:
            o = _rms(st[tp]["acc"], g_ref[...]) * (1.0 - LAM_INIT)
            o_ref[tp * th:(tp + 1) * th, :] = o.astype(o_ref.dtype)


def _diff_attention(lam, qkv, kv_ctx, g_subln, nb, ns):
    tq = ATTN_TQ
    nq = ns // tq
    nl = kv_ctx.shape[0] // nb
    stage_buf = pltpu.VMEM((2, 2 * ATTN_TH, nl + ns), F32)
    return pl.pallas_call(
        _attn_kernel,
        out_shape=jax.ShapeDtypeStruct((nb * ns, N_HEADS * ATTN_DV), BF16),
        grid=(nb, N_HEADS, nq),
        in_specs=[pl.BlockSpec(memory_space=pltpu.SMEM),
                  pl.BlockSpec((tq, LANES), lambda b, h, i: (b * nq + i, h)),
                  pl.BlockSpec((nl, LANES), lambda b, h, i: (b, h)),
                  pl.BlockSpec((ns, LANES), lambda b, h, i: (b, N_HEADS + h)),
                  pl.BlockSpec((nl, LANES), lambda b, h, i: (b, N_HEADS + h)),
                  pl.BlockSpec((ns, LANES), lambda b, h, i: (b, 2 * N_HEADS + h)),
                  pl.BlockSpec((1, LANES), lambda b, h, i: (0, 0))],
        out_specs=pl.BlockSpec((tq, LANES), lambda b, h, i: (b * nq + i, h)),
        scratch_shapes=[stage_buf, stage_buf],
        compiler_params=_cparams("parallel", "parallel", "arbitrary"),
    )(lam, qkv, kv_ctx, qkv, kv_ctx, qkv, g_subln)


def _ssd_conv(xin, prev_row, next_row, cw_ref, cb_ref):
    q = xin.shape[0]
    rid = lax.broadcasted_iota(jnp.int32, xin.shape, 0)
    up = jnp.where(rid == 0, prev_row, pltpu.roll(xin, 1, 0))
    dn = jnp.where(rid == q - 1, next_row, pltpu.roll(xin, q - 1, 0))
    return _silu(cb_ref[...] + up * cw_ref[0:1, :] + xin * cw_ref[1:2, :] + dn * cw_ref[2:3, :])


def _split3(v):
    a = v.astype(BF16)
    r = v - a.astype(F32)
    b = r.astype(BF16)
    return a, b, (r - b.astype(F32)).astype(BF16)


def _split_dot(v, e_bf16):
    hi, lo = _split_bf16(v)
    return jnp.dot(jnp.concatenate([hi, lo], axis=1), jnp.concatenate([e_bf16, e_bf16], axis=0),
                   preferred_element_type=F32)


def _ssd_chunk(u, dt_blk, dtT_blk, bias_row, bias_col, alog_row, alog_col, is_f, states, want_y):
    q = u.shape[0]
    nh = SSD_HEADS
    xs = u[:, :D_SSD]
    bm = u[:, D_SSD:D_SSD + SSD_GROUPS * SSD_STATE]
    cm = u[:, D_SSD + SSD_GROUPS * SSD_STATE:]

    dt_all = _softplus(dt_blk + bias_row)
    dtT_all = _softplus(dtT_blk + bias_col)
    dt = jnp.where(is_f, dt_all[:, 0:nh], dt_all[:, nh:2 * nh])
    dtT = jnp.where(is_f, dtT_all[0:nh, :], dtT_all[nh:2 * nh, :])
    a_r = -jnp.exp(jnp.where(is_f, alog_row[0:1, :], alog_row[1:2, :]))
    a_c = -jnp.exp(jnp.where(is_f, alog_col[:, 0:1], alog_col[:, 1:2]))

    ri = lax.broadcasted_iota(jnp.int32, (q, q), 0)
    ci = lax.broadcasted_iota(jnp.int32, (q, q), 1)
    lower = (ri >= ci).astype(BF16)
    upper = (ri <= ci).astype(BF16)
    tmat = jnp.where(is_f, lower, upper)
    tmat_t = jnp.where(is_f, upper, lower)
    causal = tmat > 0.5
    acum = sum(jnp.dot(tmat, p, preferred_element_type=F32) for p in _split3(dt * a_r))
    acum_t = sum(jnp.dot(p, tmat_t, preferred_element_type=F32) for p in _split3(dtT * a_c))
    tot = jnp.where(is_f, acum[q - 1:q, :], acum[0:1, :])
    wgt = jnp.exp(tot - acum) * dt

    hid = lax.broadcasted_iota(jnp.int32, (nh, D_SSD), 0)
    cid = lax.broadcasted_iota(jnp.int32, (nh, D_SSD), 1)
    expand = (cid // SSD_HEADDIM == hid).astype(BF16)
    pieces = [wgt, jnp.broadcast_to(jnp.exp(tot), (SUBLANES, nh))]
    if want_y:
        pieces.append(jnp.exp(acum))
    wide = _split_dot(jnp.concatenate(pieces, axis=0), expand)
    wgt_x = wide[0:q]
    dec_x = wide[q:q + 1]
    if want_y:
        ea_x = wide[q + SUBLANES:2 * q + SUBLANES]
        src_t = acum_t - jnp.log(dtT)

    ys, new_states = [], []
    for g in range(SSD_GROUPS):
        gsl = slice(g * GROUP_W, (g + 1) * GROUP_W)
        nsl = slice(g * SSD_STATE, (g + 1) * SSD_STATE)
        xg = xs[:, gsl]
        bm_g = bm[:, nsl].astype(BF16)
        upd = lax.dot_general(bm_g, (xg * wgt_x[:, gsl]).astype(BF16), _TN, preferred_element_type=F32)
        new_states.append(upd if states is None else states[g] * dec_x[:, gsl] + upd)
        if not want_y:
            continue
        cm_g = cm[:, nsl].astype(BF16)
        cb = lax.dot_general(cm_g, bm_g, _NT, preferred_element_type=F32)
        xb = xg.astype(BF16)
        y_heads = []
        for r in range(SSD_HPG):
            h = g * SSD_HPG + r
            seg = acum[:, h:h + 1] - src_t[h:h + 1, :]
            m_h = (cb * jnp.exp(jnp.where(causal, seg, NEG_BIG))).astype(BF16)
            y_heads.append(jnp.dot(m_h, xb[:, r * SSD_HEADDIM:(r + 1) * SSD_HEADDIM], preferred_element_type=F32))
        y_g = jnp.concatenate(y_heads, axis=1)
        if states is not None:
            y_g = y_g + jnp.dot(cm_g, states[g].astype(BF16), preferred_element_type=F32) * ea_x[:, gsl]
        ys.append(y_g)
    return (jnp.concatenate(ys, axis=1) if want_y else None), new_states


def _ssd_ctx_kernel(x_ref, dt_ref, dtT_ref, cw_ref, cb_ref, brow_ref, bcol_ref, arow_ref, acol_ref, st_ref):
    is_f = pl.program_id(1) == 0
    xin = x_ref[...].astype(F32)
    zero_row = jnp.zeros((1, xin.shape[1]), F32)
    u = _ssd_conv(xin, zero_row, zero_row, cw_ref, cb_ref)
    _, new_states = _ssd_chunk(u, dt_ref[...], dtT_ref[...], brow_ref[...], bcol_ref[...], arow_ref[...],
                               acol_ref[...], is_f, None, False)
    for g in range(SSD_GROUPS):
        st_ref[0, 0, g] = new_states[g]


def _ssd_lat_kernel(x_ref, prev_ref, next_ref, dt_ref, dtT_ref, h0_ref, cw_ref, cb_ref, brow_ref, bcol_ref,
                    arow_ref, acol_ref, dskip_ref, o_ref, st_scr, *, n_chunks):
    d = pl.program_id(1)
    s = pl.program_id(2)
    is_f = d == 0
    chunk = jnp.where(is_f, s, n_chunks - 1 - s)
    q = x_ref.shape[0]

    @pl.when(s == 0)
    def _():
        st_scr[...] = h0_ref[0, 0]

    xin = x_ref[...].astype(F32)
    halo = prev_ref.shape[0]
    prev_row = jnp.where(chunk == 0, 0.0, prev_ref[halo - 1:halo, :].astype(F32))
    next_row = jnp.where(chunk == n_chunks - 1, 0.0, next_ref[0:1, :].astype(F32))
    u = _ssd_conv(xin, prev_row, next_row, cw_ref, cb_ref)
    states = [st_scr[g] for g in range(SSD_GROUPS)]
    y, new_states = _ssd_chunk(u, dt_ref[...], dtT_ref[...], brow_ref[...], bcol_ref[...], arow_ref[...],
                               acol_ref[...], is_f, states, True)
    for g in range(SSD_GROUPS):
        st_scr[g] = new_states[g]
    row0 = pl.multiple_of(chunk * q, q)

    @pl.when(is_f)
    def _():
        o_ref[pl.ds(row0, q), :] = y + dskip_ref[...] * u[:, :D_SSD]

    @pl.when(jnp.logical_not(is_f))
    def _():
        o_ref[pl.ds(row0, q), :] += y


def _ssd(xbc_lat_src, lat_col_block, dt_lat, dtT_lat, xbc_ctx, dt_ctx, dtT_ctx, conv_w, conv_b, dt_bias, a_log,
         d_skip, nb, ns):
    q = SSD_Q
    nl = xbc_ctx.shape[0] // nb
    nh = SSD_HEADS
    brow = jnp.zeros((1, LANES), F32).at[0, :2 * nh].set(dt_bias.reshape(-1))
    bcol = brow.reshape(LANES, 1)
    arow = a_log.reshape(2, nh)
    acol = arow.T
    cb2 = conv_b.reshape(1, XBC_DIM)
    const2 = lambda *_: (0, 0)
    param_specs = [pl.BlockSpec((3, XBC_DIM), const2), pl.BlockSpec((1, XBC_DIM), const2),
                   pl.BlockSpec((1, LANES), const2), pl.BlockSpec((LANES, 1), const2),
                   pl.BlockSpec((2, nh), const2), pl.BlockSpec((nh, 2), const2)]
    params = [conv_w, cb2, brow, bcol, arow, acol]
    state_shape = (nb, 2, SSD_GROUPS, SSD_STATE, GROUP_W)

    h0 = pl.pallas_call(
        _ssd_ctx_kernel,
        out_shape=jax.ShapeDtypeStruct(state_shape, F32),
        grid=(nb, 2),
        in_specs=[pl.BlockSpec((nl, XBC_DIM), lambda b, d: (b, 0)),
                  pl.BlockSpec((nl, LANES), lambda b, d: (b, 0)),
                  pl.BlockSpec((LANES, nl), lambda b, d: (0, b))] + param_specs,
        out_specs=pl.BlockSpec((1, 1, SSD_GROUPS, SSD_STATE, GROUP_W), lambda b, d: (b, d, 0, 0, 0)),
        compiler_params=_cparams("parallel", "arbitrary"),
    )(xbc_ctx, dt_ctx, dtT_ctx, *params)

    nc = ns // q
    halo = BF16_SUBLANES
    halo_per_chunk = q // halo
    n_halo = nb * ns // halo

    def chunk_of(d, s):
        return jnp.where(d == 0, s, nc - 1 - s)

    def prev_map(b, d, s):
        return (jnp.maximum((b * nc + chunk_of(d, s)) * halo_per_chunk - 1, 0), lat_col_block)

    def next_map(b, d, s):
        return (jnp.minimum((b * nc + chunk_of(d, s) + 1) * halo_per_chunk, n_halo - 1), lat_col_block)

    return pl.pallas_call(
        functools.partial(_ssd_lat_kernel, n_chunks=nc),
        out_shape=jax.ShapeDtypeStruct((nb * ns, D_SSD), F32),
        grid=(nb, 2, nc),
        in_specs=[pl.BlockSpec((q, XBC_DIM), lambda b, d, s: (b * nc + chunk_of(d, s), lat_col_block)),
                  pl.BlockSpec((halo, XBC_DIM), prev_map),
                  pl.BlockSpec((halo, XBC_DIM), next_map),
                  pl.BlockSpec((q, LANES), lambda b, d, s: (b * nc + chunk_of(d, s), 0)),
                  pl.BlockSpec((LANES, q), lambda b, d, s: (0, b * nc + chunk_of(d, s))),
                  pl.BlockSpec((1, 1, SSD_GROUPS, SSD_STATE, GROUP_W), lambda b, d, s: (b, d, 0, 0, 0))]
        + param_specs + [pl.BlockSpec((1, D_SSD), const2)],
        out_specs=pl.BlockSpec((ns, D_SSD), lambda b, d, s: (b, 0)),
        scratch_shapes=[pltpu.VMEM((SSD_GROUPS, SSD_STATE, GROUP_W), F32)],
        compiler_params=_cparams("parallel", "arbitrary", "arbitrary"),
    )(xbc_lat_src, xbc_lat_src, xbc_lat_src, dt_lat, dtT_lat, h0, *params,
      jnp.repeat(d_skip, SSD_HEADDIM).reshape(1, D_SSD))


def _merge_kernel(y_ref, z_ref, ga_ref, gs_ref, oa_ref, x_ref, gt1_ref, sh2_ref, sc2_ref, gssd_ref, gpost_ref,
                  gpre_ref, wba_ref, wbs_ref, wout_ref, wrhi_ref, wrlo_ref, x1_ref, h2_ref, lg_ref):
    y = y_ref[...] * _silu(z_ref[...].astype(F32))
    gw = D_SSD // SSD_GROUPS
    y = jnp.concatenate(
        [y[:, g * gw:(g + 1) * gw]
         * lax.rsqrt(jnp.mean(y[:, g * gw:(g + 1) * gw] * y[:, g * gw:(g + 1) * gw], axis=-1, keepdims=True) + EPS)
         for g in range(SSD_GROUPS)], axis=1)
    y = (y * gssd_ref[...]).astype(BF16)
    ys = jnp.dot(y, wbs_ref[...], preferred_element_type=F32)
    ya = jnp.dot(oa_ref[...], wba_ref[...], preferred_element_type=F32)
    mixed = (_sigmoid(ga_ref[...].astype(F32)) * ya + _sigmoid(gs_ref[...].astype(F32)) * ys).astype(BF16)
    mix = jnp.dot(mixed, wout_ref[...], preferred_element_type=F32)
    x1 = x_ref[...] + gt1_ref[0] * _rms(mix, gpost_ref[...])
    x1_ref[...] = x1
    h2 = _rms(x1, gpre_ref[...]) * (1.0 + sc2_ref[0]) + sh2_ref[0]
    h2_hi, h2_lo = _split_bf16(h2)
    h2_ref[...] = h2_hi
    lg_ref[...] = (jnp.dot(h2_hi, wrhi_ref[...], preferred_element_type=F32)
                   + jnp.dot(h2_lo, wrhi_ref[...], preferred_element_type=F32)
                   + jnp.dot(h2_hi, wrlo_ref[...], preferred_element_type=F32))


def _merge(y_ssd, rest, oattn, x2d, gt1, sh2, sc2, g_ssd_norm, g_post_mix, g_pre_ffn, wba, wbs, wout, w_router,
           rows_per_batch):
    m, d = x2d.shape
    tm = MERGE_TM
    tpb = rows_per_batch // tm
    row = lambda c: pl.BlockSpec((tm, d), lambda i: (i, c))
    per_batch = pl.BlockSpec((1, 1, d), lambda i: (i // tpb, 0, 0))
    vec = pl.BlockSpec((1, d), lambda i: (0, 0))
    mat = pl.BlockSpec((d, d), lambda i: (0, 0))
    wr = jnp.zeros((d, LANES), F32).at[:, :N_EXPERTS].set(w_router)
    wr_hi, wr_lo = _split_bf16(wr)
    wr_spec = pl.BlockSpec((d, LANES), lambda i: (0, 0))
    return pl.pallas_call(
        _merge_kernel,
        out_shape=[jax.ShapeDtypeStruct((m, d), F32), jax.ShapeDtypeStruct((m, d), BF16),
                   jax.ShapeDtypeStruct((m, LANES), F32)],
        grid=(m // tm,),
        in_specs=[row(0), row(2), row(3), row(4), row(0), row(0), per_batch, per_batch, per_batch, vec, vec, vec,
                  mat, mat, mat, wr_spec, wr_spec],
        out_specs=[row(0), row(0), pl.BlockSpec((tm, LANES), lambda i: (i, 0))],
        compiler_params=_cparams("parallel"),
    )(y_ssd, rest, rest, rest, oattn, x2d, gt1, sh2, sc2, g_ssd_norm, g_post_mix, g_pre_ffn, wba, wbs, wout,
      wr_hi, wr_lo)


def _first_index(hit, iota, limit):
    return jnp.min(jnp.where(hit, iota, limit), axis=0, keepdims=True)


def _route_kernel(lg_ref, bias_ref, eidx_ref, wts_ref, rank_ref, cnt_ref, carry):
    i = pl.program_id(0)
    tl = lg_ref.shape[0]

    @pl.when(i == 0)
    def _():
        carry[...] = jnp.zeros_like(carry)

    scores = _sigmoid(lg_ref[...].T[:N_EXPERTS])
    sel = scores + bias_ref[...]
    epg = EXPERTS_PER_GROUP
    io8 = lax.broadcasted_iota(jnp.int32, (epg, tl), 0)
    neg_inf = -jnp.inf

    grp = []
    for g in range(N_EXPERT_GROUPS):
        blk = sel[g * epg:(g + 1) * epg]
        m1 = jnp.max(blk, axis=0, keepdims=True)
        rest = jnp.where(io8 == _first_index(blk == m1, io8, epg), neg_inf, blk)
        grp.append(m1 + jnp.max(rest, axis=0, keepdims=True))
    gsc = jnp.concatenate(grp, axis=0)
    gkeep = jnp.zeros_like(gsc)
    for _ in range(TOPK_GROUPS):
        pick = io8 == _first_index(gsc == jnp.max(gsc, axis=0, keepdims=True), io8, N_EXPERT_GROUPS)
        gkeep = jnp.where(pick, 1.0, gkeep)
        gsc = jnp.where(pick, neg_inf, gsc)
    cand = jnp.concatenate(
        [jnp.where(gkeep[g:g + 1] > 0.5, sel[g * epg:(g + 1) * epg], neg_inf) for g in range(N_EXPERT_GROUPS)], axis=0)

    ioe = lax.broadcasted_iota(jnp.int32, (N_EXPERTS, tl), 0)
    picks, pscore = [], []
    chosen = jnp.zeros_like(cand)
    for _ in range(TOP_K):
        e_k = _first_index(cand == jnp.max(cand, axis=0, keepdims=True), ioe, N_EXPERTS)
        hit = ioe == e_k
        picks.append(e_k)
        pscore.append(jnp.sum(jnp.where(hit, scores, 0.0), axis=0, keepdims=True))
        chosen = jnp.where(hit, 1.0, chosen)
        cand = jnp.where(hit, neg_inf, cand)
    wsum = pscore[0]
    for k in range(1, TOP_K):
        wsum = wsum + pscore[k]

    ti = lax.broadcasted_iota(jnp.int32, (tl, tl), 0)
    tj = lax.broadcasted_iota(jnp.int32, (tl, tl), 1)
    before = (ti < tj).astype(BF16)
    rank = jnp.dot(chosen.astype(BF16), before, preferred_element_type=F32) + carry[...]
    total = carry[...] + jnp.sum(chosen, axis=1, keepdims=True)
    carry[...] = total
    cnt_ref[...] = jnp.broadcast_to(total, cnt_ref.shape)

    eidx_ref[...] = jnp.concatenate(picks, axis=0)
    wts_ref[...] = jnp.concatenate([p / wsum * ROUTED_SCALE for p in pscore], axis=0)
    rank_ref[...] = jnp.concatenate(
        [jnp.sum(jnp.where(ioe == e_k, rank, 0.0), axis=0, keepdims=True) for e_k in picks], axis=0).astype(jnp.int32)


def _route(logits, router_bias):
    m = logits.shape[0]
    ne = N_EXPERTS
    tl = ROUTE_TL
    tok = lambda dt: jax.ShapeDtypeStruct((TOP_K, m), dt)
    tok_spec = pl.BlockSpec((TOP_K, tl), lambda i: (0, i))
    return pl.pallas_call(
        _route_kernel,
        out_shape=[tok(jnp.int32), tok(F32), tok(jnp.int32), jax.ShapeDtypeStruct((ne, LANES), F32)],
        grid=(m // tl,),
        in_specs=[pl.BlockSpec((tl, LANES), lambda i: (i, 0)), pl.BlockSpec((ne, 1), lambda i: (0, 0))],
        out_specs=[tok_spec, tok_spec, tok_spec, pl.BlockSpec((ne, LANES), lambda i: (0, 0))],
        scratch_shapes=[pltpu.VMEM((ne, 1), F32)],
        compiler_params=_cparams("arbitrary"),
    )(logits, router_bias.reshape(ne, 1))


def _expert_kernel(tile_ref, e_ref, lo_ref, hi_ref, x_ref, wg_ref, wu_ref, wd_ref, o_ref, wg_s, wu_s, wd_s):
    w = pl.program_id(0)
    lo, hi = lo_ref[w], hi_ref[w]
    tm = x_ref.shape[0]
    base = tile_ref[w] * tm

    @pl.when((w == 0) | (e_ref[w] != e_ref[jnp.maximum(w - 1, 0)]))
    def _():
        wg_s[...] = wg_ref[0].astype(BF16)
        wu_s[...] = wu_ref[0].astype(BF16)
        wd_s[...] = wd_ref[0].astype(BF16)

    @pl.when(hi > lo)
    def _():
        x = x_ref[...]
        gate = jnp.dot(x, wg_s[...], preferred_element_type=F32)
        up = jnp.dot(x, wu_s[...], preferred_element_type=F32)
        y = jnp.dot((_silu(gate) * up).astype(BF16), wd_s[...], preferred_element_type=F32).astype(o_ref.dtype)
        rid = base + lax.broadcasted_iota(jnp.int32, (tm, 1), 0)

        @pl.when(lo == base)
        def _():
            o_ref[...] = jnp.where(rid < hi, y, jnp.zeros_like(y))

        @pl.when(lo != base)
        def _():
            o_ref[...] = jnp.where((rid >= lo) & (rid < hi), y, o_ref[...])


def _drop_alias_ref(kernel_fn, pos):
    def body(*refs):
        return kernel_fn(*refs[:pos], *refs[pos + 1:])
    return body


def _experts(item_tile, item_e, item_lo, item_hi, xs, w_gate, w_up, w_down, y_all, chunk, n_rows):
    rows_c, d = xs.shape
    tm = EXPERT_TM
    ff = w_gate.shape[-1]
    tile0 = chunk * (rows_c // tm)
    row_map = lambda w, t, e, lo, hi: (t[w], 0)
    in_specs = [pl.BlockSpec((tm, d), row_map),
                pl.BlockSpec((1, d, ff), lambda w, t, e, lo, hi: (e[w], 0, 0)),
                pl.BlockSpec((1, d, ff), lambda w, t, e, lo, hi: (e[w], 0, 0)),
                pl.BlockSpec((1, ff, d), lambda w, t, e, lo, hi: (e[w], 0, 0))]
    args = [item_tile, item_e, item_lo, item_hi, xs, w_gate, w_up, w_down]
    body, aliases = _expert_kernel, {}
    if y_all is not None:
        in_specs.append(pl.BlockSpec(memory_space=pl.ANY))
        aliases = {len(args): 0}
        body = _drop_alias_ref(_expert_kernel, len(args))
        args.append(y_all)
    return pl.pallas_call(
        body,
        out_shape=jax.ShapeDtypeStruct((n_rows, d), BF16),
        grid_spec=pltpu.PrefetchScalarGridSpec(
            num_scalar_prefetch=4,
            grid=(item_tile.shape[0],),
            in_specs=in_specs,
            out_specs=pl.BlockSpec((tm, d), lambda w, t, e, lo, hi: (tile0 + t[w], 0)),
            scratch_shapes=[pltpu.VMEM((d, ff), BF16), pltpu.VMEM((d, ff), BF16), pltpu.VMEM((ff, d), BF16)]),
        input_output_aliases=aliases,
        compiler_params=_cparams("arbitrary"),
    )(*args)


def _final_kernel(x1_ref, h2_ref, yk_ref, wt_ref, gt2_ref, gpost_ref, wg_ref, wu_ref, wd_ref, o_ref):
    h2 = h2_ref[...]
    gate = jnp.dot(h2, wg_ref[...], preferred_element_type=F32)
    up = jnp.dot(h2, wu_ref[...], preferred_element_type=F32)
    f = jnp.dot((_silu(gate) * up).astype(BF16), wd_ref[...], preferred_element_type=F32)
    wt = wt_ref[...]
    for k in range(TOP_K):
        f = f + yk_ref[k].astype(F32) * wt[:, k:k + 1]
    o_ref[...] = x1_ref[...] + gt2_ref[0] * _rms(f, gpost_ref[...])


def _final(x1, h2, y_tok, wts_t, gt2, g_post_ffn, w_sh_gate, w_sh_up, w_sh_down, rows_per_batch, out_all, chunk):
    m, d = x1.shape
    rows_c = y_tok.shape[1]
    tm = FINAL_TM
    tpb = rows_per_batch // tm
    off = chunk * (rows_c // tm)
    row = pl.BlockSpec((tm, d), lambda i: (off + i, 0))
    ff = w_sh_gate.shape[1]
    in_specs = [row, row, pl.BlockSpec((TOP_K, tm, d), lambda i: (0, i, 0)),
                pl.BlockSpec((tm, TOP_K), lambda i: (off + i, 0)),
                pl.BlockSpec((1, 1, d), lambda i: ((off + i) // tpb, 0, 0)),
                pl.BlockSpec((1, d), lambda i: (0, 0)),
                pl.BlockSpec((d, ff), lambda i: (0, 0)), pl.BlockSpec((d, ff), lambda i: (0, 0)),
                pl.BlockSpec((ff, d), lambda i: (0, 0))]
    args = [x1, h2, y_tok, wts_t, gt2, g_post_ffn, w_sh_gate, w_sh_up, w_sh_down]
    body, aliases = _final_kernel, {}
    if out_all is not None:
        in_specs.append(pl.BlockSpec(memory_space=pl.ANY))
        aliases = {len(args): 0}
        body = _drop_alias_ref(_final_kernel, len(args))
        args.append(out_all)
    return pl.pallas_call(
        body,
        out_shape=jax.ShapeDtypeStruct((m, d), F32),
        grid=(rows_c // tm,),
        in_specs=in_specs,
        out_specs=row,
        input_output_aliases=aliases,
        compiler_params=_cparams("parallel"),
    )(*args)


def kernel(x, c, ctx, c_ctx, w_ada, b_ada, g_pre_mix, g_post_mix, g_pre_ffn, g_post_ffn, w_in, lam_q1, lam_k1, lam_q2, lam_k2, g_attn_subln, conv_w, conv_b, dt_bias, a_log, d_skip, g_ssd_norm, w_branch_attn, w_branch_ssd, w_out, w_router, router_bias, w_e_gate, w_e_up, w_e_down, w_sh_gate, w_sh_up, w_sh_down):
    nb, ns, d = x.shape
    nl = ctx.shape[1]
    m = nb * ns
    li = 0

    c_all = jnp.zeros((3 * SUBLANES, d), F32).at[:nb].set(c).at[nb].set(c_ctx)
    mod = _modulation(c_all, w_ada[li], b_ada[li])
    sh1, sc1, gt1, sh2, sc2, gt2 = (mod[:nb, k * d:(k + 1) * d].reshape(nb, 1, d) for k in range(6))
    sh1c, sc1c = (mod[nb:nb + 1, k * d:(k + 1) * d].reshape(1, 1, d) for k in range(2))

    qk_dim = N_HEADS * 2 * ATTN_DH
    v_dim = N_HEADS * ATTN_DV
    o_q, o_k, o_v, o_z = 0, qk_dim, 2 * qk_dim, 2 * qk_dim + v_dim
    o_xbc = o_z + D_SSD
    o_dt = o_xbc + XBC_DIM
    o_g = o_dt + 2 * SSD_HEADS
    wi = w_in[li]
    w_qkv = wi[:, o_q:o_z].astype(BF16)
    w_qk_rot = _rotate_half_columns(wi[:, o_q:o_v]).astype(BF16)
    w_kv = wi[:, o_k:o_z].astype(BF16)
    w_xbc = wi[:, o_xbc:o_dt].astype(BF16)
    w_rest = jnp.concatenate([wi[:, o_xbc:o_dt], wi[:, o_z:o_xbc], wi[:, o_g:]], axis=1).astype(BF16)
    w_dt = jnp.zeros((d, LANES), BF16).at[:, :2 * SSD_HEADS].set(wi[:, o_dt:o_g].astype(BF16))

    x2d = x.reshape(m, d)
    c2d = ctx.reshape(nb * nl, d)
    g1 = g_pre_mix[li].reshape(1, d)
    cos, sin = _rope_tables(ns)
    qkv = _project(x2d, sh1, sc1, g1, w_qkv, rows_per_mod=ns, rope=(w_qk_rot, cos, sin, qk_dim))
    rest, dt_lat, dtT_lat = _project(x2d, sh1, sc1, g1, w_rest, rows_per_mod=ns, w_dt=w_dt)
    kv_ctx = _project(c2d, sh1c, sc1c, g1, w_kv, rows_per_mod=nb * nl)
    xbc_ctx, dt_ctx, dtT_ctx = _project(c2d, sh1c, sc1c, g1, w_xbc, rows_per_mod=nb * nl, w_dt=w_dt)

    lam = (jnp.exp(jnp.sum(lam_q1[li] * lam_k1[li])) - jnp.exp(jnp.sum(lam_q2[li] * lam_k2[li])) + LAM_INIT)
    oattn = _diff_attention(lam.reshape(1).astype(F32), qkv, kv_ctx, g_attn_subln[li].reshape(1, ATTN_DV), nb, ns)

    y_ssd = _ssd(rest, 0, dt_lat, dtT_lat, xbc_ctx, dt_ctx, dtT_ctx, conv_w[li], conv_b[li], dt_bias[li], a_log[li],
                 d_skip[li], nb, ns)

    x1, h2, logits = _merge(
        y_ssd, rest, oattn, x2d, gt1, sh2, sc2, g_ssd_norm[li].reshape(1, d), g_post_mix[li].reshape(1, d),
        g_pre_ffn[li].reshape(1, d), w_branch_attn[li].astype(BF16), w_branch_ssd[li].astype(BF16),
        w_out[li].astype(BF16), w_router[li], ns)

    eidx, wts, rank, counts = _route(logits, router_bias[li])

    tm = EXPERT_TM
    n_rows = m * TOP_K
    n_tiles = n_rows // tm
    counts = counts[:, 0].astype(jnp.int32)
    ends = jnp.cumsum(counts)
    start = ends - counts
    eid = jnp.arange(N_EXPERTS, dtype=jnp.int32)
    dest = jnp.sum(jnp.where(eidx[None] == eid[:, None, None], start[:, None, None], 0), axis=0) + rank
    tok = jnp.arange(m, dtype=jnp.int32)
    slot_tok = jnp.sort((eidx * m + tok[None, :]).reshape(-1)) % m

    rows_c = n_rows // MOE_CHUNKS
    tiles_c = rows_c // tm
    y_sorted = None
    for ch in range(MOE_CHUNKS):
        row0 = ch * rows_c
        xs = h2.at[slot_tok[row0:row0 + rows_c]].get(mode="promise_in_bounds")
        cuts = jnp.sort(jnp.concatenate([jnp.arange(tiles_c, dtype=jnp.int32) * tm,
                                         jnp.clip(start[1:] - row0, 0, rows_c)]))
        item_hi = jnp.concatenate([cuts[1:], jnp.full((1,), rows_c, jnp.int32)])
        item_tile = jnp.minimum(cuts // tm, tiles_c - 1)
        item_e = jnp.minimum(jnp.sum((ends[None, :] <= (cuts + row0)[:, None]).astype(jnp.int32), axis=1),
                             N_EXPERTS - 1)
        y_sorted = _experts(item_tile, item_e, cuts, item_hi, xs, w_e_gate[li], w_e_up[li], w_e_down[li],
                            y_sorted, ch, n_rows)

    tok_c = m // MOE_CHUNKS
    wts_t = wts.T
    w_sh = (w_sh_gate[li].astype(BF16), w_sh_up[li].astype(BF16), w_sh_down[li].astype(BF16))
    out = None
    for ch in range(MOE_CHUNKS):
        dest_c = dest[:, ch * tok_c:(ch + 1) * tok_c].reshape(-1)
        y_tok = y_sorted.at[dest_c].get(mode="promise_in_bounds").reshape(TOP_K, tok_c, d)
        out = _final(x1, h2, y_tok, wts_t, gt2, g_post_ffn[li].reshape(1, d), *w_sh, ns, out, ch)
    return out.reshape(nb, ns, d)
```

```python
import functools
import math

import jax
import jax.numpy as jnp
from jax import lax
from jax.experimental import pallas as pl
from jax.experimental.pallas import tpu as pltpu

F32 = jnp.float32
BF16 = jnp.bfloat16
HIGHEST = lax.Precision.HIGHEST

D_MODEL = 1024
CTX_LEN = 256
GRID_W = 64
EPS = 1e-6

N_HEADS = 8
ATTN_DH = 64
ATTN_DV = 128
ATTN_SCALE = ATTN_DH ** -0.5
LOG2E = math.log2(math.e)
ROPE_THETA = 10000.0
ROPE_AXIS_DIM = ATTN_DH // 2
LAM_INIT = 0.8 - 0.6 * math.exp(-0.3 * 0)

D_SSD = 1024
SSD_HEADDIM = 64
SSD_HEADS = 16
SSD_GROUPS = 4
SSD_HPG = 4
SSD_STATE = 128
XBC_DIM = D_SSD + 2 * SSD_GROUPS * SSD_STATE
GROUP_W = SSD_HPG * SSD_HEADDIM

N_EXPERTS = 64
EXPERT_FF = 256
TOP_K = 8
N_EXPERT_GROUPS = 8
EXPERTS_PER_GROUP = N_EXPERTS // N_EXPERT_GROUPS
TOPK_GROUPS = 4
ROUTED_SCALE = 2.5

LANES = 128
SUBLANES = 8
BF16_SUBLANES = 16
VMEM_LIMIT = 52 * 1024 * 1024

PROJ_TM = 512
PROJ_TN = 512
ATTN_TQ = 2048
ATTN_TH = 256
ATTN_KB = 256
SSD_Q = 256
MERGE_TM = 512
MERGE_SUB = 256
ROUTE_TL = 512
EXPERT_TM = 1024
FINAL_TM = 512
MOE_CHUNKS = 4
COMBINE_CHUNKS = 8

NEG_BIG = -1e30

_NT = (((1,), (1,)), ((), ()))
_TN = (((0,), (0,)), ((), ()))


def _cparams(*sem):
    return pltpu.CompilerParams(dimension_semantics=sem, vmem_limit_bytes=VMEM_LIMIT)


def _sigmoid(v):
    return 0.5 * jnp.tanh(0.5 * v) + 0.5


def _silu(v):
    return v * _sigmoid(v)


def _softplus(v):
    return jnp.maximum(v, 0.0) + jnp.log1p(jnp.exp(-jnp.abs(v)))


def _rms(v, g):
    return v * lax.rsqrt(jnp.mean(v * v, axis=-1, keepdims=True) + EPS) * g


def _split_bf16(v):
    hi = v.astype(BF16)
    return hi, (v - hi.astype(F32)).astype(BF16)


def _mod_kernel(c_ref, w_ref, b_ref, o_ref):
    o_ref[...] = jnp.dot(_silu(c_ref[...]), w_ref[...], preferred_element_type=F32,
                         precision=HIGHEST) + b_ref[...]


def _modulation(c_all, w_ada, b_ada):
    rows, d = c_all.shape
    n = w_ada.shape[1]
    tn = 1024
    return pl.pallas_call(
        _mod_kernel,
        out_shape=jax.ShapeDtypeStruct((rows, n), F32),
        grid=(n // tn,),
        in_specs=[pl.BlockSpec((rows, d), lambda j: (0, 0)),
                  pl.BlockSpec((d, tn), lambda j: (0, j)),
                  pl.BlockSpec((1, tn), lambda j: (0, j))],
        out_specs=pl.BlockSpec((rows, tn), lambda j: (0, j)),
        compiler_params=_cparams("arbitrary"),
    )(c_all, w_ada, b_ada.reshape(1, n))


def _proj_kernel(*refs, n_rot, n_q, has_dt):
    it = iter(refs)
    x_ref, sh_ref, sc_ref, g_ref, w_ref = (next(it) for _ in range(5))
    if n_rot:
        wrot_ref, cos_ref, sin_ref = next(it), next(it), next(it)
    if has_dt:
        wdt_ref, wdtT_ref = next(it), next(it)
    o_ref = next(it)
    if has_dt:
        dt_ref, dtT_ref = next(it), next(it)

    h = (_rms(x_ref[...], g_ref[...]) * (1.0 + sc_ref[0]) + sh_ref[0]).astype(BF16)
    if has_dt:
        dt_ref[...] = jnp.dot(h, wdt_ref[...], preferred_element_type=F32)
        dtT_ref[...] = lax.dot_general(wdtT_ref[...], h, _NT, preferred_element_type=F32)
    tn = PROJ_TN
    for c in range(w_ref.shape[1] // tn):
        cs = slice(c * tn, (c + 1) * tn)
        u = jnp.dot(h, w_ref[:, cs], preferred_element_type=F32)
        if c * tn < n_rot:
            ur = jnp.dot(h, wrot_ref[:, cs], preferred_element_type=F32)
            scale = ATTN_SCALE * LOG2E if c * tn < n_q else 1.0
            cos = cos_ref[...] * scale
            sin = sin_ref[...] * scale
            for s in range(tn // LANES):
                sl = slice(s * LANES, (s + 1) * LANES)
                o_ref[:, c * tn + s * LANES:c * tn + (s + 1) * LANES] = (
                    u[:, sl] * cos + ur[:, sl] * sin).astype(o_ref.dtype)
        else:
            o_ref[:, cs] = u.astype(o_ref.dtype)


def _project(x2d, shift, scale, gain, w, *, rows_per_mod, rope=None, w_dt=None):
    m, d = x2d.shape
    n = w.shape[1]
    tm = min(PROJ_TM, rows_per_mod)
    tiles_per_mod = rows_per_mod // tm
    has_dt = w_dt is not None
    const = lambda i: (0, 0)
    in_specs = [pl.BlockSpec((tm, d), lambda i: (i, 0)),
                pl.BlockSpec((1, 1, d), lambda i: (i // tiles_per_mod, 0, 0)),
                pl.BlockSpec((1, 1, d), lambda i: (i // tiles_per_mod, 0, 0)),
                pl.BlockSpec((1, d), const),
                pl.BlockSpec((d, n), const)]
    args = [x2d, shift, scale, gain, w]
    out_shape = [jax.ShapeDtypeStruct((m, n), BF16)]
    out_specs = [pl.BlockSpec((tm, n), lambda i: (i, 0))]
    n_rot = n_q = 0
    if rope is not None:
        w_rot, cos, sin, n_q = rope
        n_rot = w_rot.shape[1]
        pos_tiles = cos.shape[0] // tm
        in_specs += [pl.BlockSpec((d, n_rot), const)] + [pl.BlockSpec((tm, LANES), lambda i: (i % pos_tiles, 0))] * 2
        args += [w_rot, cos, sin]
    if has_dt:
        in_specs += [pl.BlockSpec((d, LANES), const), pl.BlockSpec((LANES, d), const)]
        args += [w_dt, w_dt.T]
        out_shape += [jax.ShapeDtypeStruct((m, LANES), F32), jax.ShapeDtypeStruct((LANES, m), F32)]
        out_specs += [pl.BlockSpec((tm, LANES), lambda i: (i, 0)), pl.BlockSpec((LANES, tm), lambda i: (0, i))]
    res = pl.pallas_call(
        functools.partial(_proj_kernel, n_rot=n_rot, n_q=n_q, has_dt=has_dt),
        out_shape=out_shape,
        grid=(m // tm,),
        in_specs=in_specs,
        out_specs=out_specs,
        compiler_params=_cparams("parallel"),
    )(*args)
    return res if has_dt else res[0]


def _rope_tables(n_tok):
    rows = n_tok // GRID_W
    row = jnp.repeat(jnp.arange(rows, dtype=F32), GRID_W)
    col = jnp.broadcast_to(jnp.arange(GRID_W, dtype=F32)[None, :], (rows, GRID_W)).reshape(-1)
    inv_freq = ROPE_THETA ** (-jnp.arange(0, ROPE_AXIS_DIM, 2, dtype=F32) / ROPE_AXIS_DIM)
    ang = jnp.concatenate([row[:, None] * inv_freq, col[:, None] * inv_freq], axis=-1)
    ang = jnp.concatenate([ang, ang, ang, ang], axis=-1)
    return jnp.cos(ang), jnp.sin(ang)


def _rotate_half_columns(w):
    d, n = w.shape
    half = ATTN_DH // 2
    wc = w.reshape(d, n // ATTN_DH, 2, half)
    return jnp.stack([-wc[:, :, 1], wc[:, :, 0]], axis=2).reshape(d, n)


def _attn_kernel(lam_ref, q_ref, kc_ref, k_ref, vc_ref, v_ref, g_ref, o_ref, s_scr, e_scr):
    th, kb = ATTN_TH, ATTN_KB
    nsub = q_ref.shape[0] // th
    lam = lam_ref[0]
    blocks = ([(kc_ref, vc_ref, o) for o in range(0, kc_ref.shape[0], kb)]
              + [(k_ref, v_ref, o) for o in range(0, k_ref.shape[0], kb)])

    def stacked_q(t):
        q = q_ref[t * th:(t + 1) * th, :]
        lane = lax.broadcasted_iota(jnp.int32, q.shape, 1)
        zero = jnp.zeros_like(q)
        return jnp.concatenate([jnp.where(lane < ATTN_DH, q, zero), jnp.where(lane >= ATTN_DH, q, zero)], axis=0)

    def lane_fold(v, op):
        out = v[:, :LANES]
        for c in range(1, v.shape[1] // LANES):
            out = op(out, v[:, c * LANES:(c + 1) * LANES])
        return out

    def scores(slot, qq, j, mx):
        k_ref_j, _, off = blocks[j]
        s = lax.dot_general(qq, k_ref_j[off:off + kb, :], _NT, preferred_element_type=F32)
        s_scr[slot, :, j * kb:(j + 1) * kb] = s
        part = lane_fold(s, jnp.maximum)
        return part if mx is None else jnp.maximum(mx, part)

    def expo(slot, m, j, sm):
        e = jnp.exp2(s_scr[slot, :, j * kb:(j + 1) * kb] - m)
        e_scr[slot, :, j * kb:(j + 1) * kb] = e
        part = lane_fold(e, jnp.add)
        return part if sm is None else sm + part

    def pv(slot, a0, a1, j, acc):
        _, v_ref_j, off = blocks[j]
        e = e_scr[slot, :, j * kb:(j + 1) * kb]
        w = (e[:th] * a0 - e[th:] * a1).astype(BF16)
        o = jnp.dot(w, v_ref_j[off:off + kb, :], preferred_element_type=F32)
        return o if acc is None else acc + o

    st = [dict() for _ in range(nsub)]
    for phase in range(nsub + 2):
        ts, te, tp = phase, phase - 1, phase - 2
        if 0 <= ts < nsub:
            st[ts].update(qq=stacked_q(ts), mx=None)
        if 0 <= te < nsub:
            st[te].update(m=jnp.max(st[te]["mx"], axis=-1, keepdims=True), sm=None)
        if 0 <= tp < nsub:
            inv = 1.0 / jnp.sum(st[tp]["sm"], axis=-1, keepdims=True)
            st[tp].update(a0=inv[:th], a1=inv[th:] * lam, acc=None)
        for j in range(len(blocks)):
            if 0 <= ts < nsub:
                st[ts]["mx"] = scores(ts % 2, st[ts]["qq"], j, st[ts]["mx"])
            if 0 <= tp < nsub:
                st[tp]["acc"] = pv(tp % 2, st[tp]["a0"], st[tp]["a1"], j, st[tp]["acc"])
            if 0 <= te < nsub:
                st[te]["sm"] = expo(te % 2, st[te]["m"], j, st[te]["sm"])
        if 0 <= tp < nsub:
            o = _rms(st[tp]["acc"], g_ref[...]) * (1.0 - LAM_INIT)
            o_ref[tp * th:(tp + 1) * th, :] = o.astype(o_ref.dtype)


def _diff_attention(lam, qkv, kv_ctx, g_subln, nb, ns):
    tq = ATTN_TQ
    nq = ns // tq
    nl = kv_ctx.shape[0] // nb
    stage_buf = pltpu.VMEM((2, 2 * ATTN_TH, nl + ns), F32)
    return pl.pallas_call(
        _attn_kernel,
        out_shape=jax.ShapeDtypeStruct((nb * ns, N_HEADS * ATTN_DV), BF16),
        grid=(nb, N_HEADS, nq),
        in_specs=[pl.BlockSpec(memory_space=pltpu.SMEM),
                  pl.BlockSpec((tq, LANES), lambda b, h, i: (b * nq + i, h)),
                  pl.BlockSpec((nl, LANES), lambda b, h, i: (b, h)),
                  pl.BlockSpec((ns, LANES), lambda b, h, i: (b, N_HEADS + h)),
                  pl.BlockSpec((nl, LANES), lambda b, h, i: (b, N_HEADS + h)),
                  pl.BlockSpec((ns, LANES), lambda b, h, i: (b, 2 * N_HEADS + h)),
                  pl.BlockSpec((1, LANES), lambda b, h, i: (0, 0))],
        out_specs=pl.BlockSpec((tq, LANES), lambda b, h, i: (b * nq + i, h)),
        scratch_shapes=[stage_buf, stage_buf],
        compiler_params=_cparams("parallel", "parallel", "arbitrary"),
    )(lam, qkv, kv_ctx, qkv, kv_ctx, qkv, g_subln)


def _ssd_conv(xin, prev_row, next_row, cw_ref, cb_ref):
    q = xin.shape[0]
    rid = lax.broadcasted_iota(jnp.int32, xin.shape, 0)
    up = jnp.where(rid == 0, prev_row, pltpu.roll(xin, 1, 0))
    dn = jnp.where(rid == q - 1, next_row, pltpu.roll(xin, q - 1, 0))
    return _silu(cb_ref[...] + up * cw_ref[0:1, :] + xin * cw_ref[1:2, :] + dn * cw_ref[2:3, :])


def _split3(v):
    a = v.astype(BF16)
    r = v - a.astype(F32)
    b = r.astype(BF16)
    return a, b, (r - b.astype(F32)).astype(BF16)


def _split_dot(v, e_bf16):
    hi, lo = _split_bf16(v)
    return jnp.dot(jnp.concatenate([hi, lo], axis=1), jnp.concatenate([e_bf16, e_bf16], axis=0),
                   preferred_element_type=F32)


def _ssd_chunk(u, dt_blk, dtT_blk, bias_row, bias_col, alog_row, alog_col, is_f, states, want_y):
    q = u.shape[0]
    nh = SSD_HEADS
    xs = u[:, :D_SSD]
    bm = u[:, D_SSD:D_SSD + SSD_GROUPS * SSD_STATE]
    cm = u[:, D_SSD + SSD_GROUPS * SSD_STATE:]

    dt_all = _softplus(dt_blk + bias_row)
    dtT_all = _softplus(dtT_blk + bias_col)
    dt = jnp.where(is_f, dt_all[:, 0:nh], dt_all[:, nh:2 * nh])
    dtT = jnp.where(is_f, dtT_all[0:nh, :], dtT_all[nh:2 * nh, :])
    a_r = -jnp.exp(jnp.where(is_f, alog_row[0:1, :], alog_row[1:2, :]))
    a_c = -jnp.exp(jnp.where(is_f, alog_col[:, 0:1], alog_col[:, 1:2]))

    ri = lax.broadcasted_iota(jnp.int32, (q, q), 0)
    ci = lax.broadcasted_iota(jnp.int32, (q, q), 1)
    lower = (ri >= ci).astype(BF16)
    upper = (ri <= ci).astype(BF16)
    tmat = jnp.where(is_f, lower, upper)
    tmat_t = jnp.where(is_f, upper, lower)
    causal = tmat > 0.5
    acum = sum(jnp.dot(tmat, p, preferred_element_type=F32) for p in _split3(dt * a_r))
    acum_t = sum(jnp.dot(p, tmat_t, preferred_element_type=F32) for p in _split3(dtT * a_c))
    tot = jnp.where(is_f, acum[q - 1:q, :], acum[0:1, :])
    wgt = jnp.exp(tot - acum) * dt

    hid = lax.broadcasted_iota(jnp.int32, (nh, D_SSD), 0)
    cid = lax.broadcasted_iota(jnp.int32, (nh, D_SSD), 1)
    expand = (cid // SSD_HEADDIM == hid).astype(BF16)
    pieces = [wgt, jnp.broadcast_to(jnp.exp(tot), (SUBLANES, nh))]
    if want_y:
        pieces.append(jnp.exp(acum))
    wide = _split_dot(jnp.concatenate(pieces, axis=0), expand)
    wgt_x = wide[0:q]
    dec_x = wide[q:q + 1]
    if want_y:
        ea_x = wide[q + SUBLANES:2 * q + SUBLANES]
        src_t = acum_t - jnp.log(dtT)

    ys, new_states = [], []
    for g in range(SSD_GROUPS):
        gsl = slice(g * GROUP_W, (g + 1) * GROUP_W)
        nsl = slice(g * SSD_STATE, (g + 1) * SSD_STATE)
        xg = xs[:, gsl]
        bm_g = bm[:, nsl].astype(BF16)
        upd = lax.dot_general(bm_g, (xg * wgt_x[:, gsl]).astype(BF16), _TN, preferred_element_type=F32)
        new_states.append(upd if states is None else states[g] * dec_x[:, gsl] + upd)
        if not want_y:
            continue
        cm_g = cm[:, nsl].astype(BF16)
        cb = lax.dot_general(cm_g, bm_g, _NT, preferred_element_type=F32)
        xb = xg.astype(BF16)
        y_heads = []
        for r in range(SSD_HPG):
            h = g * SSD_HPG + r
            seg = acum[:, h:h + 1] - src_t[h:h + 1, :]
            m_h = (cb * jnp.exp(jnp.where(causal, seg, NEG_BIG))).astype(BF16)
            y_heads.append(jnp.dot(m_h, xb[:, r * SSD_HEADDIM:(r + 1) * SSD_HEADDIM], preferred_element_type=F32))
        y_g = jnp.concatenate(y_heads, axis=1)
        if states is not None:
            y_g = y_g + jnp.dot(cm_g, states[g].astype(BF16), preferred_element_type=F32) * ea_x[:, gsl]
        ys.append(y_g)
    return (jnp.concatenate(ys, axis=1) if want_y else None), new_states


def _ssd_ctx_kernel(x_ref, dt_ref, dtT_ref, cw_ref, cb_ref, brow_ref, bcol_ref, arow_ref, acol_ref, st_ref):
    is_f = pl.program_id(1) == 0
    xin = x_ref[...].astype(F32)
    zero_row = jnp.zeros((1, xin.shape[1]), F32)
    u = _ssd_conv(xin, zero_row, zero_row, cw_ref, cb_ref)
    _, new_states = _ssd_chunk(u, dt_ref[...], dtT_ref[...], brow_ref[...], bcol_ref[...], arow_ref[...],
                               acol_ref[...], is_f, None, False)
    for g in range(SSD_GROUPS):
        st_ref[0, 0, g] = new_states[g]


def _ssd_lat_kernel(x_ref, prev_ref, next_ref, dt_ref, dtT_ref, h0_ref, cw_ref, cb_ref, brow_ref, bcol_ref,
                    arow_ref, acol_ref, dskip_ref, o_ref, st_scr, *, n_chunks):
    d = pl.program_id(1)
    s = pl.program_id(2)
    is_f = d == 0
    chunk = jnp.where(is_f, s, n_chunks - 1 - s)
    q = x_ref.shape[0]

    @pl.when(s == 0)
    def _():
        st_scr[...] = h0_ref[0, 0]

    xin = x_ref[...].astype(F32)
    halo = prev_ref.shape[0]
    prev_row = jnp.where(chunk == 0, 0.0, prev_ref[halo - 1:halo, :].astype(F32))
    next_row = jnp.where(chunk == n_chunks - 1, 0.0, next_ref[0:1, :].astype(F32))
    u = _ssd_conv(xin, prev_row, next_row, cw_ref, cb_ref)
    states = [st_scr[g] for g in range(SSD_GROUPS)]
    y, new_states = _ssd_chunk(u, dt_ref[...], dtT_ref[...], brow_ref[...], bcol_ref[...], arow_ref[...],
                               acol_ref[...], is_f, states, True)
    for g in range(SSD_GROUPS):
        st_scr[g] = new_states[g]
    row0 = pl.multiple_of(chunk * q, q)

    @pl.when(is_f)
    def _():
        o_ref[pl.ds(row0, q), :] = y + dskip_ref[...] * u[:, :D_SSD]

    @pl.when(jnp.logical_not(is_f))
    def _():
        o_ref[pl.ds(row0, q), :] += y


def _ssd(xbc_lat_src, lat_col_block, dt_lat, dtT_lat, xbc_ctx, dt_ctx, dtT_ctx, conv_w, conv_b, dt_bias, a_log,
         d_skip, nb, ns):
    q = SSD_Q
    nl = xbc_ctx.shape[0] // nb
    nh = SSD_HEADS
    brow = jnp.zeros((1, LANES), F32).at[0, :2 * nh].set(dt_bias.reshape(-1))
    bcol = brow.reshape(LANES, 1)
    arow = a_log.reshape(2, nh)
    acol = arow.T
    cb2 = conv_b.reshape(1, XBC_DIM)
    const2 = lambda *_: (0, 0)
    param_specs = [pl.BlockSpec((3, XBC_DIM), const2), pl.BlockSpec((1, XBC_DIM), const2),
                   pl.BlockSpec((1, LANES), const2), pl.BlockSpec((LANES, 1), const2),
                   pl.BlockSpec((2, nh), const2), pl.BlockSpec((nh, 2), const2)]
    params = [conv_w, cb2, brow, bcol, arow, acol]
    state_shape = (nb, 2, SSD_GROUPS, SSD_STATE, GROUP_W)

    h0 = pl.pallas_call(
        _ssd_ctx_kernel,
        out_shape=jax.ShapeDtypeStruct(state_shape, F32),
        grid=(nb, 2),
        in_specs=[pl.BlockSpec((nl, XBC_DIM), lambda b, d: (b, 0)),
                  pl.BlockSpec((nl, LANES), lambda b, d: (b, 0)),
                  pl.BlockSpec((LANES, nl), lambda b, d: (0, b))] + param_specs,
        out_specs=pl.BlockSpec((1, 1, SSD_GROUPS, SSD_STATE, GROUP_W), lambda b, d: (b, d, 0, 0, 0)),
        compiler_params=_cparams("parallel", "arbitrary"),
    )(xbc_ctx, dt_ctx, dtT_ctx, *params)

    nc = ns // q
    halo = BF16_SUBLANES
    halo_per_chunk = q // halo
    n_halo = nb * ns // halo

    def chunk_of(d, s):
        return jnp.where(d == 0, s, nc - 1 - s)

    def prev_map(b, d, s):
        return (jnp.maximum((b * nc + chunk_of(d, s)) * halo_per_chunk - 1, 0), lat_col_block)

    def next_map(b, d, s):
        return (jnp.minimum((b * nc + chunk_of(d, s) + 1) * halo_per_chunk, n_halo - 1), lat_col_block)

    return pl.pallas_call(
        functools.partial(_ssd_lat_kernel, n_chunks=nc),
        out_shape=jax.ShapeDtypeStruct((nb * ns, D_SSD), F32),
        grid=(nb, 2, nc),
        in_specs=[pl.BlockSpec((q, XBC_DIM), lambda b, d, s: (b * nc + chunk_of(d, s), lat_col_block)),
                  pl.BlockSpec((halo, XBC_DIM), prev_map),
                  pl.BlockSpec((halo, XBC_DIM), next_map),
                  pl.BlockSpec((q, LANES), lambda b, d, s: (b * nc + chunk_of(d, s), 0)),
                  pl.BlockSpec((LANES, q), lambda b, d, s: (0, b * nc + chunk_of(d, s))),
                  pl.BlockSpec((1, 1, SSD_GROUPS, SSD_STATE, GROUP_W), lambda b, d, s: (b, d, 0, 0, 0))]
        + param_specs + [pl.BlockSpec((1, D_SSD), const2)],
        out_specs=pl.BlockSpec((ns, D_SSD), lambda b, d, s: (b, 0)),
        scratch_shapes=[pltpu.VMEM((SSD_GROUPS, SSD_STATE, GROUP_W), F32)],
        compiler_params=_cparams("parallel", "arbitrary", "arbitrary"),
    )(xbc_lat_src, xbc_lat_src, xbc_lat_src, dt_lat, dtT_lat, h0, *params,
      jnp.repeat(d_skip, SSD_HEADDIM).reshape(1, D_SSD))


def _merge_kernel(y_ref, z_ref, ga_ref, gs_ref, oa_ref, x_ref, gt1_ref, sh2_ref, sc2_ref, gssd_ref, gpost_ref,
                  gpre_ref, wba_ref, wbs_ref, wout_ref, wrhi_ref, wrlo_ref, x1_ref, h2_ref, lg_ref):
    gw = D_SSD // SSD_GROUPS
    sub = MERGE_SUB
    nsub = y_ref.shape[0] // sub

    def gated_norm(r):
        y = y_ref[r, :] * _silu(z_ref[r, :].astype(F32))
        y = jnp.concatenate(
            [y[:, g * gw:(g + 1) * gw]
             * lax.rsqrt(jnp.mean(y[:, g * gw:(g + 1) * gw] * y[:, g * gw:(g + 1) * gw], axis=-1, keepdims=True) + EPS)
             for g in range(SSD_GROUPS)], axis=1)
        return (y * gssd_ref[...]).astype(BF16)

    def branches(r, y):
        ys = jnp.dot(y, wbs_ref[...], preferred_element_type=F32)
        ya = jnp.dot(oa_ref[r, :], wba_ref[...], preferred_element_type=F32)
        return (_sigmoid(ga_ref[r, :].astype(F32)) * ya + _sigmoid(gs_ref[r, :].astype(F32)) * ys).astype(BF16)

    def residual(r, mixed):
        mix = jnp.dot(mixed, wout_ref[...], preferred_element_type=F32)
        x1 = x_ref[r, :] + gt1_ref[0] * _rms(mix, gpost_ref[...])
        x1_ref[r, :] = x1
        h2 = _rms(x1, gpre_ref[...]) * (1.0 + sc2_ref[0]) + sh2_ref[0]
        h2_hi, h2_lo = _split_bf16(h2)
        h2_ref[r, :] = h2_hi
        lg_ref[r, :] = (jnp.dot(h2_hi, wrhi_ref[...], preferred_element_type=F32)
                        + jnp.dot(h2_lo, wrhi_ref[...], preferred_element_type=F32)
                        + jnp.dot(h2_hi, wrlo_ref[...], preferred_element_type=F32))

    rows = [slice(t * sub, (t + 1) * sub) for t in range(nsub)]
    ys, mixed = {}, {}
    for phase in range(nsub + 2):
        if 0 <= phase - 2 < nsub:
            residual(rows[phase - 2], mixed.pop(phase - 2))
        if 0 <= phase - 1 < nsub:
            mixed[phase - 1] = branches(rows[phase - 1], ys.pop(phase - 1))
        if phase < nsub:
            ys[phase] = gated_norm(rows[phase])


def _merge(y_ssd, rest, oattn, x2d, gt1, sh2, sc2, g_ssd_norm, g_post_mix, g_pre_ffn, wba, wbs, wout, w_router,
           rows_per_batch):
    m, d = x2d.shape
    tm = MERGE_TM
    tpb = rows_per_batch // tm
    row = lambda c: pl.BlockSpec((tm, d), lambda i: (i, c))
    per_batch = pl.BlockSpec((1, 1, d), lambda i: (i // tpb, 0, 0))
    vec = pl.BlockSpec((1, d), lambda i: (0, 0))
    mat = pl.BlockSpec((d, d), lambda i: (0, 0))
    wr = jnp.zeros((d, LANES), F32).at[:, :N_EXPERTS].set(w_router)
    wr_hi, wr_lo = _split_bf16(wr)
    wr_spec = pl.BlockSpec((d, LANES), lambda i: (0, 0))
    return pl.pallas_call(
        _merge_kernel,
        out_shape=[jax.ShapeDtypeStruct((m, d), F32), jax.ShapeDtypeStruct((m, d), BF16),
                   jax.ShapeDtypeStruct((m, LANES), F32)],
        grid=(m // tm,),
        in_specs=[row(0), row(2), row(3), row(4), row(0), row(0), per_batch, per_batch, per_batch, vec, vec, vec,
                  mat, mat, mat, wr_spec, wr_spec],
        out_specs=[row(0), row(0), pl.BlockSpec((tm, LANES), lambda i: (i, 0))],
        compiler_params=_cparams("parallel"),
    )(y_ssd, rest, rest, rest, oattn, x2d, gt1, sh2, sc2, g_ssd_norm, g_post_mix, g_pre_ffn, wba, wbs, wout,
      wr_hi, wr_lo)


def _first_index(hit, iota, limit):
    return jnp.min(jnp.where(hit, iota, limit), axis=0, keepdims=True)


def _route_kernel(lg_ref, bias_ref, eidx_ref, wts_ref, rank_ref, cnt_ref, carry):
    i = pl.program_id(0)
    tl = lg_ref.shape[0]

    @pl.when(i == 0)
    def _():
        carry[...] = jnp.zeros_like(carry)

    scores = _sigmoid(lg_ref[...].T[:N_EXPERTS])
    sel = scores + bias_ref[...]
    epg = EXPERTS_PER_GROUP
    io8 = lax.broadcasted_iota(jnp.int32, (epg, tl), 0)
    neg_inf = -jnp.inf

    grp = []
    for g in range(N_EXPERT_GROUPS):
        blk = sel[g * epg:(g + 1) * epg]
        m1 = jnp.max(blk, axis=0, keepdims=True)
        rest = jnp.where(io8 == _first_index(blk == m1, io8, epg), neg_inf, blk)
        grp.append(m1 + jnp.max(rest, axis=0, keepdims=True))
    gsc = jnp.concatenate(grp, axis=0)
    gkeep = jnp.zeros_like(gsc)
    for _ in range(TOPK_GROUPS):
        pick = io8 == _first_index(gsc == jnp.max(gsc, axis=0, keepdims=True), io8, N_EXPERT_GROUPS)
        gkeep = jnp.where(pick, 1.0, gkeep)
        gsc = jnp.where(pick, neg_inf, gsc)
    cand = jnp.concatenate(
        [jnp.where(gkeep[g:g + 1] > 0.5, sel[g * epg:(g + 1) * epg], neg_inf) for g in range(N_EXPERT_GROUPS)], axis=0)

    ioe = lax.broadcasted_iota(jnp.int32, (N_EXPERTS, tl), 0)
    picks, pscore = [], []
    chosen = jnp.zeros_like(cand)
    for _ in range(TOP_K):
        e_k = _first_index(cand == jnp.max(cand, axis=0, keepdims=True), ioe, N_EXPERTS)
        hit = ioe == e_k
        picks.append(e_k)
        pscore.append(jnp.sum(jnp.where(hit, scores, 0.0), axis=0, keepdims=True))
        chosen = jnp.where(hit, 1.0, chosen)
        cand = jnp.where(hit, neg_inf, cand)
    wsum = pscore[0]
    for k in range(1, TOP_K):
        wsum = wsum + pscore[k]

    ti = lax.broadcasted_iota(jnp.int32, (tl, tl), 0)
    tj = lax.broadcasted_iota(jnp.int32, (tl, tl), 1)
    before = (ti < tj).astype(BF16)
    rank = jnp.dot(chosen.astype(BF16), before, preferred_element_type=F32) + carry[...]
    total = carry[...] + jnp.sum(chosen, axis=1, keepdims=True)
    carry[...] = total
    cnt_ref[...] = jnp.broadcast_to(total, cnt_ref.shape)

    eidx_ref[...] = jnp.concatenate(picks, axis=0)
    wts_ref[...] = jnp.concatenate([p / wsum * ROUTED_SCALE for p in pscore], axis=0)
    rank_ref[...] = jnp.concatenate(
        [jnp.sum(jnp.where(ioe == e_k, rank, 0.0), axis=0, keepdims=True) for e_k in picks], axis=0).astype(jnp.int32)


def _route(logits, router_bias):
    m = logits.shape[0]
    ne = N_EXPERTS
    tl = ROUTE_TL
    tok = lambda dt: jax.ShapeDtypeStruct((TOP_K, m), dt)
    tok_spec = pl.BlockSpec((TOP_K, tl), lambda i: (0, i))
    return pl.pallas_call(
        _route_kernel,
        out_shape=[tok(jnp.int32), tok(F32), tok(jnp.int32), jax.ShapeDtypeStruct((ne, LANES), F32)],
        grid=(m // tl,),
        in_specs=[pl.BlockSpec((tl, LANES), lambda i: (i, 0)), pl.BlockSpec((ne, 1), lambda i: (0, 0))],
        out_specs=[tok_spec, tok_spec, tok_spec, pl.BlockSpec((ne, LANES), lambda i: (0, 0))],
        scratch_shapes=[pltpu.VMEM((ne, 1), F32)],
        compiler_params=_cparams("arbitrary"),
    )(logits, router_bias.reshape(ne, 1))


def _expert_kernel(tile_ref, e_ref, lo_ref, hi_ref, x_ref, wg_ref, wu_ref, wd_ref, o_ref, wg_s, wu_s, wd_s):
    w = pl.program_id(0)
    lo, hi = lo_ref[w], hi_ref[w]
    tm = x_ref.shape[0]
    base = tile_ref[w] * tm

    @pl.when((w == 0) | (e_ref[w] != e_ref[jnp.maximum(w - 1, 0)]))
    def _():
        wg_s[...] = wg_ref[0].astype(BF16)
        wu_s[...] = wu_ref[0].astype(BF16)
        wd_s[...] = wd_ref[0].astype(BF16)

    @pl.when(hi > lo)
    def _():
        x = x_ref[...]
        gate = jnp.dot(x, wg_s[...], preferred_element_type=F32)
        up = jnp.dot(x, wu_s[...], preferred_element_type=F32)
        y = jnp.dot((_silu(gate) * up).astype(BF16), wd_s[...], preferred_element_type=F32).astype(o_ref.dtype)
        rid = base + lax.broadcasted_iota(jnp.int32, (tm, 1), 0)

        @pl.when(lo == base)
        def _():
            o_ref[...] = jnp.where(rid < hi, y, jnp.zeros_like(y))

        @pl.when(lo != base)
        def _():
            o_ref[...] = jnp.where((rid >= lo) & (rid < hi), y, o_ref[...])


def _drop_alias_ref(kernel_fn, pos):
    def body(*refs):
        return kernel_fn(*refs[:pos], *refs[pos + 1:])
    return body


def _experts(item_tile, item_e, item_lo, item_hi, xs, w_gate, w_up, w_down, y_all, chunk, n_rows):
    rows_c, d = xs.shape
    tm = EXPERT_TM
    ff = w_gate.shape[-1]
    tile0 = chunk * (rows_c // tm)
    row_map = lambda w, t, e, lo, hi: (t[w], 0)
    in_specs = [pl.BlockSpec((tm, d), row_map),
                pl.BlockSpec((1, d, ff), lambda w, t, e, lo, hi: (e[w], 0, 0)),
                pl.BlockSpec((1, d, ff), lambda w, t, e, lo, hi: (e[w], 0, 0)),
                pl.BlockSpec((1, ff, d), lambda w, t, e, lo, hi: (e[w], 0, 0))]
    args = [item_tile, item_e, item_lo, item_hi, xs, w_gate, w_up, w_down]
    body, aliases = _expert_kernel, {}
    if y_all is not None:
        in_specs.append(pl.BlockSpec(memory_space=pl.ANY))
        aliases = {len(args): 0}
        body = _drop_alias_ref(_expert_kernel, len(args))
        args.append(y_all)
    return pl.pallas_call(
        body,
        out_shape=jax.ShapeDtypeStruct((n_rows, d), BF16),
        grid_spec=pltpu.PrefetchScalarGridSpec(
            num_scalar_prefetch=4,
            grid=(item_tile.shape[0],),
            in_specs=in_specs,
            out_specs=pl.BlockSpec((tm, d), lambda w, t, e, lo, hi: (tile0 + t[w], 0)),
            scratch_shapes=[pltpu.VMEM((d, ff), BF16), pltpu.VMEM((d, ff), BF16), pltpu.VMEM((ff, d), BF16)]),
        input_output_aliases=aliases,
        compiler_params=_cparams("arbitrary"),
    )(*args)


def _final_kernel(x1_ref, h2_ref, yk_ref, wt_ref, gt2_ref, gpost_ref, wg_ref, wu_ref, wd_ref, o_ref):
    h2 = h2_ref[...]
    gate = jnp.dot(h2, wg_ref[...], preferred_element_type=F32)
    up = jnp.dot(h2, wu_ref[...], preferred_element_type=F32)
    f = jnp.dot((_silu(gate) * up).astype(BF16), wd_ref[...], preferred_element_type=F32)
    wt = wt_ref[...]
    for k in range(TOP_K):
        f = f + yk_ref[k].astype(F32) * wt[:, k:k + 1]
    o_ref[...] = x1_ref[...] + gt2_ref[0] * _rms(f, gpost_ref[...])


def _final(x1, h2, y_tok, wts_t, gt2, g_post_ffn, w_sh_gate, w_sh_up, w_sh_down, rows_per_batch, out_all, chunk):
    m, d = x1.shape
    rows_c = y_tok.shape[1]
    tm = min(FINAL_TM, rows_c)
    tpb = rows_per_batch // tm
    off = chunk * (rows_c // tm)
    row = pl.BlockSpec((tm, d), lambda i: (off + i, 0))
    ff = w_sh_gate.shape[1]
    in_specs = [row, row, pl.BlockSpec((TOP_K, tm, d), lambda i: (0, i, 0)),
                pl.BlockSpec((tm, TOP_K), lambda i: (off + i, 0)),
                pl.BlockSpec((1, 1, d), lambda i: ((off + i) // tpb, 0, 0)),
                pl.BlockSpec((1, d), lambda i: (0, 0)),
                pl.BlockSpec((d, ff), lambda i: (0, 0)), pl.BlockSpec((d, ff), lambda i: (0, 0)),
                pl.BlockSpec((ff, d), lambda i: (0, 0))]
    args = [x1, h2, y_tok, wts_t, gt2, g_post_ffn, w_sh_gate, w_sh_up, w_sh_down]
    body, aliases = _final_kernel, {}
    if out_all is not None:
        in_specs.append(pl.BlockSpec(memory_space=pl.ANY))
        aliases = {len(args): 0}
        body = _drop_alias_ref(_final_kernel, len(args))
        args.append(out_all)
    return pl.pallas_call(
        body,
        out_shape=jax.ShapeDtypeStruct((m, d), F32),
        grid=(rows_c // tm,),
        in_specs=in_specs,
        out_specs=row,
        input_output_aliases=aliases,
        compiler_params=_cparams("parallel"),
    )(*args)


def kernel(x, c, ctx, c_ctx, w_ada, b_ada, g_pre_mix, g_post_mix, g_pre_ffn, g_post_ffn, w_in, lam_q1, lam_k1, lam_q2, lam_k2, g_attn_subln, conv_w, conv_b, dt_bias, a_log, d_skip, g_ssd_norm, w_branch_attn, w_branch_ssd, w_out, w_router, router_bias, w_e_gate, w_e_up, w_e_down, w_sh_gate, w_sh_up, w_sh_down):
    nb, ns, d = x.shape
    nl = ctx.shape[1]
    m = nb * ns
    li = 0

    c_all = jnp.zeros((3 * SUBLANES, d), F32).at[:nb].set(c).at[nb].set(c_ctx)
    mod = _modulation(c_all, w_ada[li], b_ada[li])
    sh1, sc1, gt1, sh2, sc2, gt2 = (mod[:nb, k * d:(k + 1) * d].reshape(nb, 1, d) for k in range(6))
    sh1c, sc1c = (mod[nb:nb + 1, k * d:(k + 1) * d].reshape(1, 1, d) for k in range(2))

    qk_dim = N_HEADS * 2 * ATTN_DH
    v_dim = N_HEADS * ATTN_DV
    o_q, o_k, o_v, o_z = 0, qk_dim, 2 * qk_dim, 2 * qk_dim + v_dim
    o_xbc = o_z + D_SSD
    o_dt = o_xbc + XBC_DIM
    o_g = o_dt + 2 * SSD_HEADS
    wi = w_in[li]
    w_qkv = wi[:, o_q:o_z].astype(BF16)
    w_qk_rot = _rotate_half_columns(wi[:, o_q:o_v]).astype(BF16)
    w_kv = wi[:, o_k:o_z].astype(BF16)
    w_xbc = wi[:, o_xbc:o_dt].astype(BF16)
    w_rest = jnp.concatenate([wi[:, o_xbc:o_dt], wi[:, o_z:o_xbc], wi[:, o_g:]], axis=1).astype(BF16)
    w_dt = jnp.zeros((d, LANES), BF16).at[:, :2 * SSD_HEADS].set(wi[:, o_dt:o_g].astype(BF16))

    x2d = x.reshape(m, d)
    c2d = ctx.reshape(nb * nl, d)
    g1 = g_pre_mix[li].reshape(1, d)
    cos, sin = _rope_tables(ns)
    qkv = _project(x2d, sh1, sc1, g1, w_qkv, rows_per_mod=ns, rope=(w_qk_rot, cos, sin, qk_dim))
    rest, dt_lat, dtT_lat = _project(x2d, sh1, sc1, g1, w_rest, rows_per_mod=ns, w_dt=w_dt)
    kv_ctx = _project(c2d, sh1c, sc1c, g1, w_kv, rows_per_mod=nb * nl)
    xbc_ctx, dt_ctx, dtT_ctx = _project(c2d, sh1c, sc1c, g1, w_xbc, rows_per_mod=nb * nl, w_dt=w_dt)

    lam = (jnp.exp(jnp.sum(lam_q1[li] * lam_k1[li])) - jnp.exp(jnp.sum(lam_q2[li] * lam_k2[li])) + LAM_INIT)
    oattn = _diff_attention(lam.reshape(1).astype(F32), qkv, kv_ctx, g_attn_subln[li].reshape(1, ATTN_DV), nb, ns)

    y_ssd = _ssd(rest, 0, dt_lat, dtT_lat, xbc_ctx, dt_ctx, dtT_ctx, conv_w[li], conv_b[li], dt_bias[li], a_log[li],
                 d_skip[li], nb, ns)

    x1, h2, logits = _merge(
        y_ssd, rest, oattn, x2d, gt1, sh2, sc2, g_ssd_norm[li].reshape(1, d), g_post_mix[li].reshape(1, d),
        g_pre_ffn[li].reshape(1, d), w_branch_attn[li].astype(BF16), w_branch_ssd[li].astype(BF16),
        w_out[li].astype(BF16), w_router[li], ns)

    eidx, wts, rank, counts = _route(logits, router_bias[li])

    tm = EXPERT_TM
    n_rows = m * TOP_K
    n_tiles = n_rows // tm
    counts = counts[:, 0].astype(jnp.int32)
    ends = jnp.cumsum(counts)
    start = ends - counts
    eid = jnp.arange(N_EXPERTS, dtype=jnp.int32)
    dest = jnp.sum(jnp.where(eidx[None] == eid[:, None, None], start[:, None, None], 0), axis=0) + rank
    tok = jnp.arange(m, dtype=jnp.int32)
    slot_tok = jnp.sort((eidx * m + tok[None, :]).reshape(-1)) % m

    rows_c = n_rows // MOE_CHUNKS
    tiles_c = rows_c // tm
    y_sorted = None
    for ch in range(MOE_CHUNKS):
        row0 = ch * rows_c
        xs = h2.at[slot_tok[row0:row0 + rows_c]].get(mode="promise_in_bounds")
        cuts = jnp.sort(jnp.concatenate([jnp.arange(tiles_c, dtype=jnp.int32) * tm,
                                         jnp.clip(start[1:] - row0, 0, rows_c)]))
        item_hi = jnp.concatenate([cuts[1:], jnp.full((1,), rows_c, jnp.int32)])
        item_tile = jnp.minimum(cuts // tm, tiles_c - 1)
        item_e = jnp.minimum(jnp.sum((ends[None, :] <= (cuts + row0)[:, None]).astype(jnp.int32), axis=1),
                             N_EXPERTS - 1)
        y_sorted = _experts(item_tile, item_e, cuts, item_hi, xs, w_e_gate[li], w_e_up[li], w_e_down[li],
                            y_sorted, ch, n_rows)

    tok_c = m // COMBINE_CHUNKS
    wts_t = wts.T
    w_sh = (w_sh_gate[li].astype(BF16), w_sh_up[li].astype(BF16), w_sh_down[li].astype(BF16))
    out = None
    for ch in range(COMBINE_CHUNKS):
        dest_c = dest[:, ch * tok_c:(ch + 1) * tok_c].reshape(-1)
        y_tok = y_sorted.at[dest_c].get(mode="promise_in_bounds").reshape(TOP_K, tok_c, d)
        out = _final(x1, h2, y_tok, wts_t, gt2, g_post_ffn[li].reshape(1, d), *w_sh, ns, out, ch)
    return out.reshape(nb, ns, d)
```

```python
import functools
import math

import jax
import jax.numpy as jnp
from jax import lax
from jax.experimental import pallas as pl
from jax.experimental.pallas import tpu as pltpu

F32 = jnp.float32
BF16 = jnp.bfloat16
HIGHEST = lax.Precision.HIGHEST

D_MODEL = 1024
CTX_LEN = 256
GRID_W = 64
EPS = 1e-6

N_HEADS = 8
ATTN_DH = 64
ATTN_DV = 128
ATTN_SCALE = ATTN_DH ** -0.5
LOG2E = math.log2(math.e)
ROPE_THETA = 10000.0
ROPE_AXIS_DIM = ATTN_DH // 2
LAM_INIT = 0.8 - 0.6 * math.exp(-0.3 * 0)

D_SSD = 1024
SSD_HEADDIM = 64
SSD_HEADS = 16
SSD_GROUPS = 4
SSD_HPG = 4
SSD_STATE = 128
XBC_DIM = D_SSD + 2 * SSD_GROUPS * SSD_STATE
GROUP_W = SSD_HPG * SSD_HEADDIM

N_EXPERTS = 64
EXPERT_FF = 256
TOP_K = 8
N_EXPERT_GROUPS = 8
EXPERTS_PER_GROUP = N_EXPERTS // N_EXPERT_GROUPS
TOPK_GROUPS = 4
ROUTED_SCALE = 2.5

LANES = 128
SUBLANES = 8
BF16_SUBLANES = 16
VMEM_LIMIT = 52 * 1024 * 1024

PROJ_TM = 512
PROJ_TN = 512
ATTN_TQ = 2048
ATTN_TH = 256
ATTN_KB = 256
SSD_Q = 256
MERGE_TM = 512
MERGE_SUB = 256
ROUTE_TL = 512
EXPERT_TM = 1024
FINAL_TM = 512
MOE_CHUNKS = 4
COMBINE_CHUNKS = 8

NEG_BIG = -1e30

_NT = (((1,), (1,)), ((), ()))
_TN = (((0,), (0,)), ((), ()))


def _cparams(*sem):
    return pltpu.CompilerParams(dimension_semantics=sem, vmem_limit_bytes=VMEM_LIMIT)


def _sigmoid(v):
    return 0.5 * jnp.tanh(0.5 * v) + 0.5


def _silu(v):
    h = 0.5 * v
    return h + h * jnp.tanh(h)


def _softplus(v):
    return jnp.maximum(v, 0.0) + jnp.log1p(jnp.exp(-jnp.abs(v)))


def _rms(v, g):
    return v * lax.rsqrt(jnp.mean(v * v, axis=-1, keepdims=True) + EPS) * g


def _split_bf16(v):
    hi = v.astype(BF16)
    return hi, (v - hi.astype(F32)).astype(BF16)


def _mod_kernel(c_ref, w_ref, b_ref, o_ref):
    o_ref[...] = jnp.dot(_silu(c_ref[...]), w_ref[...], preferred_element_type=F32,
                         precision=HIGHEST) + b_ref[...]


def _modulation(c_all, w_ada, b_ada):
    rows, d = c_all.shape
    n = w_ada.shape[1]
    tn = 1024
    return pl.pallas_call(
        _mod_kernel,
        out_shape=jax.ShapeDtypeStruct((rows, n), F32),
        grid=(n // tn,),
        in_specs=[pl.BlockSpec((rows, d), lambda j: (0, 0)),
                  pl.BlockSpec((d, tn), lambda j: (0, j)),
                  pl.BlockSpec((1, tn), lambda j: (0, j))],
        out_specs=pl.BlockSpec((rows, tn), lambda j: (0, j)),
        compiler_params=_cparams("arbitrary"),
    )(c_all, w_ada, b_ada.reshape(1, n))


def _proj_kernel(*refs, n_rot, n_q, has_dt):
    it = iter(refs)
    x_ref, sh_ref, sc_ref, g_ref, w_ref = (next(it) for _ in range(5))
    if n_rot:
        wrot_ref, cos_ref, sin_ref = next(it), next(it), next(it)
    if has_dt:
        wdt_ref, wdtT_ref = next(it), next(it)
    o_ref = next(it)
    if has_dt:
        dt_ref, dtT_ref = next(it), next(it)

    h = (_rms(x_ref[...], g_ref[...]) * (1.0 + sc_ref[0]) + sh_ref[0]).astype(BF16)
    if has_dt:
        dt_ref[...] = jnp.dot(h, wdt_ref[...], preferred_element_type=F32)
        dtT_ref[...] = lax.dot_general(wdtT_ref[...], h, _NT, preferred_element_type=F32)
    tn = PROJ_TN
    for c in range(w_ref.shape[1] // tn):
        cs = slice(c * tn, (c + 1) * tn)
        u = jnp.dot(h, w_ref[:, cs], preferred_element_type=F32)
        if c * tn < n_rot:
            ur = jnp.dot(h, wrot_ref[:, cs], preferred_element_type=F32)
            scale = ATTN_SCALE * LOG2E if c * tn < n_q else 1.0
            cos = cos_ref[...] * scale
            sin = sin_ref[...] * scale
            for s in range(tn // LANES):
                sl = slice(s * LANES, (s + 1) * LANES)
                o_ref[:, c * tn + s * LANES:c * tn + (s + 1) * LANES] = (
                    u[:, sl] * cos + ur[:, sl] * sin).astype(o_ref.dtype)
        else:
            o_ref[:, cs] = u.astype(o_ref.dtype)


def _project(x2d, shift, scale, gain, w, *, rows_per_mod, rope=None, w_dt=None):
    m, d = x2d.shape
    n = w.shape[1]
    tm = min(PROJ_TM, rows_per_mod)
    tiles_per_mod = rows_per_mod // tm
    has_dt = w_dt is not None
    const = lambda i: (0, 0)
    in_specs = [pl.BlockSpec((tm, d), lambda i: (i, 0)),
                pl.BlockSpec((1, 1, d), lambda i: (i // tiles_per_mod, 0, 0)),
                pl.BlockSpec((1, 1, d), lambda i: (i // tiles_per_mod, 0, 0)),
                pl.BlockSpec((1, d), const),
                pl.BlockSpec((d, n), const)]
    args = [x2d, shift, scale, gain, w]
    out_shape = [jax.ShapeDtypeStruct((m, n), BF16)]
    out_specs = [pl.BlockSpec((tm, n), lambda i: (i, 0))]
    n_rot = n_q = 0
    if rope is not None:
        w_rot, cos, sin, n_q = rope
        n_rot = w_rot.shape[1]
        pos_tiles = cos.shape[0] // tm
        in_specs += [pl.BlockSpec((d, n_rot), const)] + [pl.BlockSpec((tm, LANES), lambda i: (i % pos_tiles, 0))] * 2
        args += [w_rot, cos, sin]
    if has_dt:
        in_specs += [pl.BlockSpec((d, LANES), const), pl.BlockSpec((LANES, d), const)]
        args += [w_dt, w_dt.T]
        out_shape += [jax.ShapeDtypeStruct((m, LANES), F32), jax.ShapeDtypeStruct((LANES, m), F32)]
        out_specs += [pl.BlockSpec((tm, LANES), lambda i: (i, 0)), pl.BlockSpec((LANES, tm), lambda i: (0, i))]
    res = pl.pallas_call(
        functools.partial(_proj_kernel, n_rot=n_rot, n_q=n_q, has_dt=has_dt),
        out_shape=out_shape,
        grid=(m // tm,),
        in_specs=in_specs,
        out_specs=out_specs,
        compiler_params=_cparams("parallel"),
    )(*args)
    return res if has_dt else res[0]


def _rope_tables(n_tok):
    rows = n_tok // GRID_W
    row = jnp.repeat(jnp.arange(rows, dtype=F32), GRID_W)
    col = jnp.broadcast_to(jnp.arange(GRID_W, dtype=F32)[None, :], (rows, GRID_W)).reshape(-1)
    inv_freq = ROPE_THETA ** (-jnp.arange(0, ROPE_AXIS_DIM, 2, dtype=F32) / ROPE_AXIS_DIM)
    ang = jnp.concatenate([row[:, None] * inv_freq, col[:, None] * inv_freq], axis=-1)
    ang = jnp.concatenate([ang, ang, ang, ang], axis=-1)
    return jnp.cos(ang), jnp.sin(ang)


def _rotate_half_columns(w):
    d, n = w.shape
    half = ATTN_DH // 2
    wc = w.reshape(d, n // ATTN_DH, 2, half)
    return jnp.stack([-wc[:, :, 1], wc[:, :, 0]], axis=2).reshape(d, n)


def _attn_kernel(lam_ref, q_ref, kc_ref, k_ref, vc_ref, v_ref, g_ref, o_ref, s_scr, e_scr):
    th, kb = ATTN_TH, ATTN_KB
    nsub = q_ref.shape[0] // th
    lam = lam_ref[0]
    blocks = ([(kc_ref, vc_ref, o) for o in range(0, kc_ref.shape[0], kb)]
              + [(k_ref, v_ref, o) for o in range(0, k_ref.shape[0], kb)])

    def stacked_q(t):
        q = q_ref[t * th:(t + 1) * th, :]
        lane = lax.broadcasted_iota(jnp.int32, q.shape, 1)
        zero = jnp.zeros_like(q)
        return jnp.concatenate([jnp.where(lane < ATTN_DH, q, zero), jnp.where(lane >= ATTN_DH, q, zero)], axis=0)

    def lane_fold(v, op):
        out = v[:, :LANES]
        for c in range(1, v.shape[1] // LANES):
            out = op(out, v[:, c * LANES:(c + 1) * LANES])
        return out

    def scores(slot, qq, j, mx):
        k_ref_j, _, off = blocks[j]
        s = lax.dot_general(qq, k_ref_j[off:off + kb, :], _NT, preferred_element_type=F32)
        s_scr[slot, :, j * kb:(j + 1) * kb] = s
        part = lane_fold(s, jnp.maximum)
        return part if mx is None else jnp.maximum(mx, part)

    def expo(slot, m, j, sm):
        e = jnp.exp2(s_scr[slot, :, j * kb:(j + 1) * kb] - m)
        e_scr[slot, :, j * kb:(j + 1) * kb] = e
        part = lane_fold(e, jnp.add)
        return part if sm is None else sm + part

    def pv(slot, a0, a1, j, acc):
        _, v_ref_j, off = blocks[j]
        e = e_scr[slot, :, j * kb:(j + 1) * kb]
        w = (e[:th] * a0 - e[th:] * a1).astype(BF16)
        o = jnp.dot(w, v_ref_j[off:off + kb, :], preferred_element_type=F32)
        return o if acc is None else acc + o

    st = [dict() for _ in range(nsub)]
    for phase in range(nsub + 2):
        ts, te, tp = phase, phase - 1, phase - 2
        if 0 <= ts < nsub:
            st[ts].update(qq=stacked_q(ts), mx=None)
        if 0 <= te < nsub:
            st[te].update(m=jnp.max(st[te]["mx"], axis=-1, keepdims=True), sm=None)
        if 0 <= tp < nsub:
            inv = 1.0 / jnp.sum(st[tp]["sm"], axis=-1, keepdims=True)
            st[tp].update(a0=inv[:th], a1=inv[th:] * lam, acc=None)
        for j in range(len(blocks)):
            if 0 <= ts < nsub:
                st[ts]["mx"] = scores(ts % 2, st[ts]["qq"], j, st[ts]["mx"])
            if 0 <= tp < nsub:
                st[tp]["acc"] = pv(tp % 2, st[tp]["a0"], st[tp]["a1"], j, st[tp]["acc"])
            if 0 <= te < nsub:
                st[te]["sm"] = expo(te % 2, st[te]["m"], j, st[te]["sm"])
        if 0 <= tp < nsub:
            o = _rms(st[tp]["acc"], g_ref[...]) * (1.0 - LAM_INIT)
            o_ref[tp * th:(tp + 1) * th, :] = o.astype(o_ref.dtype)


def _diff_attention(lam, qkv, kv_ctx, g_subln, nb, ns):
    tq = ATTN_TQ
    nq = ns // tq
    nl = kv_ctx.shape[0] // nb
    stage_buf = pltpu.VMEM((2, 2 * ATTN_TH, nl + ns), F32)
    return pl.pallas_call(
        _attn_kernel,
        out_shape=jax.ShapeDtypeStruct((nb * ns, N_HEADS * ATTN_DV), BF16),
        grid=(nb, N_HEADS, nq),
        in_specs=[pl.BlockSpec(memory_space=pltpu.SMEM),
                  pl.BlockSpec((tq, LANES), lambda b, h, i: (b * nq + i, h)),
                  pl.BlockSpec((nl, LANES), lambda b, h, i: (b, h)),
                  pl.BlockSpec((ns, LANES), lambda b, h, i: (b, N_HEADS + h)),
                  pl.BlockSpec((nl, LANES), lambda b, h, i: (b, N_HEADS + h)),
                  pl.BlockSpec((ns, LANES), lambda b, h, i: (b, 2 * N_HEADS + h)),
                  pl.BlockSpec((1, LANES), lambda b, h, i: (0, 0))],
        out_specs=pl.BlockSpec((tq, LANES), lambda b, h, i: (b * nq + i, h)),
        scratch_shapes=[stage_buf, stage_buf],
        compiler_params=_cparams("parallel", "parallel", "arbitrary"),
    )(lam, qkv, kv_ctx, qkv, kv_ctx, qkv, g_subln)


def _ssd_conv(x, prev_row, next_row, cw_ref, cb_ref):
    q = x.shape[0]
    ri = lax.broadcasted_iota(jnp.int32, (q, q), 0)
    ci = lax.broadcasted_iota(jnp.int32, (q, q), 1)
    up = jnp.dot((ci == ri - 1).astype(BF16), x, preferred_element_type=F32)
    dn = jnp.dot((ci == ri + 1).astype(BF16), x, preferred_element_type=F32)
    rid = lax.broadcasted_iota(jnp.int32, (SUBLANES, 1), 0)
    up = jnp.concatenate([jnp.where(rid == 0, prev_row, up[:SUBLANES]), up[SUBLANES:]], axis=0)
    dn = jnp.concatenate([dn[:q - SUBLANES], jnp.where(rid == SUBLANES - 1, next_row, dn[q - SUBLANES:])], axis=0)
    return _silu(cb_ref[...] + up * cw_ref[0:1, :] + x.astype(F32) * cw_ref[1:2, :] + dn * cw_ref[2:3, :])


def _split3(v):
    a = v.astype(BF16)
    r = v - a.astype(F32)
    b = r.astype(BF16)
    return a, b, (r - b.astype(F32)).astype(BF16)


def _split_dot(v, e_bf16):
    hi, lo = _split_bf16(v)
    return jnp.dot(jnp.concatenate([hi, lo], axis=1), jnp.concatenate([e_bf16, e_bf16], axis=0),
                   preferred_element_type=F32)


def _ssd_chunk(u, dt_blk, dtT_blk, bias_row, bias_col, alog_row, alog_col, is_f, states, want_y):
    q = u.shape[0]
    nh = SSD_HEADS
    xs = u[:, :D_SSD]
    bm = u[:, D_SSD:D_SSD + SSD_GROUPS * SSD_STATE]
    cm = u[:, D_SSD + SSD_GROUPS * SSD_STATE:]

    dt_all = _softplus(dt_blk + bias_row)
    dtT_all = _softplus(dtT_blk + bias_col)
    dt = jnp.where(is_f, dt_all[:, 0:nh], dt_all[:, nh:2 * nh])
    dtT = jnp.where(is_f, dtT_all[0:nh, :], dtT_all[nh:2 * nh, :])
    a_r = -jnp.exp(jnp.where(is_f, alog_row[0:1, :], alog_row[1:2, :])) * LOG2E
    a_c = -jnp.exp(jnp.where(is_f, alog_col[:, 0:1], alog_col[:, 1:2])) * LOG2E

    ri = lax.broadcasted_iota(jnp.int32, (q, q), 0)
    ci = lax.broadcasted_iota(jnp.int32, (q, q), 1)
    lower = (ri >= ci).astype(BF16)
    upper = (ri <= ci).astype(BF16)
    tmat = jnp.where(is_f, lower, upper)
    tmat_t = jnp.where(is_f, upper, lower)
    causal = tmat > 0.5
    acum = sum(jnp.dot(tmat, p, preferred_element_type=F32) for p in _split3(dt * a_r))
    acum_t = sum(jnp.dot(p, tmat_t, preferred_element_type=F32) for p in _split3(dtT * a_c))
    tot = jnp.where(is_f, acum[q - 1:q, :], acum[0:1, :])
    wgt = jnp.exp2(tot - acum) * dt

    hid = lax.broadcasted_iota(jnp.int32, (nh, D_SSD), 0)
    cid = lax.broadcasted_iota(jnp.int32, (nh, D_SSD), 1)
    expand = (cid // SSD_HEADDIM == hid).astype(BF16)
    pieces = [wgt, jnp.broadcast_to(jnp.exp2(tot), (SUBLANES, nh))]
    if want_y:
        pieces.append(jnp.exp2(acum))
    wide = _split_dot(jnp.concatenate(pieces, axis=0), expand)
    wgt_x = wide[0:q]
    dec_x = wide[q:q + 1]
    if want_y:
        ea_x = wide[q + SUBLANES:2 * q + SUBLANES]
        src_t = acum_t - jnp.log(dtT) * LOG2E

    ys, new_states = [], []
    for g in range(SSD_GROUPS):
        gsl = slice(g * GROUP_W, (g + 1) * GROUP_W)
        nsl = slice(g * SSD_STATE, (g + 1) * SSD_STATE)
        xg = xs[:, gsl]
        bm_g = bm[:, nsl].astype(BF16)
        upd = lax.dot_general(bm_g, (xg * wgt_x[:, gsl]).astype(BF16), _TN, preferred_element_type=F32)
        new_states.append(upd if states is None else states[g] * dec_x[:, gsl] + upd)
        if not want_y:
            continue
        cm_g = cm[:, nsl].astype(BF16)
        cb = lax.dot_general(cm_g, bm_g, _NT, preferred_element_type=F32)
        xb = xg.astype(BF16)
        y_heads = []
        for r in range(SSD_HPG):
            h = g * SSD_HPG + r
            seg = acum[:, h:h + 1] - src_t[h:h + 1, :]
            m_h = (cb * jnp.exp2(jnp.where(causal, seg, NEG_BIG))).astype(BF16)
            y_heads.append(jnp.dot(m_h, xb[:, r * SSD_HEADDIM:(r + 1) * SSD_HEADDIM], preferred_element_type=F32))
        y_g = jnp.concatenate(y_heads, axis=1)
        if states is not None:
            y_g = y_g + jnp.dot(cm_g, states[g].astype(BF16), preferred_element_type=F32) * ea_x[:, gsl]
        ys.append(y_g)
    return (jnp.concatenate(ys, axis=1) if want_y else None), new_states


def _ssd_ctx_kernel(x_ref, dt_ref, dtT_ref, cw_ref, cb_ref, brow_ref, bcol_ref, arow_ref, acol_ref, st_ref):
    is_f = pl.program_id(1) == 0
    zero_row = jnp.zeros((1, x_ref.shape[1]), F32)
    u = _ssd_conv(x_ref[...], zero_row, zero_row, cw_ref, cb_ref)
    _, new_states = _ssd_chunk(u, dt_ref[...], dtT_ref[...], brow_ref[...], bcol_ref[...], arow_ref[...],
                               acol_ref[...], is_f, None, False)
    for g in range(SSD_GROUPS):
        st_ref[0, 0, g] = new_states[g]


def _ssd_lat_kernel(x_ref, prev_ref, next_ref, dt_ref, dtT_ref, h0_ref, cw_ref, cb_ref, brow_ref, bcol_ref,
                    arow_ref, acol_ref, dskip_ref, o_ref, st_scr, *, n_chunks):
    d = pl.program_id(1)
    s = pl.program_id(2)
    is_f = d == 0
    chunk = jnp.where(is_f, s, n_chunks - 1 - s)
    q = x_ref.shape[0]

    @pl.when(s == 0)
    def _():
        st_scr[...] = h0_ref[0, 0]

    halo = prev_ref.shape[0]
    prev_row = jnp.where(chunk == 0, 0.0, prev_ref[halo - 1:halo, :].astype(F32))
    next_row = jnp.where(chunk == n_chunks - 1, 0.0, next_ref[0:1, :].astype(F32))
    u = _ssd_conv(x_ref[...], prev_row, next_row, cw_ref, cb_ref)
    states = [st_scr[g] for g in range(SSD_GROUPS)]
    y, new_states = _ssd_chunk(u, dt_ref[...], dtT_ref[...], brow_ref[...], bcol_ref[...], arow_ref[...],
                               acol_ref[...], is_f, states, True)
    for g in range(SSD_GROUPS):
        st_scr[g] = new_states[g]
    row0 = pl.multiple_of(chunk * q, q)

    @pl.when(is_f)
    def _():
        o_ref[pl.ds(row0, q), :] = y + dskip_ref[...] * u[:, :D_SSD]

    @pl.when(jnp.logical_not(is_f))
    def _():
        o_ref[pl.ds(row0, q), :] += y


def _ssd(xbc_lat_src, lat_col_block, dt_lat, dtT_lat, xbc_ctx, dt_ctx, dtT_ctx, conv_w, conv_b, dt_bias, a_log,
         d_skip, nb, ns):
    q = SSD_Q
    nl = xbc_ctx.shape[0] // nb
    nh = SSD_HEADS
    brow = jnp.zeros((1, LANES), F32).at[0, :2 * nh].set(dt_bias.reshape(-1))
    bcol = brow.reshape(LANES, 1)
    arow = a_log.reshape(2, nh)
    acol = arow.T
    cb2 = conv_b.reshape(1, XBC_DIM)
    const2 = lambda *_: (0, 0)
    param_specs = [pl.BlockSpec((3, XBC_DIM), const2), pl.BlockSpec((1, XBC_DIM), const2),
                   pl.BlockSpec((1, LANES), const2), pl.BlockSpec((LANES, 1), const2),
                   pl.BlockSpec((2, nh), const2), pl.BlockSpec((nh, 2), const2)]
    params = [conv_w, cb2, brow, bcol, arow, acol]
    state_shape = (nb, 2, SSD_GROUPS, SSD_STATE, GROUP_W)

    h0 = pl.pallas_call(
        _ssd_ctx_kernel,
        out_shape=jax.ShapeDtypeStruct(state_shape, F32),
        grid=(nb, 2),
        in_specs=[pl.BlockSpec((nl, XBC_DIM), lambda b, d: (b, 0)),
                  pl.BlockSpec((nl, LANES), lambda b, d: (b, 0)),
                  pl.BlockSpec((LANES, nl), lambda b, d: (0, b))] + param_specs,
        out_specs=pl.BlockSpec((1, 1, SSD_GROUPS, SSD_STATE, GROUP_W), lambda b, d: (b, d, 0, 0, 0)),
        compiler_params=_cparams("parallel", "arbitrary"),
    )(xbc_ctx, dt_ctx, dtT_ctx, *params)

    nc = ns // q
    halo = BF16_SUBLANES
    halo_per_chunk = q // halo
    n_halo = nb * ns // halo

    def chunk_of(d, s):
        return jnp.where(d == 0, s, nc - 1 - s)

    def prev_map(b, d, s):
        return (jnp.maximum((b * nc + chunk_of(d, s)) * halo_per_chunk - 1, 0), lat_col_block)

    def next_map(b, d, s):
        return (jnp.minimum((b * nc + chunk_of(d, s) + 1) * halo_per_chunk, n_halo - 1), lat_col_block)

    return pl.pallas_call(
        functools.partial(_ssd_lat_kernel, n_chunks=nc),
        out_shape=jax.ShapeDtypeStruct((nb * ns, D_SSD), F32),
        grid=(nb, 2, nc),
        in_specs=[pl.BlockSpec((q, XBC_DIM), lambda b, d, s: (b * nc + chunk_of(d, s), lat_col_block)),
                  pl.BlockSpec((halo, XBC_DIM), prev_map),
                  pl.BlockSpec((halo, XBC_DIM), next_map),
                  pl.BlockSpec((q, LANES), lambda b, d, s: (b * nc + chunk_of(d, s), 0)),
                  pl.BlockSpec((LANES, q), lambda b, d, s: (0, b * nc + chunk_of(d, s))),
                  pl.BlockSpec((1, 1, SSD_GROUPS, SSD_STATE, GROUP_W), lambda b, d, s: (b, d, 0, 0, 0))]
        + param_specs + [pl.BlockSpec((1, D_SSD), const2)],
        out_specs=pl.BlockSpec((ns, D_SSD), lambda b, d, s: (b, 0)),
        scratch_shapes=[pltpu.VMEM((SSD_GROUPS, SSD_STATE, GROUP_W), F32)],
        compiler_params=_cparams("parallel", "arbitrary", "arbitrary"),
    )(xbc_lat_src, xbc_lat_src, xbc_lat_src, dt_lat, dtT_lat, h0, *params,
      jnp.repeat(d_skip, SSD_HEADDIM).reshape(1, D_SSD))


def _merge_kernel(y_ref, z_ref, ga_ref, gs_ref, oa_ref, x_ref, gt1_ref, sh2_ref, sc2_ref, gssd_ref, gpost_ref,
                  gpre_ref, wba_ref, wbs_ref, wout_ref, wrhi_ref, wrlo_ref, x1_ref, h2_ref, lg_ref):
    gw = D_SSD // SSD_GROUPS
    sub = MERGE_SUB
    nsub = y_ref.shape[0] // sub

    def gated_norm(r):
        y = y_ref[r, :] * _silu(z_ref[r, :].astype(F32))
        y = jnp.concatenate(
            [y[:, g * gw:(g + 1) * gw]
             * lax.rsqrt(jnp.mean(y[:, g * gw:(g + 1) * gw] * y[:, g * gw:(g + 1) * gw], axis=-1, keepdims=True) + EPS)
             for g in range(SSD_GROUPS)], axis=1)
        return (y * gssd_ref[...]).astype(BF16)

    def branches(r, y):
        ys = jnp.dot(y, wbs_ref[...], preferred_element_type=F32)
        ya = jnp.dot(oa_ref[r, :], wba_ref[...], preferred_element_type=F32)
        return (_sigmoid(ga_ref[r, :].astype(F32)) * ya + _sigmoid(gs_ref[r, :].astype(F32)) * ys).astype(BF16)

    def residual(r, mixed):
        mix = jnp.dot(mixed, wout_ref[...], preferred_element_type=F32)
        x1 = x_ref[r, :] + gt1_ref[0] * _rms(mix, gpost_ref[...])
        x1_ref[r, :] = x1
        h2 = _rms(x1, gpre_ref[...]) * (1.0 + sc2_ref[0]) + sh2_ref[0]
        h2_hi, h2_lo = _split_bf16(h2)
        h2_ref[r, :] = h2_hi
        lg_ref[r, :] = (jnp.dot(h2_hi, wrhi_ref[...], preferred_element_type=F32)
                        + jnp.dot(h2_lo, wrhi_ref[...], preferred_element_type=F32)
                        + jnp.dot(h2_hi, wrlo_ref[...], preferred_element_type=F32))

    rows = [slice(t * sub, (t + 1) * sub) for t in range(nsub)]
    ys, mixed = {}, {}
    for phase in range(nsub + 2):
        if 0 <= phase - 2 < nsub:
            residual(rows[phase - 2], mixed.pop(phase - 2))
        if 0 <= phase - 1 < nsub:
            mixed[phase - 1] = branches(rows[phase - 1], ys.pop(phase - 1))
        if phase < nsub:
            ys[phase] = gated_norm(rows[phase])


def _merge(y_ssd, rest, oattn, x2d, gt1, sh2, sc2, g_ssd_norm, g_post_mix, g_pre_ffn, wba, wbs, wout, w_router,
           rows_per_batch):
    m, d = x2d.shape
    tm = MERGE_TM
    tpb = rows_per_batch // tm
    row = lambda c: pl.BlockSpec((tm, d), lambda i: (i, c))
    per_batch = pl.BlockSpec((1, 1, d), lambda i: (i // tpb, 0, 0))
    vec = pl.BlockSpec((1, d), lambda i: (0, 0))
    mat = pl.BlockSpec((d, d), lambda i: (0, 0))
    wr = jnp.zeros((d, LANES), F32).at[:, :N_EXPERTS].set(w_router)
    wr_hi, wr_lo = _split_bf16(wr)
    wr_spec = pl.BlockSpec((d, LANES), lambda i: (0, 0))
    return pl.pallas_call(
        _merge_kernel,
        out_shape=[jax.ShapeDtypeStruct((m, d), F32), jax.ShapeDtypeStruct((m, d), BF16),
                   jax.ShapeDtypeStruct((m, LANES), F32)],
        grid=(m // tm,),
        in_specs=[row(0), row(2), row(3), row(4), row(0), row(0), per_batch, per_batch, per_batch, vec, vec, vec,
                  mat, mat, mat, wr_spec, wr_spec],
        out_specs=[row(0), row(0), pl.BlockSpec((tm, LANES), lambda i: (i, 0))],
        compiler_params=_cparams("parallel"),
    )(y_ssd, rest, rest, rest, oattn, x2d, gt1, sh2, sc2, g_ssd_norm, g_post_mix, g_pre_ffn, wba, wbs, wout,
      wr_hi, wr_lo)


def _first_index(hit, iota, limit):
    return jnp.min(jnp.where(hit, iota, limit), axis=0, keepdims=True)


def _route_kernel(lg_ref, bias_ref, eidx_ref, wts_ref, rank_ref, cnt_ref, carry):
    i = pl.program_id(0)
    tl = lg_ref.shape[0]

    @pl.when(i == 0)
    def _():
        carry[...] = jnp.zeros_like(carry)

    scores = _sigmoid(lg_ref[...].T[:N_EXPERTS])
    sel = scores + bias_ref[...]
    epg = EXPERTS_PER_GROUP
    io8 = lax.broadcasted_iota(jnp.int32, (epg, tl), 0)
    neg_inf = -jnp.inf

    grp = []
    for g in range(N_EXPERT_GROUPS):
        blk = sel[g * epg:(g + 1) * epg]
        m1 = jnp.max(blk, axis=0, keepdims=True)
        rest = jnp.where(io8 == _first_index(blk == m1, io8, epg), neg_inf, blk)
        grp.append(m1 + jnp.max(rest, axis=0, keepdims=True))
    gsc = jnp.concatenate(grp, axis=0)
    gkeep = jnp.zeros_like(gsc)
    for _ in range(TOPK_GROUPS):
        pick = io8 == _first_index(gsc == jnp.max(gsc, axis=0, keepdims=True), io8, N_EXPERT_GROUPS)
        gkeep = jnp.where(pick, 1.0, gkeep)
        gsc = jnp.where(pick, neg_inf, gsc)
    cand = jnp.concatenate(
        [jnp.where(gkeep[g:g + 1] > 0.5, sel[g * epg:(g + 1) * epg], neg_inf) for g in range(N_EXPERT_GROUPS)], axis=0)

    ioe = lax.broadcasted_iota(jnp.int32, (N_EXPERTS, tl), 0)
    picks, pscore = [], []
    chosen = jnp.zeros_like(cand)
    for _ in range(TOP_K):
        e_k = _first_index(cand == jnp.max(cand, axis=0, keepdims=True), ioe, N_EXPERTS)
        hit = ioe == e_k
        picks.append(e_k)
        pscore.append(jnp.sum(jnp.where(hit, scores, 0.0), axis=0, keepdims=True))
        chosen = jnp.where(hit, 1.0, chosen)
        cand = jnp.where(hit, neg_inf, cand)
    wsum = pscore[0]
    for k in range(1, TOP_K):
        wsum = wsum + pscore[k]

    ti = lax.broadcasted_iota(jnp.int32, (tl, tl), 0)
    tj = lax.broadcasted_iota(jnp.int32, (tl, tl), 1)
    before = (ti < tj).astype(BF16)
    rank = jnp.dot(chosen.astype(BF16), before, preferred_element_type=F32) + carry[...]
    total = carry[...] + jnp.sum(chosen, axis=1, keepdims=True)
    carry[...] = total
    cnt_ref[...] = jnp.broadcast_to(total, cnt_ref.shape)

    eidx_ref[...] = jnp.concatenate(picks, axis=0)
    wts_ref[...] = jnp.concatenate([p / wsum * ROUTED_SCALE for p in pscore], axis=0)
    rank_ref[...] = jnp.concatenate(
        [jnp.sum(jnp.where(ioe == e_k, rank, 0.0), axis=0, keepdims=True) for e_k in picks], axis=0).astype(jnp.int32)


def _route(logits, router_bias):
    m = logits.shape[0]
    ne = N_EXPERTS
    tl = ROUTE_TL
    tok = lambda dt: jax.ShapeDtypeStruct((TOP_K, m), dt)
    tok_spec = pl.BlockSpec((TOP_K, tl), lambda i: (0, i))
    return pl.pallas_call(
        _route_kernel,
        out_shape=[tok(jnp.int32), tok(F32), tok(jnp.int32), jax.ShapeDtypeStruct((ne, LANES), F32)],
        grid=(m // tl,),
        in_specs=[pl.BlockSpec((tl, LANES), lambda i: (i, 0)), pl.BlockSpec((ne, 1), lambda i: (0, 0))],
        out_specs=[tok_spec, tok_spec, tok_spec, pl.BlockSpec((ne, LANES), lambda i: (0, 0))],
        scratch_shapes=[pltpu.VMEM((ne, 1), F32)],
        compiler_params=_cparams("arbitrary"),
    )(logits, router_bias.reshape(ne, 1))


def _expert_kernel(tile_ref, e_ref, lo_ref, hi_ref, x_ref, wg_ref, wu_ref, wd_ref, o_ref, wg_s, wu_s, wd_s):
    w = pl.program_id(0)
    lo, hi = lo_ref[w], hi_ref[w]
    tm = x_ref.shape[0]
    base = tile_ref[w] * tm

    @pl.when((w == 0) | (e_ref[w] != e_ref[jnp.maximum(w - 1, 0)]))
    def _():
        wg_s[...] = wg_ref[0].astype(BF16)
        wu_s[...] = wu_ref[0].astype(BF16)
        wd_s[...] = wd_ref[0].astype(BF16)

    @pl.when(hi > lo)
    def _():
        x = x_ref[...]
        gate = jnp.dot(x, wg_s[...], preferred_element_type=F32)
        up = jnp.dot(x, wu_s[...], preferred_element_type=F32)
        y = jnp.dot((_silu(gate) * up).astype(BF16), wd_s[...], preferred_element_type=F32).astype(o_ref.dtype)
        rid = base + lax.broadcasted_iota(jnp.int32, (tm, 1), 0)

        @pl.when(lo == base)
        def _():
            o_ref[...] = jnp.where(rid < hi, y, jnp.zeros_like(y))

        @pl.when(lo != base)
        def _():
            o_ref[...] = jnp.where((rid >= lo) & (rid < hi), y, o_ref[...])


def _drop_alias_ref(kernel_fn, pos):
    def body(*refs):
        return kernel_fn(*refs[:pos], *refs[pos + 1:])
    return body


def _experts(item_tile, item_e, item_lo, item_hi, xs, w_gate, w_up, w_down, y_all, chunk):
    rows_c, d = xs.shape
    tm = EXPERT_TM
    ff = w_gate.shape[-1]
    tile0 = chunk * (rows_c // tm)
    row_map = lambda w, t, e, lo, hi: (t[w], 0)
    args = [item_tile, item_e, item_lo, item_hi, xs, w_gate, w_up, w_down, y_all]
    carry = len(args) - 1
    return pl.pallas_call(
        _drop_alias_ref(_expert_kernel, carry),
        out_shape=jax.ShapeDtypeStruct(y_all.shape, y_all.dtype),
        grid_spec=pltpu.PrefetchScalarGridSpec(
            num_scalar_prefetch=4,
            grid=(item_tile.shape[0],),
            in_specs=[pl.BlockSpec((tm, d), row_map),
                      pl.BlockSpec((1, d, ff), lambda w, t, e, lo, hi: (e[w], 0, 0)),
                      pl.BlockSpec((1, d, ff), lambda w, t, e, lo, hi: (e[w], 0, 0)),
                      pl.BlockSpec((1, ff, d), lambda w, t, e, lo, hi: (e[w], 0, 0)),
                      pl.BlockSpec(memory_space=pl.ANY)],
            out_specs=pl.BlockSpec((tm, d), lambda w, t, e, lo, hi: (tile0 + t[w], 0)),
            scratch_shapes=[pltpu.VMEM((d, ff), BF16), pltpu.VMEM((d, ff), BF16), pltpu.VMEM((ff, d), BF16)]),
        input_output_aliases={carry: 0},
        compiler_params=_cparams("arbitrary"),
    )(*args)


def _final_kernel(x1_ref, h2_ref, yk_ref, wt_ref, gt2_ref, gpost_ref, wg_ref, wu_ref, wd_ref, o_ref):
    h2 = h2_ref[...]
    gate = jnp.dot(h2, wg_ref[...], preferred_element_type=F32)
    up = jnp.dot(h2, wu_ref[...], preferred_element_type=F32)
    f = jnp.dot((_silu(gate) * up).astype(BF16), wd_ref[...], preferred_element_type=F32)
    wt = wt_ref[...]
    for k in range(TOP_K):
        f = f + yk_ref[k].astype(F32) * wt[:, k:k + 1]
    o_ref[...] = x1_ref[...] + gt2_ref[0] * _rms(f, gpost_ref[...])


def _final(x_carry, h2, y_tok, wts_t, gt2, g_post_ffn, w_sh_gate, w_sh_up, w_sh_down, rows_per_batch, chunk):
    m, d = x_carry.shape
    rows_c = y_tok.shape[1]
    tm = min(FINAL_TM, rows_c)
    tpb = rows_per_batch // tm
    off = chunk * (rows_c // tm)
    row = pl.BlockSpec((tm, d), lambda i: (off + i, 0))
    ff = w_sh_gate.shape[1]
    return pl.pallas_call(
        _final_kernel,
        out_shape=jax.ShapeDtypeStruct((m, d), F32),
        grid=(rows_c // tm,),
        in_specs=[row, row, pl.BlockSpec((TOP_K, tm, d), lambda i: (0, i, 0)),
                  pl.BlockSpec((tm, TOP_K), lambda i: (off + i, 0)),
                  pl.BlockSpec((1, 1, d), lambda i: ((off + i) // tpb, 0, 0)),
                  pl.BlockSpec((1, d), lambda i: (0, 0)),
                  pl.BlockSpec((d, ff), lambda i: (0, 0)), pl.BlockSpec((d, ff), lambda i: (0, 0)),
                  pl.BlockSpec((ff, d), lambda i: (0, 0))],
        out_specs=row,
        input_output_aliases={0: 0},
        compiler_params=_cparams("parallel"),
    )(x_carry, h2, y_tok, wts_t, gt2, g_post_ffn, w_sh_gate, w_sh_up, w_sh_down)


def kernel(x, c, ctx, c_ctx, w_ada, b_ada, g_pre_mix, g_post_mix, g_pre_ffn, g_post_ffn, w_in, lam_q1, lam_k1, lam_q2, lam_k2, g_attn_subln, conv_w, conv_b, dt_bias, a_log, d_skip, g_ssd_norm, w_branch_attn, w_branch_ssd, w_out, w_router, router_bias, w_e_gate, w_e_up, w_e_down, w_sh_gate, w_sh_up, w_sh_down):
    nb, ns, d = x.shape
    nl = ctx.shape[1]
    m = nb * ns
    li = 0

    c_all = jnp.zeros((3 * SUBLANES, d), F32).at[:nb].set(c).at[nb].set(c_ctx)
    mod = _modulation(c_all, w_ada[li], b_ada[li])
    sh1, sc1, gt1, sh2, sc2, gt2 = (mod[:nb, k * d:(k + 1) * d].reshape(nb, 1, d) for k in range(6))
    sh1c, sc1c = (mod[nb:nb + 1, k * d:(k + 1) * d].reshape(1, 1, d) for k in range(2))

    qk_dim = N_HEADS * 2 * ATTN_DH
    v_dim = N_HEADS * ATTN_DV
    o_q, o_k, o_v, o_z = 0, qk_dim, 2 * qk_dim, 2 * qk_dim + v_dim
    o_xbc = o_z + D_SSD
    o_dt = o_xbc + XBC_DIM
    o_g = o_dt + 2 * SSD_HEADS
    wi = w_in[li]
    w_qkv = wi[:, o_q:o_z].astype(BF16)
    w_qk_rot = _rotate_half_columns(wi[:, o_q:o_v]).astype(BF16)
    w_kv = wi[:, o_k:o_z].astype(BF16)
    w_xbc = wi[:, o_xbc:o_dt].astype(BF16)
    w_rest = jnp.concatenate([wi[:, o_xbc:o_dt], wi[:, o_z:o_xbc], wi[:, o_g:]], axis=1).astype(BF16)
    w_dt = jnp.zeros((d, LANES), BF16).at[:, :2 * SSD_HEADS].set(wi[:, o_dt:o_g].astype(BF16))

    x2d = x.reshape(m, d)
    c2d = ctx.reshape(nb * nl, d)
    g1 = g_pre_mix[li].reshape(1, d)
    cos, sin = _rope_tables(ns)
    qkv = _project(x2d, sh1, sc1, g1, w_qkv, rows_per_mod=ns, rope=(w_qk_rot, cos, sin, qk_dim))
    rest, dt_lat, dtT_lat = _project(x2d, sh1, sc1, g1, w_rest, rows_per_mod=ns, w_dt=w_dt)
    kv_ctx = _project(c2d, sh1c, sc1c, g1, w_kv, rows_per_mod=nb * nl)
    xbc_ctx, dt_ctx, dtT_ctx = _project(c2d, sh1c, sc1c, g1, w_xbc, rows_per_mod=nb * nl, w_dt=w_dt)

    lam = (jnp.exp(jnp.sum(lam_q1[li] * lam_k1[li])) - jnp.exp(jnp.sum(lam_q2[li] * lam_k2[li])) + LAM_INIT)
    oattn = _diff_attention(lam.reshape(1).astype(F32), qkv, kv_ctx, g_attn_subln[li].reshape(1, ATTN_DV), nb, ns)

    y_ssd = _ssd(rest, 0, dt_lat, dtT_lat, xbc_ctx, dt_ctx, dtT_ctx, conv_w[li], conv_b[li], dt_bias[li], a_log[li],
                 d_skip[li], nb, ns)

    x1, h2, logits = _merge(
        y_ssd, rest, oattn, x2d, gt1, sh2, sc2, g_ssd_norm[li].reshape(1, d), g_post_mix[li].reshape(1, d),
        g_pre_ffn[li].reshape(1, d), w_branch_attn[li].astype(BF16), w_branch_ssd[li].astype(BF16),
        w_out[li].astype(BF16), w_router[li], ns)

    eidx, wts, rank, counts = _route(logits, router_bias[li])

    tm = EXPERT_TM
    n_rows = m * TOP_K
    n_tiles = n_rows // tm
    counts = counts[:, 0].astype(jnp.int32)
    ends = jnp.cumsum(counts)
    start = ends - counts
    eid = jnp.arange(N_EXPERTS, dtype=jnp.int32)
    dest = jnp.sum(jnp.where(eidx[None] == eid[:, None, None], start[:, None, None], 0), axis=0) + rank
    tok = jnp.arange(m, dtype=jnp.int32)
    slot_tok = jnp.sort((eidx * m + tok[None, :]).reshape(-1)) % m

    rows_c = n_rows // MOE_CHUNKS
    tiles_c = rows_c // tm
    y_sorted = jnp.zeros((n_rows, d), BF16)
    for ch in range(MOE_CHUNKS):
        row0 = ch * rows_c
        xs = h2.at[slot_tok[row0:row0 + rows_c]].get(mode="promise_in_bounds")
        cuts = jnp.sort(jnp.concatenate([jnp.arange(tiles_c, dtype=jnp.int32) * tm,
                                         jnp.clip(start[1:] - row0, 0, rows_c)]))
        item_hi = jnp.concatenate([cuts[1:], jnp.full((1,), rows_c, jnp.int32)])
        item_tile = jnp.minimum(cuts // tm, tiles_c - 1)
        item_e = jnp.minimum(jnp.sum((ends[None, :] <= (cuts + row0)[:, None]).astype(jnp.int32), axis=1),
                             N_EXPERTS - 1)
        y_sorted = _experts(item_tile, item_e, cuts, item_hi, xs, w_e_gate[li], w_e_up[li], w_e_down[li],
                            y_sorted, ch)

    tok_c = m // COMBINE_CHUNKS
    wts_t = wts.T
    w_sh = (w_sh_gate[li].astype(BF16), w_sh_up[li].astype(BF16), w_sh_down[li].astype(BF16))
    out = x1
    for ch in range(COMBINE_CHUNKS):
        dest_c = dest[:, ch * tok_c:(ch + 1) * tok_c].reshape(-1)
        y_tok = y_sorted.at[dest_c].get(mode="promise_in_bounds").reshape(TOP_K, tok_c, d)
        out = _final(out, h2, y_tok, wts_t, gt2, g_post_ffn[li].reshape(1, d), *w_sh, ns, ch)
    return out.reshape(nb, ns, d)
```

```python
import functools
import math

import jax
import jax.numpy as jnp
from jax import lax
from jax.experimental import pallas as pl
from jax.experimental.pallas import tpu as pltpu

F32 = jnp.float32
BF16 = jnp.bfloat16
HIGHEST = lax.Precision.HIGHEST

D_MODEL = 1024
CTX_LEN = 256
GRID_W = 64
EPS = 1e-6

N_HEADS = 8
ATTN_DH = 64
ATTN_DV = 128
ATTN_SCALE = ATTN_DH ** -0.5
LOG2E = math.log2(math.e)
ROPE_THETA = 10000.0
ROPE_AXIS_DIM = ATTN_DH // 2
LAM_INIT = 0.8 - 0.6 * math.exp(-0.3 * 0)

D_SSD = 1024
SSD_HEADDIM = 64
SSD_HEADS = 16
SSD_GROUPS = 4
SSD_HPG = 4
SSD_STATE = 128
XBC_DIM = D_SSD + 2 * SSD_GROUPS * SSD_STATE
GROUP_W = SSD_HPG * SSD_HEADDIM

N_EXPERTS = 64
EXPERT_FF = 256
TOP_K = 8
N_EXPERT_GROUPS = 8
EXPERTS_PER_GROUP = N_EXPERTS // N_EXPERT_GROUPS
TOPK_GROUPS = 4
ROUTED_SCALE = 2.5

LANES = 128
SUBLANES = 8
BF16_SUBLANES = 16
VMEM_LIMIT = 52 * 1024 * 1024

PROJ_TM = 512
PROJ_TN = 512
ATTN_TQ = 2048
ATTN_TH = 256
ATTN_KB = 256
SSD_Q = 256
MERGE_TM = 512
MERGE_SUB = 256
ROUTE_TL = 512
EXPERT_TM = 1024
FINAL_TM = 512
MOE_CHUNKS = 4
COMBINE_CHUNKS = 8

NEG_BIG = -1e30

_NT = (((1,), (1,)), ((), ()))
_TN = (((0,), (0,)), ((), ()))


def _cparams(*sem):
    return pltpu.CompilerParams(dimension_semantics=sem, vmem_limit_bytes=VMEM_LIMIT)


def _sigmoid(v):
    return 0.5 * jnp.tanh(0.5 * v) + 0.5


def _silu(v):
    h = 0.5 * v
    return h + h * jnp.tanh(h)


def _softplus(v):
    return jnp.maximum(v, 0.0) + jnp.log1p(jnp.exp(-jnp.abs(v)))


def _rms(v, g):
    return v * lax.rsqrt(jnp.mean(v * v, axis=-1, keepdims=True) + EPS) * g


def _split_bf16(v):
    hi = v.astype(BF16)
    return hi, (v - hi.astype(F32)).astype(BF16)


def _mod_kernel(c_ref, w_ref, b_ref, o_ref):
    o_ref[...] = jnp.dot(_silu(c_ref[...]), w_ref[...], preferred_element_type=F32,
                         precision=HIGHEST) + b_ref[...]


def _modulation(c_all, w_ada, b_ada):
    rows, d = c_all.shape
    n = w_ada.shape[1]
    tn = 1024
    return pl.pallas_call(
        _mod_kernel,
        out_shape=jax.ShapeDtypeStruct((rows, n), F32),
        grid=(n // tn,),
        in_specs=[pl.BlockSpec((rows, d), lambda j: (0, 0)),
                  pl.BlockSpec((d, tn), lambda j: (0, j)),
                  pl.BlockSpec((1, tn), lambda j: (0, j))],
        out_specs=pl.BlockSpec((rows, tn), lambda j: (0, j)),
        compiler_params=_cparams("arbitrary"),
    )(c_all, w_ada, b_ada.reshape(1, n))


def _proj_kernel(*refs, n_rot, n_q, has_dt):
    it = iter(refs)
    x_ref, sh_ref, sc_ref, g_ref, w_ref = (next(it) for _ in range(5))
    if n_rot:
        wrot_ref, cos_ref, sin_ref = next(it), next(it), next(it)
    if has_dt:
        wdt_ref, wdtT_ref = next(it), next(it)
    o_ref = next(it)
    if has_dt:
        dt_ref, dtT_ref = next(it), next(it)

    h = (_rms(x_ref[...], g_ref[...]) * (1.0 + sc_ref[0]) + sh_ref[0]).astype(BF16)
    if has_dt:
        dt_ref[...] = jnp.dot(h, wdt_ref[...], preferred_element_type=F32)
        dtT_ref[...] = lax.dot_general(wdtT_ref[...], h, _NT, preferred_element_type=F32)
    tn = PROJ_TN
    for c in range(w_ref.shape[1] // tn):
        cs = slice(c * tn, (c + 1) * tn)
        u = jnp.dot(h, w_ref[:, cs], preferred_element_type=F32)
        if c * tn < n_rot:
            ur = jnp.dot(h, wrot_ref[:, cs], preferred_element_type=F32)
            scale = ATTN_SCALE * LOG2E if c * tn < n_q else 1.0
            cos = cos_ref[...] * scale
            sin = sin_ref[...] * scale
            for s in range(tn // LANES):
                sl = slice(s * LANES, (s + 1) * LANES)
                o_ref[:, c * tn + s * LANES:c * tn + (s + 1) * LANES] = (
                    u[:, sl] * cos + ur[:, sl] * sin).astype(o_ref.dtype)
        else:
            o_ref[:, cs] = u.astype(o_ref.dtype)


def _project(x2d, shift, scale, gain, w, *, rows_per_mod, rope=None, w_dt=None):
    m, d = x2d.shape
    n = w.shape[1]
    tm = min(PROJ_TM, rows_per_mod)
    tiles_per_mod = rows_per_mod // tm
    has_dt = w_dt is not None
    const = lambda i: (0, 0)
    in_specs = [pl.BlockSpec((tm, d), lambda i: (i, 0)),
                pl.BlockSpec((1, 1, d), lambda i: (i // tiles_per_mod, 0, 0)),
                pl.BlockSpec((1, 1, d), lambda i: (i // tiles_per_mod, 0, 0)),
                pl.BlockSpec((1, d), const),
                pl.BlockSpec((d, n), const)]
    args = [x2d, shift, scale, gain, w]
    out_shape = [jax.ShapeDtypeStruct((m, n), BF16)]
    out_specs = [pl.BlockSpec((tm, n), lambda i: (i, 0))]
    n_rot = n_q = 0
    if rope is not None:
        w_rot, cos, sin, n_q = rope
        n_rot = w_rot.shape[1]
        pos_tiles = cos.shape[0] // tm
        in_specs += [pl.BlockSpec((d, n_rot), const)] + [pl.BlockSpec((tm, LANES), lambda i: (i % pos_tiles, 0))] * 2
        args += [w_rot, cos, sin]
    if has_dt:
        in_specs += [pl.BlockSpec((d, LANES), const), pl.BlockSpec((LANES, d), const)]
        args += [w_dt, w_dt.T]
        out_shape += [jax.ShapeDtypeStruct((m, LANES), F32), jax.ShapeDtypeStruct((LANES, m), F32)]
        out_specs += [pl.BlockSpec((tm, LANES), lambda i: (i, 0)), pl.BlockSpec((LANES, tm), lambda i: (0, i))]
    res = pl.pallas_call(
        functools.partial(_proj_kernel, n_rot=n_rot, n_q=n_q, has_dt=has_dt),
        out_shape=out_shape,
        grid=(m // tm,),
        in_specs=in_specs,
        out_specs=out_specs,
        compiler_params=_cparams("parallel"),
    )(*args)
    return res if has_dt else res[0]


def _rope_tables(n_tok):
    rows = n_tok // GRID_W
    row = jnp.repeat(jnp.arange(rows, dtype=F32), GRID_W)
    col = jnp.broadcast_to(jnp.arange(GRID_W, dtype=F32)[None, :], (rows, GRID_W)).reshape(-1)
    inv_freq = ROPE_THETA ** (-jnp.arange(0, ROPE_AXIS_DIM, 2, dtype=F32) / ROPE_AXIS_DIM)
    ang = jnp.concatenate([row[:, None] * inv_freq, col[:, None] * inv_freq], axis=-1)
    ang = jnp.concatenate([ang, ang, ang, ang], axis=-1)
    return jnp.cos(ang), jnp.sin(ang)


def _rotate_half_columns(w):
    d, n = w.shape
    half = ATTN_DH // 2
    wc = w.reshape(d, n // ATTN_DH, 2, half)
    return jnp.stack([-wc[:, :, 1], wc[:, :, 0]], axis=2).reshape(d, n)


def _attn_kernel(lam_ref, q_ref, kc_ref, k_ref, vc_ref, v_ref, g_ref, o_ref, s_scr, e_scr):
    th, kb = ATTN_TH, ATTN_KB
    nsub = q_ref.shape[0] // th
    lam = lam_ref[0]
    blocks = ([(kc_ref, vc_ref, o) for o in range(0, kc_ref.shape[0], kb)]
              + [(k_ref, v_ref, o) for o in range(0, k_ref.shape[0], kb)])

    def stacked_q(t):
        q = q_ref[t * th:(t + 1) * th, :]
        lane = lax.broadcasted_iota(jnp.int32, q.shape, 1)
        zero = jnp.zeros_like(q)
        return jnp.concatenate([jnp.where(lane < ATTN_DH, q, zero), jnp.where(lane >= ATTN_DH, q, zero)], axis=0)

    def lane_fold(v, op):
        out = v[:, :LANES]
        for c in range(1, v.shape[1] // LANES):
            out = op(out, v[:, c * LANES:(c + 1) * LANES])
        return out

    def scores(slot, qq, j, mx):
        k_ref_j, _, off = blocks[j]
        s = lax.dot_general(qq, k_ref_j[off:off + kb, :], _NT, preferred_element_type=F32)
        s_scr[slot, :, j * kb:(j + 1) * kb] = s
        part = lane_fold(s, jnp.maximum)
        return part if mx is None else jnp.maximum(mx, part)

    def expo(slot, m, j, sm):
        e = jnp.exp2(s_scr[slot, :, j * kb:(j + 1) * kb] - m)
        e_scr[slot, :, j * kb:(j + 1) * kb] = e
        part = lane_fold(e, jnp.add)
        return part if sm is None else sm + part

    def pv(slot, a0, a1, j, acc):
        _, v_ref_j, off = blocks[j]
        e = e_scr[slot, :, j * kb:(j + 1) * kb]
        w = (e[:th] * a0 - e[th:] * a1).astype(BF16)
        o = jnp.dot(w, v_ref_j[off:off + kb, :], preferred_element_type=F32)
        return o if acc is None else acc + o

    st = [dict() for _ in range(nsub)]
    for phase in range(nsub + 2):
        ts, te, tp = phase, phase - 1, phase - 2
        if 0 <= ts < nsub:
            st[ts].update(qq=stacked_q(ts), mx=None)
        if 0 <= te < nsub:
            st[te].update(m=jnp.max(st[te]["mx"], axis=-1, keepdims=True), sm=None)
        if 0 <= tp < nsub:
            inv = 1.0 / jnp.sum(st[tp]["sm"], axis=-1, keepdims=True)
            st[tp].update(a0=inv[:th], a1=inv[th:] * lam, acc=None)
        for j in range(len(blocks)):
            if 0 <= ts < nsub:
                st[ts]["mx"] = scores(ts % 2, st[ts]["qq"], j, st[ts]["mx"])
            if 0 <= tp < nsub:
                st[tp]["acc"] = pv(tp % 2, st[tp]["a0"], st[tp]["a1"], j, st[tp]["acc"])
            if 0 <= te < nsub:
                st[te]["sm"] = expo(te % 2, st[te]["m"], j, st[te]["sm"])
        if 0 <= tp < nsub:
            o = _rms(st[tp]["acc"], g_ref[...]) * (1.0 - LAM_INIT)
            o_ref[tp * th:(tp + 1) * th, :] = o.astype(o_ref.dtype)


def _diff_attention(lam, qkv, kv_ctx, g_subln, nb, ns):
    tq = ATTN_TQ
    nq = ns // tq
    nl = kv_ctx.shape[0] // nb
    stage_buf = pltpu.VMEM((2, 2 * ATTN_TH, nl + ns), F32)
    return pl.pallas_call(
        _attn_kernel,
        out_shape=jax.ShapeDtypeStruct((nb * ns, N_HEADS * ATTN_DV), BF16),
        grid=(nb, N_HEADS, nq),
        in_specs=[pl.BlockSpec(memory_space=pltpu.SMEM),
                  pl.BlockSpec((tq, LANES), lambda b, h, i: (b * nq + i, h)),
                  pl.BlockSpec((nl, LANES), lambda b, h, i: (b, h)),
                  pl.BlockSpec((ns, LANES), lambda b, h, i: (b, N_HEADS + h)),
                  pl.BlockSpec((nl, LANES), lambda b, h, i: (b, N_HEADS + h)),
                  pl.BlockSpec((ns, LANES), lambda b, h, i: (b, 2 * N_HEADS + h)),
                  pl.BlockSpec((1, LANES), lambda b, h, i: (0, 0))],
        out_specs=pl.BlockSpec((tq, LANES), lambda b, h, i: (b * nq + i, h)),
        scratch_shapes=[stage_buf, stage_buf],
        compiler_params=_cparams("parallel", "parallel", "arbitrary"),
    )(lam, qkv, kv_ctx, qkv, kv_ctx, qkv, g_subln)


def _ssd_conv(x, prev_row, next_row, cw_ref, cb_ref):
    q = x.shape[0]
    ri = lax.broadcasted_iota(jnp.int32, (q, q), 0)
    ci = lax.broadcasted_iota(jnp.int32, (q, q), 1)
    up = jnp.dot((ci == ri - 1).astype(BF16), x, preferred_element_type=F32)
    dn = jnp.dot((ci == ri + 1).astype(BF16), x, preferred_element_type=F32)
    rid = lax.broadcasted_iota(jnp.int32, (SUBLANES, 1), 0)
    up = jnp.concatenate([jnp.where(rid == 0, prev_row, up[:SUBLANES]), up[SUBLANES:]], axis=0)
    dn = jnp.concatenate([dn[:q - SUBLANES], jnp.where(rid == SUBLANES - 1, next_row, dn[q - SUBLANES:])], axis=0)
    return _silu(cb_ref[...] + up * cw_ref[0:1, :] + x.astype(F32) * cw_ref[1:2, :] + dn * cw_ref[2:3, :])


def _split3(v):
    a = v.astype(BF16)
    r = v - a.astype(F32)
    b = r.astype(BF16)
    return a, b, (r - b.astype(F32)).astype(BF16)


def _split_dot(v, e_bf16):
    hi, lo = _split_bf16(v)
    return jnp.dot(jnp.concatenate([hi, lo], axis=1), jnp.concatenate([e_bf16, e_bf16], axis=0),
                   preferred_element_type=F32)


def _ssd_chunk(u, dt_blk, dtT_blk, bias_row, bias_col, alog_row, alog_col, is_f, states, want_y):
    q = u.shape[0]
    nh = SSD_HEADS
    xs = u[:, :D_SSD]
    bm = u[:, D_SSD:D_SSD + SSD_GROUPS * SSD_STATE]
    cm = u[:, D_SSD + SSD_GROUPS * SSD_STATE:]

    dt_all = _softplus(dt_blk + bias_row)
    dtT_all = _softplus(dtT_blk + bias_col)
    dt = jnp.where(is_f, dt_all[:, 0:nh], dt_all[:, nh:2 * nh])
    dtT = jnp.where(is_f, dtT_all[0:nh, :], dtT_all[nh:2 * nh, :])
    a_r = -jnp.exp(jnp.where(is_f, alog_row[0:1, :], alog_row[1:2, :])) * LOG2E
    a_c = -jnp.exp(jnp.where(is_f, alog_col[:, 0:1], alog_col[:, 1:2])) * LOG2E

    ri = lax.broadcasted_iota(jnp.int32, (q, q), 0)
    ci = lax.broadcasted_iota(jnp.int32, (q, q), 1)
    lower = (ri >= ci).astype(BF16)
    upper = (ri <= ci).astype(BF16)
    tmat = jnp.where(is_f, lower, upper)
    tmat_t = jnp.where(is_f, upper, lower)
    causal = tmat > 0.5
    acum = sum(jnp.dot(tmat, p, preferred_element_type=F32) for p in _split3(dt * a_r))
    acum_t = sum(jnp.dot(p, tmat_t, preferred_element_type=F32) for p in _split3(dtT * a_c))
    tot = jnp.where(is_f, acum[q - 1:q, :], acum[0:1, :])
    wgt = jnp.exp2(tot - acum) * dt

    hid = lax.broadcasted_iota(jnp.int32, (nh, D_SSD), 0)
    cid = lax.broadcasted_iota(jnp.int32, (nh, D_SSD), 1)
    expand = (cid // SSD_HEADDIM == hid).astype(BF16)
    pieces = [wgt, jnp.broadcast_to(jnp.exp2(tot), (SUBLANES, nh))]
    if want_y:
        pieces.append(jnp.exp2(acum))
    wide = _split_dot(jnp.concatenate(pieces, axis=0), expand)
    wgt_x = wide[0:q]
    dec_x = wide[q:q + 1]
    if want_y:
        ea_x = wide[q + SUBLANES:2 * q + SUBLANES]
        src_t = acum_t - jnp.log(dtT) * LOG2E

    ys, new_states = [], []
    for g in range(SSD_GROUPS):
        gsl = slice(g * GROUP_W, (g + 1) * GROUP_W)
        nsl = slice(g * SSD_STATE, (g + 1) * SSD_STATE)
        xg = xs[:, gsl]
        bm_g = bm[:, nsl].astype(BF16)
        upd = lax.dot_general(bm_g, (xg * wgt_x[:, gsl]).astype(BF16), _TN, preferred_element_type=F32)
        new_states.append(upd if states is None else states[g] * dec_x[:, gsl] + upd)
        if not want_y:
            continue
        cm_g = cm[:, nsl].astype(BF16)
        cb = lax.dot_general(cm_g, bm_g, _NT, preferred_element_type=F32)
        xb = xg.astype(BF16)
        y_heads = []
        for r in range(SSD_HPG):
            h = g * SSD_HPG + r
            seg = acum[:, h:h + 1] - src_t[h:h + 1, :]
            m_h = (cb * jnp.exp2(jnp.where(causal, seg, NEG_BIG))).astype(BF16)
            y_heads.append(jnp.dot(m_h, xb[:, r * SSD_HEADDIM:(r + 1) * SSD_HEADDIM], preferred_element_type=F32))
        y_g = jnp.concatenate(y_heads, axis=1)
        if states is not None:
            y_g = y_g + jnp.dot(cm_g, states[g].astype(BF16), preferred_element_type=F32) * ea_x[:, gsl]
        ys.append(y_g)
    return (jnp.concatenate(ys, axis=1) if want_y else None), new_states


def _ssd_ctx_kernel(x_ref, dt_ref, dtT_ref, cw_ref, cb_ref, brow_ref, bcol_ref, arow_ref, acol_ref, st_ref):
    is_f = pl.program_id(1) == 0
    zero_row = jnp.zeros((1, x_ref.shape[1]), F32)
    u = _ssd_conv(x_ref[...], zero_row, zero_row, cw_ref, cb_ref)
    _, new_states = _ssd_chunk(u, dt_ref[...], dtT_ref[...], brow_ref[...], bcol_ref[...], arow_ref[...],
                               acol_ref[...], is_f, None, False)
    for g in range(SSD_GROUPS):
        st_ref[0, 0, g] = new_states[g]


def _ssd_lat_kernel(x_ref, prev_ref, next_ref, dt_ref, dtT_ref, h0_ref, cw_ref, cb_ref, brow_ref, bcol_ref,
                    arow_ref, acol_ref, dskip_ref, o_ref, zero_ref, st_scr, *, n_chunks):
    d = pl.program_id(1)
    s = pl.program_id(2)
    is_f = d == 0
    chunk = jnp.where(is_f, s, n_chunks - 1 - s)
    q = x_ref.shape[0]
    zero_ref[...] = jnp.zeros(zero_ref.shape, zero_ref.dtype)

    @pl.when(s == 0)
    def _():
        st_scr[...] = h0_ref[0, 0]

    halo = prev_ref.shape[0]
    prev_row = jnp.where(chunk == 0, 0.0, prev_ref[halo - 1:halo, :].astype(F32))
    next_row = jnp.where(chunk == n_chunks - 1, 0.0, next_ref[0:1, :].astype(F32))
    u = _ssd_conv(x_ref[...], prev_row, next_row, cw_ref, cb_ref)
    states = [st_scr[g] for g in range(SSD_GROUPS)]
    y, new_states = _ssd_chunk(u, dt_ref[...], dtT_ref[...], brow_ref[...], bcol_ref[...], arow_ref[...],
                               acol_ref[...], is_f, states, True)
    for g in range(SSD_GROUPS):
        st_scr[g] = new_states[g]
    row0 = pl.multiple_of(chunk * q, q)

    @pl.when(is_f)
    def _():
        o_ref[pl.ds(row0, q), :] = y + dskip_ref[...] * u[:, :D_SSD]

    @pl.when(jnp.logical_not(is_f))
    def _():
        o_ref[pl.ds(row0, q), :] += y


def _ssd(xbc_lat_src, lat_col_block, dt_lat, dtT_lat, xbc_ctx, dt_ctx, dtT_ctx, conv_w, conv_b, dt_bias, a_log,
         d_skip, nb, ns):
    q = SSD_Q
    nl = xbc_ctx.shape[0] // nb
    nh = SSD_HEADS
    brow = jnp.zeros((1, LANES), F32).at[0, :2 * nh].set(dt_bias.reshape(-1))
    bcol = brow.reshape(LANES, 1)
    arow = a_log.reshape(2, nh)
    acol = arow.T
    cb2 = conv_b.reshape(1, XBC_DIM)
    const2 = lambda *_: (0, 0)
    param_specs = [pl.BlockSpec((3, XBC_DIM), const2), pl.BlockSpec((1, XBC_DIM), const2),
                   pl.BlockSpec((1, LANES), const2), pl.BlockSpec((LANES, 1), const2),
                   pl.BlockSpec((2, nh), const2), pl.BlockSpec((nh, 2), const2)]
    params = [conv_w, cb2, brow, bcol, arow, acol]
    state_shape = (nb, 2, SSD_GROUPS, SSD_STATE, GROUP_W)

    h0 = pl.pallas_call(
        _ssd_ctx_kernel,
        out_shape=jax.ShapeDtypeStruct(state_shape, F32),
        grid=(nb, 2),
        in_specs=[pl.BlockSpec((nl, XBC_DIM), lambda b, d: (b, 0)),
                  pl.BlockSpec((nl, LANES), lambda b, d: (b, 0)),
                  pl.BlockSpec((LANES, nl), lambda b, d: (0, b))] + param_specs,
        out_specs=pl.BlockSpec((1, 1, SSD_GROUPS, SSD_STATE, GROUP_W), lambda b, d: (b, d, 0, 0, 0)),
        compiler_params=_cparams("parallel", "arbitrary"),
    )(xbc_ctx, dt_ctx, dtT_ctx, *params)

    nc = ns // q
    halo = BF16_SUBLANES
    halo_per_chunk = q // halo
    n_halo = nb * ns // halo

    def chunk_of(d, s):
        return jnp.where(d == 0, s, nc - 1 - s)

    def prev_map(b, d, s):
        return (jnp.maximum((b * nc + chunk_of(d, s)) * halo_per_chunk - 1, 0), lat_col_block)

    def next_map(b, d, s):
        return (jnp.minimum((b * nc + chunk_of(d, s) + 1) * halo_per_chunk, n_halo - 1), lat_col_block)

    zrows = ns * TOP_K // (2 * nc)
    return pl.pallas_call(
        functools.partial(_ssd_lat_kernel, n_chunks=nc),
        out_shape=[jax.ShapeDtypeStruct((nb * ns, D_SSD), F32),
                   jax.ShapeDtypeStruct((nb * ns * TOP_K, D_MODEL), BF16)],
        grid=(nb, 2, nc),
        in_specs=[pl.BlockSpec((q, XBC_DIM), lambda b, d, s: (b * nc + chunk_of(d, s), lat_col_block)),
                  pl.BlockSpec((halo, XBC_DIM), prev_map),
                  pl.BlockSpec((halo, XBC_DIM), next_map),
                  pl.BlockSpec((q, LANES), lambda b, d, s: (b * nc + chunk_of(d, s), 0)),
                  pl.BlockSpec((LANES, q), lambda b, d, s: (0, b * nc + chunk_of(d, s))),
                  pl.BlockSpec((1, 1, SSD_GROUPS, SSD_STATE, GROUP_W), lambda b, d, s: (b, d, 0, 0, 0))]
        + param_specs + [pl.BlockSpec((1, D_SSD), const2)],
        out_specs=[pl.BlockSpec((ns, D_SSD), lambda b, d, s: (b, 0)),
                   pl.BlockSpec((zrows, D_MODEL), lambda b, d, s: ((b * 2 + d) * nc + s, 0))],
        scratch_shapes=[pltpu.VMEM((SSD_GROUPS, SSD_STATE, GROUP_W), F32)],
        compiler_params=_cparams("parallel", "arbitrary", "arbitrary"),
    )(xbc_lat_src, xbc_lat_src, xbc_lat_src, dt_lat, dtT_lat, h0, *params,
      jnp.repeat(d_skip, SSD_HEADDIM).reshape(1, D_SSD))


def _merge_kernel(y_ref, z_ref, ga_ref, gs_ref, oa_ref, x_ref, gt1_ref, sh2_ref, sc2_ref, gssd_ref, gpost_ref,
                  gpre_ref, wba_ref, wbs_ref, wout_ref, wrhi_ref, wrlo_ref, x1_ref, h2_ref, lg_ref):
    gw = D_SSD // SSD_GROUPS
    sub = MERGE_SUB
    nsub = y_ref.shape[0] // sub

    def gated_norm(r):
        y = y_ref[r, :] * _silu(z_ref[r, :].astype(F32))
        y = jnp.concatenate(
            [y[:, g * gw:(g + 1) * gw]
             * lax.rsqrt(jnp.mean(y[:, g * gw:(g + 1) * gw] * y[:, g * gw:(g + 1) * gw], axis=-1, keepdims=True) + EPS)
             for g in range(SSD_GROUPS)], axis=1)
        return (y * gssd_ref[...]).astype(BF16)

    def branches(r, y):
        ys = jnp.dot(y, wbs_ref[...], preferred_element_type=F32)
        ya = jnp.dot(oa_ref[r, :], wba_ref[...], preferred_element_type=F32)
        return (_sigmoid(ga_ref[r, :].astype(F32)) * ya + _sigmoid(gs_ref[r, :].astype(F32)) * ys).astype(BF16)

    def residual(r, mixed):
        mix = jnp.dot(mixed, wout_ref[...], preferred_element_type=F32)
        x1 = x_ref[r, :] + gt1_ref[0] * _rms(mix, gpost_ref[...])
        x1_ref[r, :] = x1
        h2 = _rms(x1, gpre_ref[...]) * (1.0 + sc2_ref[0]) + sh2_ref[0]
        h2_hi, h2_lo = _split_bf16(h2)
        h2_ref[r, :] = h2_hi
        lg_ref[r, :] = (jnp.dot(h2_hi, wrhi_ref[...], preferred_element_type=F32)
                        + jnp.dot(h2_lo, wrhi_ref[...], preferred_element_type=F32)
                        + jnp.dot(h2_hi, wrlo_ref[...], preferred_element_type=F32))

    rows = [slice(t * sub, (t + 1) * sub) for t in range(nsub)]
    ys, mixed = {}, {}
    for phase in range(nsub + 2):
        if 0 <= phase - 2 < nsub:
            residual(rows[phase - 2], mixed.pop(phase - 2))
        if 0 <= phase - 1 < nsub:
            mixed[phase - 1] = branches(rows[phase - 1], ys.pop(phase - 1))
        if phase < nsub:
            ys[phase] = gated_norm(rows[phase])


def _merge(y_ssd, rest, oattn, x2d, gt1, sh2, sc2, g_ssd_norm, g_post_mix, g_pre_ffn, wba, wbs, wout, w_router,
           rows_per_batch):
    m, d = x2d.shape
    tm = MERGE_TM
    tpb = rows_per_batch // tm
    row = lambda c: pl.BlockSpec((tm, d), lambda i: (i, c))
    per_batch = pl.BlockSpec((1, 1, d), lambda i: (i // tpb, 0, 0))
    vec = pl.BlockSpec((1, d), lambda i: (0, 0))
    mat = pl.BlockSpec((d, d), lambda i: (0, 0))
    wr = jnp.zeros((d, LANES), F32).at[:, :N_EXPERTS].set(w_router)
    wr_hi, wr_lo = _split_bf16(wr)
    wr_spec = pl.BlockSpec((d, LANES), lambda i: (0, 0))
    return pl.pallas_call(
        _merge_kernel,
        out_shape=[jax.ShapeDtypeStruct((m, d), F32), jax.ShapeDtypeStruct((m, d), BF16),
                   jax.ShapeDtypeStruct((m, LANES), F32)],
        grid=(m // tm,),
        in_specs=[row(0), row(2), row(3), row(4), row(0), row(0), per_batch, per_batch, per_batch, vec, vec, vec,
                  mat, mat, mat, wr_spec, wr_spec],
        out_specs=[row(0), row(0), pl.BlockSpec((tm, LANES), lambda i: (i, 0))],
        compiler_params=_cparams("parallel"),
    )(y_ssd, rest, rest, rest, oattn, x2d, gt1, sh2, sc2, g_ssd_norm, g_post_mix, g_pre_ffn, wba, wbs, wout,
      wr_hi, wr_lo)


def _first_index(hit, iota, limit):
    return jnp.min(jnp.where(hit, iota, limit), axis=0, keepdims=True)


def _route_kernel(lg_ref, bias_ref, eidx_ref, wts_ref, rank_ref, cnt_ref, carry):
    i = pl.program_id(0)
    tl = lg_ref.shape[0]

    @pl.when(i == 0)
    def _():
        carry[...] = jnp.zeros_like(carry)

    scores = _sigmoid(lg_ref[...].T[:N_EXPERTS])
    sel = scores + bias_ref[...]
    epg = EXPERTS_PER_GROUP
    io8 = lax.broadcasted_iota(jnp.int32, (epg, tl), 0)
    neg_inf = -jnp.inf

    grp = []
    for g in range(N_EXPERT_GROUPS):
        blk = sel[g * epg:(g + 1) * epg]
        m1 = jnp.max(blk, axis=0, keepdims=True)
        rest = jnp.where(io8 == _first_index(blk == m1, io8, epg), neg_inf, blk)
        grp.append(m1 + jnp.max(rest, axis=0, keepdims=True))
    gsc = jnp.concatenate(grp, axis=0)
    gkeep = jnp.zeros_like(gsc)
    for _ in range(TOPK_GROUPS):
        pick = io8 == _first_index(gsc == jnp.max(gsc, axis=0, keepdims=True), io8, N_EXPERT_GROUPS)
        gkeep = jnp.where(pick, 1.0, gkeep)
        gsc = jnp.where(pick, neg_inf, gsc)
    cand = jnp.concatenate(
        [jnp.where(gkeep[g:g + 1] > 0.5, sel[g * epg:(g + 1) * epg], neg_inf) for g in range(N_EXPERT_GROUPS)], axis=0)

    ioe = lax.broadcasted_iota(jnp.int32, (N_EXPERTS, tl), 0)
    picks, pscore = [], []
    chosen = jnp.zeros_like(cand)
    for _ in range(TOP_K):
        e_k = _first_index(cand == jnp.max(cand, axis=0, keepdims=True), ioe, N_EXPERTS)
        hit = ioe == e_k
        picks.append(e_k)
        pscore.append(jnp.sum(jnp.where(hit, scores, 0.0), axis=0, keepdims=True))
        chosen = jnp.where(hit, 1.0, chosen)
        cand = jnp.where(hit, neg_inf, cand)
    wsum = pscore[0]
    for k in range(1, TOP_K):
        wsum = wsum + pscore[k]

    ti = lax.broadcasted_iota(jnp.int32, (tl, tl), 0)
    tj = lax.broadcasted_iota(jnp.int32, (tl, tl), 1)
    before = (ti < tj).astype(BF16)
    rank = jnp.dot(chosen.astype(BF16), before, preferred_element_type=F32) + carry[...]
    total = carry[...] + jnp.sum(chosen, axis=1, keepdims=True)
    carry[...] = total
    cnt_ref[...] = jnp.broadcast_to(total, cnt_ref.shape)

    eidx_ref[...] = jnp.concatenate(picks, axis=0)
    wts_ref[...] = jnp.concatenate([p / wsum * ROUTED_SCALE for p in pscore], axis=0)
    rank_ref[...] = jnp.concatenate(
        [jnp.sum(jnp.where(ioe == e_k, rank, 0.0), axis=0, keepdims=True) for e_k in picks], axis=0).astype(jnp.int32)


def _route(logits, router_bias):
    m = logits.shape[0]
    ne = N_EXPERTS
    tl = ROUTE_TL
    tok = lambda dt: jax.ShapeDtypeStruct((TOP_K, m), dt)
    tok_spec = pl.BlockSpec((TOP_K, tl), lambda i: (0, i))
    return pl.pallas_call(
        _route_kernel,
        out_shape=[tok(jnp.int32), tok(F32), tok(jnp.int32), jax.ShapeDtypeStruct((ne, LANES), F32)],
        grid=(m // tl,),
        in_specs=[pl.BlockSpec((tl, LANES), lambda i: (i, 0)), pl.BlockSpec((ne, 1), lambda i: (0, 0))],
        out_specs=[tok_spec, tok_spec, tok_spec, pl.BlockSpec((ne, LANES), lambda i: (0, 0))],
        scratch_shapes=[pltpu.VMEM((ne, 1), F32)],
        compiler_params=_cparams("arbitrary"),
    )(logits, router_bias.reshape(ne, 1))


def _expert_kernel(tile_ref, e_ref, lo_ref, hi_ref, x_ref, wg_ref, wu_ref, wd_ref, o_ref, wg_s, wu_s, wd_s):
    w = pl.program_id(0)
    lo, hi = lo_ref[w], hi_ref[w]
    tm = x_ref.shape[0]
    base = tile_ref[w] * tm

    @pl.when((w == 0) | (e_ref[w] != e_ref[jnp.maximum(w - 1, 0)]))
    def _():
        wg_s[...] = wg_ref[0].astype(BF16)
        wu_s[...] = wu_ref[0].astype(BF16)
        wd_s[...] = wd_ref[0].astype(BF16)

    @pl.when(hi > lo)
    def _():
        x = x_ref[...]
        gate = jnp.dot(x, wg_s[...], preferred_element_type=F32)
        up = jnp.dot(x, wu_s[...], preferred_element_type=F32)
        y = jnp.dot((_silu(gate) * up).astype(BF16), wd_s[...], preferred_element_type=F32).astype(o_ref.dtype)
        rid = base + lax.broadcasted_iota(jnp.int32, (tm, 1), 0)

        @pl.when(lo == base)
        def _():
            o_ref[...] = jnp.where(rid < hi, y, jnp.zeros_like(y))

        @pl.when(lo != base)
        def _():
            o_ref[...] = jnp.where((rid >= lo) & (rid < hi), y, o_ref[...])


def _drop_alias_ref(kernel_fn, pos):
    def body(*refs):
        return kernel_fn(*refs[:pos], *refs[pos + 1:])
    return body


def _experts(item_tile, item_e, item_lo, item_hi, xs, w_gate, w_up, w_down, y_all, chunk):
    rows_c, d = xs.shape
    tm = EXPERT_TM
    ff = w_gate.shape[-1]
    tile0 = chunk * (rows_c // tm)
    row_map = lambda w, t, e, lo, hi: (t[w], 0)
    args = [item_tile, item_e, item_lo, item_hi, xs, w_gate, w_up, w_down, y_all]
    carry = len(args) - 1
    return pl.pallas_call(
        _drop_alias_ref(_expert_kernel, carry),
        out_shape=jax.ShapeDtypeStruct(y_all.shape, y_all.dtype),
        grid_spec=pltpu.PrefetchScalarGridSpec(
            num_scalar_prefetch=4,
            grid=(item_tile.shape[0],),
            in_specs=[pl.BlockSpec((tm, d), row_map),
                      pl.BlockSpec((1, d, ff), lambda w, t, e, lo, hi: (e[w], 0, 0)),
                      pl.BlockSpec((1, d, ff), lambda w, t, e, lo, hi: (e[w], 0, 0)),
                      pl.BlockSpec((1, ff, d), lambda w, t, e, lo, hi: (e[w], 0, 0)),
                      pl.BlockSpec(memory_space=pl.ANY)],
            out_specs=pl.BlockSpec((tm, d), lambda w, t, e, lo, hi: (tile0 + t[w], 0)),
            scratch_shapes=[pltpu.VMEM((d, ff), BF16), pltpu.VMEM((d, ff), BF16), pltpu.VMEM((ff, d), BF16)]),
        input_output_aliases={carry: 0},
        compiler_params=_cparams("arbitrary"),
    )(*args)


def _final_kernel(x1_ref, h2_ref, yk_ref, wt_ref, gt2_ref, gpost_ref, wg_ref, wu_ref, wd_ref, o_ref):
    h2 = h2_ref[...]
    gate = jnp.dot(h2, wg_ref[...], preferred_element_type=F32)
    up = jnp.dot(h2, wu_ref[...], preferred_element_type=F32)
    f = jnp.dot((_silu(gate) * up).astype(BF16), wd_ref[...], preferred_element_type=F32)
    wt = wt_ref[...]
    for k in range(TOP_K):
        f = f + yk_ref[k].astype(F32) * wt[:, k:k + 1]
    o_ref[...] = x1_ref[...] + gt2_ref[0] * _rms(f, gpost_ref[...])


def _final(x_carry, h2, y_tok, wts_t, gt2, g_post_ffn, w_sh_gate, w_sh_up, w_sh_down, rows_per_batch, chunk):
    m, d = x_carry.shape
    rows_c = y_tok.shape[1]
    tm = min(FINAL_TM, rows_c)
    tpb = rows_per_batch // tm
    off = chunk * (rows_c // tm)
    row = pl.BlockSpec((tm, d), lambda i: (off + i, 0))
    ff = w_sh_gate.shape[1]
    return pl.pallas_call(
        _final_kernel,
        out_shape=jax.ShapeDtypeStruct((m, d), F32),
        grid=(rows_c // tm,),
        in_specs=[row, row, pl.BlockSpec((TOP_K, tm, d), lambda i: (0, i, 0)),
                  pl.BlockSpec((tm, TOP_K), lambda i: (off + i, 0)),
                  pl.BlockSpec((1, 1, d), lambda i: ((off + i) // tpb, 0, 0)),
                  pl.BlockSpec((1, d), lambda i: (0, 0)),
                  pl.BlockSpec((d, ff), lambda i: (0, 0)), pl.BlockSpec((d, ff), lambda i: (0, 0)),
                  pl.BlockSpec((ff, d), lambda i: (0, 0))],
        out_specs=row,
        input_output_aliases={0: 0},
        compiler_params=_cparams("parallel"),
    )(x_carry, h2, y_tok, wts_t, gt2, g_post_ffn, w_sh_gate, w_sh_up, w_sh_down)


def kernel(x, c, ctx, c_ctx, w_ada, b_ada, g_pre_mix, g_post_mix, g_pre_ffn, g_post_ffn, w_in, lam_q1, lam_k1, lam_q2, lam_k2, g_attn_subln, conv_w, conv_b, dt_bias, a_log, d_skip, g_ssd_norm, w_branch_attn, w_branch_ssd, w_out, w_router, router_bias, w_e_gate, w_e_up, w_e_down, w_sh_gate, w_sh_up, w_sh_down):
    nb, ns, d = x.shape
    nl = ctx.shape[1]
    m = nb * ns
    li = 0

    c_all = jnp.zeros((3 * SUBLANES, d), F32).at[:nb].set(c).at[nb].set(c_ctx)
    mod = _modulation(c_all, w_ada[li], b_ada[li])
    sh1, sc1, gt1, sh2, sc2, gt2 = (mod[:nb, k * d:(k + 1) * d].reshape(nb, 1, d) for k in range(6))
    sh1c, sc1c = (mod[nb:nb + 1, k * d:(k + 1) * d].reshape(1, 1, d) for k in range(2))

    qk_dim = N_HEADS * 2 * ATTN_DH
    v_dim = N_HEADS * ATTN_DV
    o_q, o_k, o_v, o_z = 0, qk_dim, 2 * qk_dim, 2 * qk_dim + v_dim
    o_xbc = o_z + D_SSD
    o_dt = o_xbc + XBC_DIM
    o_g = o_dt + 2 * SSD_HEADS
    wi = w_in[li]
    w_qkv = wi[:, o_q:o_z].astype(BF16)
    w_qk_rot = _rotate_half_columns(wi[:, o_q:o_v]).astype(BF16)
    w_kv = wi[:, o_k:o_z].astype(BF16)
    w_xbc = wi[:, o_xbc:o_dt].astype(BF16)
    w_rest = jnp.concatenate([wi[:, o_xbc:o_dt], wi[:, o_z:o_xbc], wi[:, o_g:]], axis=1).astype(BF16)
    w_dt = jnp.zeros((d, LANES), BF16).at[:, :2 * SSD_HEADS].set(wi[:, o_dt:o_g].astype(BF16))

    x2d = x.reshape(m, d)
    c2d = ctx.reshape(nb * nl, d)
    g1 = g_pre_mix[li].reshape(1, d)
    cos, sin = _rope_tables(ns)
    qkv = _project(x2d, sh1, sc1, g1, w_qkv, rows_per_mod=ns, rope=(w_qk_rot, cos, sin, qk_dim))
    rest, dt_lat, dtT_lat = _project(x2d, sh1, sc1, g1, w_rest, rows_per_mod=ns, w_dt=w_dt)
    kv_ctx = _project(c2d, sh1c, sc1c, g1, w_kv, rows_per_mod=nb * nl)
    xbc_ctx, dt_ctx, dtT_ctx = _project(c2d, sh1c, sc1c, g1, w_xbc, rows_per_mod=nb * nl, w_dt=w_dt)

    lam = (jnp.exp(jnp.sum(lam_q1[li] * lam_k1[li])) - jnp.exp(jnp.sum(lam_q2[li] * lam_k2[li])) + LAM_INIT)
    oattn = _diff_attention(lam.reshape(1).astype(F32), qkv, kv_ctx, g_attn_subln[li].reshape(1, ATTN_DV), nb, ns)

    y_ssd, y_sorted = _ssd(rest, 0, dt_lat, dtT_lat, xbc_ctx, dt_ctx, dtT_ctx, conv_w[li], conv_b[li], dt_bias[li],
                           a_log[li], d_skip[li], nb, ns)

    x1, h2, logits = _merge(
        y_ssd, rest, oattn, x2d, gt1, sh2, sc2, g_ssd_norm[li].reshape(1, d), g_post_mix[li].reshape(1, d),
        g_pre_ffn[li].reshape(1, d), w_branch_attn[li].astype(BF16), w_branch_ssd[li].astype(BF16),
        w_out[li].astype(BF16), w_router[li], ns)

    eidx, wts, rank, counts = _route(logits, router_bias[li])

    tm = EXPERT_TM
    n_rows = m * TOP_K
    n_tiles = n_rows // tm
    counts = counts[:, 0].astype(jnp.int32)
    ends = jnp.cumsum(counts)
    start = ends - counts
    eid = jnp.arange(N_EXPERTS, dtype=jnp.int32)
    dest = jnp.sum(jnp.where(eidx[None] == eid[:, None, None], start[:, None, None], 0), axis=0) + rank
    tok = jnp.arange(m, dtype=jnp.int32)
    slot_tok = jnp.sort((eidx * m + tok[None, :]).reshape(-1)) % m

    rows_c = n_rows // MOE_CHUNKS
    tiles_c = rows_c // tm
    for ch in range(MOE_CHUNKS):
        row0 = ch * rows_c
        xs = h2.at[slot_tok[row0:row0 + rows_c]].get(mode="promise_in_bounds")
        cuts = jnp.sort(jnp.concatenate([jnp.arange(tiles_c, dtype=jnp.int32) * tm,
                                         jnp.clip(start[1:] - row0, 0, rows_c)]))
        item_hi = jnp.concatenate([cuts[1:], jnp.full((1,), rows_c, jnp.int32)])
        item_tile = jnp.minimum(cuts // tm, tiles_c - 1)
        item_e = jnp.minimum(jnp.sum((ends[None, :] <= (cuts + row0)[:, None]).astype(jnp.int32), axis=1),
                             N_EXPERTS - 1)
        y_sorted = _experts(item_tile, item_e, cuts, item_hi, xs, w_e_gate[li], w_e_up[li], w_e_down[li],
                            y_sorted, ch)

    tok_c = m // COMBINE_CHUNKS
    wts_t = wts.T
    w_sh = (w_sh_gate[li].astype(BF16), w_sh_up[li].astype(BF16), w_sh_down[li].astype(BF16))
    out = x1
    for ch in range(COMBINE_CHUNKS):
        dest_c = dest[:, ch * tok_c:(ch + 1) * tok_c].reshape(-1)
        y_tok = y_sorted.at[dest_c].get(mode="promise_in_bounds").reshape(TOP_K, tok_c, d)
        out = _final(out, h2, y_tok, wts_t, gt2, g_post_ffn[li].reshape(1, d), *w_sh, ns, ch)
    return out.reshape(nb, ns, d)
```

```python
import functools
import math

import jax
import jax.numpy as jnp
from jax import lax
from jax.experimental import pallas as pl
from jax.experimental.pallas import tpu as pltpu

F32 = jnp.float32
BF16 = jnp.bfloat16
HIGHEST = lax.Precision.HIGHEST

D_MODEL = 1024
CTX_LEN = 256
GRID_W = 64
EPS = 1e-6

N_HEADS = 8
ATTN_DH = 64
ATTN_DV = 128
ATTN_SCALE = ATTN_DH ** -0.5
LOG2E = math.log2(math.e)
ROPE_THETA = 10000.0
ROPE_AXIS_DIM = ATTN_DH // 2
LAM_INIT = 0.8 - 0.6 * math.exp(-0.3 * 0)

D_SSD = 1024
SSD_HEADDIM = 64
SSD_HEADS = 16
SSD_GROUPS = 4
SSD_HPG = 4
SSD_STATE = 128
XBC_DIM = D_SSD + 2 * SSD_GROUPS * SSD_STATE
GROUP_W = SSD_HPG * SSD_HEADDIM

N_EXPERTS = 64
EXPERT_FF = 256
TOP_K = 8
N_EXPERT_GROUPS = 8
EXPERTS_PER_GROUP = N_EXPERTS // N_EXPERT_GROUPS
TOPK_GROUPS = 4
ROUTED_SCALE = 2.5

LANES = 128
SUBLANES = 8
BF16_SUBLANES = 16
VMEM_LIMIT = 52 * 1024 * 1024

PROJ_TM = 512
PROJ_TN = 512
ATTN_TQ = 2048
ATTN_TH = 256
ATTN_KB = 256
SSD_Q = 256
MERGE_TM = 512
MERGE_SUB = 256
ROUTE_TL = 512
EXPERT_TM = 1024
FINAL_TM = 512
MOE_CHUNKS = 4
COMBINE_CHUNKS = 16

NEG_BIG = -1e30

_NT = (((1,), (1,)), ((), ()))
_TN = (((0,), (0,)), ((), ()))


def _cparams(*sem):
    return pltpu.CompilerParams(dimension_semantics=sem, vmem_limit_bytes=VMEM_LIMIT)


def _sigmoid(v):
    return 0.5 * jnp.tanh(0.5 * v) + 0.5


def _silu(v):
    h = 0.5 * v
    return h + h * jnp.tanh(h)


def _softplus(v):
    return jnp.maximum(v, 0.0) + jnp.log1p(jnp.exp(-jnp.abs(v)))


def _rms(v, g):
    return v * lax.rsqrt(jnp.mean(v * v, axis=-1, keepdims=True) + EPS) * g


def _split_bf16(v):
    hi = v.astype(BF16)
    return hi, (v - hi.astype(F32)).astype(BF16)


def _mod_kernel(c_ref, w_ref, b_ref, o_ref):
    o_ref[...] = jnp.dot(_silu(c_ref[...]), w_ref[...], preferred_element_type=F32,
                         precision=HIGHEST) + b_ref[...]


def _modulation(c_all, w_ada, b_ada):
    rows, d = c_all.shape
    n = w_ada.shape[1]
    tn = 1024
    return pl.pallas_call(
        _mod_kernel,
        out_shape=jax.ShapeDtypeStruct((rows, n), F32),
        grid=(n // tn,),
        in_specs=[pl.BlockSpec((rows, d), lambda j: (0, 0)),
                  pl.BlockSpec((d, tn), lambda j: (0, j)),
                  pl.BlockSpec((1, tn), lambda j: (0, j))],
        out_specs=pl.BlockSpec((rows, tn), lambda j: (0, j)),
        compiler_params=_cparams("arbitrary"),
    )(c_all, w_ada, b_ada.reshape(1, n))


def _proj_kernel(*refs, n_rot, n_q, has_dt):
    it = iter(refs)
    x_ref, sh_ref, sc_ref, g_ref, w_ref = (next(it) for _ in range(5))
    if n_rot:
        wrot_ref, cos_ref, sin_ref = next(it), next(it), next(it)
    if has_dt:
        wdt_ref, wdtT_ref = next(it), next(it)
    o_ref = next(it)
    if has_dt:
        dt_ref, dtT_ref = next(it), next(it)

    h = (_rms(x_ref[...], g_ref[...]) * (1.0 + sc_ref[0]) + sh_ref[0]).astype(BF16)
    if has_dt:
        dt_ref[...] = jnp.dot(h, wdt_ref[...], preferred_element_type=F32)
        dtT_ref[...] = lax.dot_general(wdtT_ref[...], h, _NT, preferred_element_type=F32)
    tn = PROJ_TN
    for c in range(w_ref.shape[1] // tn):
        cs = slice(c * tn, (c + 1) * tn)
        u = jnp.dot(h, w_ref[:, cs], preferred_element_type=F32)
        if c * tn < n_rot:
            ur = jnp.dot(h, wrot_ref[:, cs], preferred_element_type=F32)
            scale = ATTN_SCALE * LOG2E if c * tn < n_q else 1.0
            cos = cos_ref[...] * scale
            sin = sin_ref[...] * scale
            for s in range(tn // LANES):
                sl = slice(s * LANES, (s + 1) * LANES)
                o_ref[:, c * tn + s * LANES:c * tn + (s + 1) * LANES] = (
                    u[:, sl] * cos + ur[:, sl] * sin).astype(o_ref.dtype)
        else:
            o_ref[:, cs] = u.astype(o_ref.dtype)


def _project(x2d, shift, scale, gain, w, *, rows_per_mod, rope=None, w_dt=None):
    m, d = x2d.shape
    n = w.shape[1]
    tm = min(PROJ_TM, rows_per_mod)
    tiles_per_mod = rows_per_mod // tm
    has_dt = w_dt is not None
    const = lambda i: (0, 0)
    in_specs = [pl.BlockSpec((tm, d), lambda i: (i, 0)),
                pl.BlockSpec((1, 1, d), lambda i: (i // tiles_per_mod, 0, 0)),
                pl.BlockSpec((1, 1, d), lambda i: (i // tiles_per_mod, 0, 0)),
                pl.BlockSpec((1, d), const),
                pl.BlockSpec((d, n), const)]
    args = [x2d, shift, scale, gain, w]
    out_shape = [jax.ShapeDtypeStruct((m, n), BF16)]
    out_specs = [pl.BlockSpec((tm, n), lambda i: (i, 0))]
    n_rot = n_q = 0
    if rope is not None:
        w_rot, cos, sin, n_q = rope
        n_rot = w_rot.shape[1]
        pos_tiles = cos.shape[0] // tm
        in_specs += [pl.BlockSpec((d, n_rot), const)] + [pl.BlockSpec((tm, LANES), lambda i: (i % pos_tiles, 0))] * 2
        args += [w_rot, cos, sin]
    if has_dt:
        in_specs += [pl.BlockSpec((d, LANES), const), pl.BlockSpec((LANES, d), const)]
        args += [w_dt, w_dt.T]
        out_shape += [jax.ShapeDtypeStruct((m, LANES), F32), jax.ShapeDtypeStruct((LANES, m), F32)]
        out_specs += [pl.BlockSpec((tm, LANES), lambda i: (i, 0)), pl.BlockSpec((LANES, tm), lambda i: (0, i))]
    res = pl.pallas_call(
        functools.partial(_proj_kernel, n_rot=n_rot, n_q=n_q, has_dt=has_dt),
        out_shape=out_shape,
        grid=(m // tm,),
        in_specs=in_specs,
        out_specs=out_specs,
        compiler_params=_cparams("parallel"),
    )(*args)
    return res if has_dt else res[0]


def _rope_tables(n_tok):
    rows = n_tok // GRID_W
    row = jnp.repeat(jnp.arange(rows, dtype=F32), GRID_W)
    col = jnp.broadcast_to(jnp.arange(GRID_W, dtype=F32)[None, :], (rows, GRID_W)).reshape(-1)
    inv_freq = ROPE_THETA ** (-jnp.arange(0, ROPE_AXIS_DIM, 2, dtype=F32) / ROPE_AXIS_DIM)
    ang = jnp.concatenate([row[:, None] * inv_freq, col[:, None] * inv_freq], axis=-1)
    ang = jnp.concatenate([ang, ang, ang, ang], axis=-1)
    return jnp.cos(ang), jnp.sin(ang)


def _rotate_half_columns(w):
    d, n = w.shape
    half = ATTN_DH // 2
    wc = w.reshape(d, n // ATTN_DH, 2, half)
    return jnp.stack([-wc[:, :, 1], wc[:, :, 0]], axis=2).reshape(d, n)


def _attn_kernel(lam_ref, q_ref, kc_ref, k_ref, vc_ref, v_ref, g_ref, o_ref, s_scr, e_scr):
    th, kb = ATTN_TH, ATTN_KB
    nsub = q_ref.shape[0] // th
    lam = lam_ref[0]
    blocks = ([(kc_ref, vc_ref, o) for o in range(0, kc_ref.shape[0], kb)]
              + [(k_ref, v_ref, o) for o in range(0, k_ref.shape[0], kb)])

    def stacked_q(t):
        q = q_ref[t * th:(t + 1) * th, :]
        lane = lax.broadcasted_iota(jnp.int32, q.shape, 1)
        zero = jnp.zeros_like(q)
        return jnp.concatenate([jnp.where(lane < ATTN_DH, q, zero), jnp.where(lane >= ATTN_DH, q, zero)], axis=0)

    def lane_fold(v, op):
        out = v[:, :LANES]
        for c in range(1, v.shape[1] // LANES):
            out = op(out, v[:, c * LANES:(c + 1) * LANES])
        return out

    def scores(slot, qq, j, mx):
        k_ref_j, _, off = blocks[j]
        s = lax.dot_general(qq, k_ref_j[off:off + kb, :], _NT, preferred_element_type=F32)
        s_scr[slot, :, j * kb:(j + 1) * kb] = s
        part = lane_fold(s, jnp.maximum)
        return part if mx is None else jnp.maximum(mx, part)

    def expo(slot, m, j, sm):
        e = jnp.exp2(s_scr[slot, :, j * kb:(j + 1) * kb] - m)
        e_scr[slot, :, j * kb:(j + 1) * kb] = e
        part = lane_fold(e, jnp.add)
        return part if sm is None else sm + part

    def pv(slot, a0, a1, j, acc):
        _, v_ref_j, off = blocks[j]
        e = e_scr[slot, :, j * kb:(j + 1) * kb]
        w = (e[:th] * a0 - e[th:] * a1).astype(BF16)
        o = jnp.dot(w, v_ref_j[off:off + kb, :], preferred_element_type=F32)
        return o if acc is None else acc + o

    st = [dict() for _ in range(nsub)]
    for phase in range(nsub + 2):
        ts, te, tp = phase, phase - 1, phase - 2
        if 0 <= ts < nsub:
            st[ts].update(qq=stacked_q(ts), mx=None)
        if 0 <= te < nsub:
            st[te].update(m=jnp.max(st[te]["mx"], axis=-1, keepdims=True), sm=None)
        if 0 <= tp < nsub:
            inv = 1.0 / jnp.sum(st[tp]["sm"], axis=-1, keepdims=True)
            st[tp].update(a0=inv[:th], a1=inv[th:] * lam, acc=None)
        for j in range(len(blocks)):
            if 0 <= ts < nsub:
                st[ts]["mx"] = scores(ts % 2, st[ts]["qq"], j, st[ts]["mx"])
            if 0 <= tp < nsub:
                st[tp]["acc"] = pv(tp % 2, st[tp]["a0"], st[tp]["a1"], j, st[tp]["acc"])
            if 0 <= te < nsub:
                st[te]["sm"] = expo(te % 2, st[te]["m"], j, st[te]["sm"])
        if 0 <= tp < nsub:
            o = _rms(st[tp]["acc"], g_ref[...]) * (1.0 - LAM_INIT)
            o_ref[tp * th:(tp + 1) * th, :] = o.astype(o_ref.dtype)


def _diff_attention(lam, qkv, kv_ctx, g_subln, nb, ns):
    tq = ATTN_TQ
    nq = ns // tq
    nl = kv_ctx.shape[0] // nb
    stage_buf = pltpu.VMEM((2, 2 * ATTN_TH, nl + ns), F32)
    return pl.pallas_call(
        _attn_kernel,
        out_shape=jax.ShapeDtypeStruct((nb * ns, N_HEADS * ATTN_DV), BF16),
        grid=(nb, N_HEADS, nq),
        in_specs=[pl.BlockSpec(memory_space=pltpu.SMEM),
                  pl.BlockSpec((tq, LANES), lambda b, h, i: (b * nq + i, h)),
                  pl.BlockSpec((nl, LANES), lambda b, h, i: (b, h)),
                  pl.BlockSpec((ns, LANES), lambda b, h, i: (b, N_HEADS + h)),
                  pl.BlockSpec((nl, LANES), lambda b, h, i: (b, N_HEADS + h)),
                  pl.BlockSpec((ns, LANES), lambda b, h, i: (b, 2 * N_HEADS + h)),
                  pl.BlockSpec((1, LANES), lambda b, h, i: (0, 0))],
        out_specs=pl.BlockSpec((tq, LANES), lambda b, h, i: (b * nq + i, h)),
        scratch_shapes=[stage_buf, stage_buf],
        compiler_params=_cparams("parallel", "parallel", "arbitrary"),
    )(lam, qkv, kv_ctx, qkv, kv_ctx, qkv, g_subln)


def _ssd_conv(x, prev_row, next_row, cw_ref, cb_ref):
    q = x.shape[0]
    ri = lax.broadcasted_iota(jnp.int32, (q, q), 0)
    ci = lax.broadcasted_iota(jnp.int32, (q, q), 1)
    up = jnp.dot((ci == ri - 1).astype(BF16), x, preferred_element_type=F32)
    dn = jnp.dot((ci == ri + 1).astype(BF16), x, preferred_element_type=F32)
    rid = lax.broadcasted_iota(jnp.int32, (SUBLANES, 1), 0)
    up = jnp.concatenate([jnp.where(rid == 0, prev_row, up[:SUBLANES]), up[SUBLANES:]], axis=0)
    dn = jnp.concatenate([dn[:q - SUBLANES], jnp.where(rid == SUBLANES - 1, next_row, dn[q - SUBLANES:])], axis=0)
    return _silu(cb_ref[...] + up * cw_ref[0:1, :] + x.astype(F32) * cw_ref[1:2, :] + dn * cw_ref[2:3, :])


def _split3(v):
    a = v.astype(BF16)
    r = v - a.astype(F32)
    b = r.astype(BF16)
    return a, b, (r - b.astype(F32)).astype(BF16)


def _split_dot(v, e_bf16):
    hi, lo = _split_bf16(v)
    return jnp.dot(jnp.concatenate([hi, lo], axis=1), jnp.concatenate([e_bf16, e_bf16], axis=0),
                   preferred_element_type=F32)


def _ssd_chunk(u, dt_blk, dtT_blk, bias_row, bias_col, alog_row, alog_col, is_f, states, want_y):
    q = u.shape[0]
    nh = SSD_HEADS
    xs = u[:, :D_SSD]
    bm = u[:, D_SSD:D_SSD + SSD_GROUPS * SSD_STATE]
    cm = u[:, D_SSD + SSD_GROUPS * SSD_STATE:]

    dt_all = _softplus(dt_blk + bias_row)
    dtT_all = _softplus(dtT_blk + bias_col)
    dt = jnp.where(is_f, dt_all[:, 0:nh], dt_all[:, nh:2 * nh])
    dtT = jnp.where(is_f, dtT_all[0:nh, :], dtT_all[nh:2 * nh, :])
    a_r = -jnp.exp(jnp.where(is_f, alog_row[0:1, :], alog_row[1:2, :])) * LOG2E
    a_c = -jnp.exp(jnp.where(is_f, alog_col[:, 0:1], alog_col[:, 1:2])) * LOG2E

    ri = lax.broadcasted_iota(jnp.int32, (q, q), 0)
    ci = lax.broadcasted_iota(jnp.int32, (q, q), 1)
    lower = (ri >= ci).astype(BF16)
    upper = (ri <= ci).astype(BF16)
    tmat = jnp.where(is_f, lower, upper)
    tmat_t = jnp.where(is_f, upper, lower)
    causal = tmat > 0.5
    acum = sum(jnp.dot(tmat, p, preferred_element_type=F32) for p in _split3(dt * a_r))
    acum_t = sum(jnp.dot(p, tmat_t, preferred_element_type=F32) for p in _split3(dtT * a_c))
    tot = jnp.where(is_f, acum[q - 1:q, :], acum[0:1, :])
    wgt = jnp.exp2(tot - acum) * dt

    hid = lax.broadcasted_iota(jnp.int32, (nh, D_SSD), 0)
    cid = lax.broadcasted_iota(jnp.int32, (nh, D_SSD), 1)
    expand = (cid // SSD_HEADDIM == hid).astype(BF16)
    pieces = [wgt, jnp.broadcast_to(jnp.exp2(tot), (SUBLANES, nh))]
    if want_y:
        pieces.append(jnp.exp2(acum))
    wide = _split_dot(jnp.concatenate(pieces, axis=0), expand)
    wgt_x = wide[0:q]
    dec_x = wide[q:q + 1]
    if want_y:
        ea_x = wide[q + SUBLANES:2 * q + SUBLANES]
        src_t = acum_t - jnp.log(dtT) * LOG2E

    ys, new_states = [], []
    for g in range(SSD_GROUPS):
        gsl = slice(g * GROUP_W, (g + 1) * GROUP_W)
        nsl = slice(g * SSD_STATE, (g + 1) * SSD_STATE)
        xg = xs[:, gsl]
        bm_g = bm[:, nsl].astype(BF16)
        upd = lax.dot_general(bm_g, (xg * wgt_x[:, gsl]).astype(BF16), _TN, preferred_element_type=F32)
        new_states.append(upd if states is None else states[g] * dec_x[:, gsl] + upd)
        if not want_y:
            continue
        cm_g = cm[:, nsl].astype(BF16)
        cb = lax.dot_general(cm_g, bm_g, _NT, preferred_element_type=F32)
        xb = xg.astype(BF16)
        y_heads = []
        for r in range(SSD_HPG):
            h = g * SSD_HPG + r
            seg = acum[:, h:h + 1] - src_t[h:h + 1, :]
            m_h = (cb * jnp.exp2(jnp.where(causal, seg, NEG_BIG))).astype(BF16)
            y_heads.append(jnp.dot(m_h, xb[:, r * SSD_HEADDIM:(r + 1) * SSD_HEADDIM], preferred_element_type=F32))
        y_g = jnp.concatenate(y_heads, axis=1)
        if states is not None:
            y_g = y_g + jnp.dot(cm_g, states[g].astype(BF16), preferred_element_type=F32) * ea_x[:, gsl]
        ys.append(y_g)
    return (jnp.concatenate(ys, axis=1) if want_y else None), new_states


def _ssd_ctx_kernel(x_ref, dt_ref, dtT_ref, cw_ref, cb_ref, brow_ref, bcol_ref, arow_ref, acol_ref, st_ref):
    is_f = pl.program_id(1) == 0
    zero_row = jnp.zeros((1, x_ref.shape[1]), F32)
    u = _ssd_conv(x_ref[...], zero_row, zero_row, cw_ref, cb_ref)
    _, new_states = _ssd_chunk(u, dt_ref[...], dtT_ref[...], brow_ref[...], bcol_ref[...], arow_ref[...],
                               acol_ref[...], is_f, None, False)
    for g in range(SSD_GROUPS):
        st_ref[0, 0, g] = new_states[g]


def _ssd_lat_kernel(x_ref, prev_ref, next_ref, dt_ref, dtT_ref, h0_ref, cw_ref, cb_ref, brow_ref, bcol_ref,
                    arow_ref, acol_ref, dskip_ref, o_ref, zero_ref, st_scr, *, n_chunks):
    d = pl.program_id(1)
    s = pl.program_id(2)
    is_f = d == 0
    chunk = jnp.where(is_f, s, n_chunks - 1 - s)
    q = x_ref.shape[0]
    zero_ref[...] = jnp.zeros(zero_ref.shape, zero_ref.dtype)

    @pl.when(s == 0)
    def _():
        st_scr[...] = h0_ref[0, 0]

    halo = prev_ref.shape[0]
    prev_row = jnp.where(chunk == 0, 0.0, prev_ref[halo - 1:halo, :].astype(F32))
    next_row = jnp.where(chunk == n_chunks - 1, 0.0, next_ref[0:1, :].astype(F32))
    u = _ssd_conv(x_ref[...], prev_row, next_row, cw_ref, cb_ref)
    states = [st_scr[g] for g in range(SSD_GROUPS)]
    y, new_states = _ssd_chunk(u, dt_ref[...], dtT_ref[...], brow_ref[...], bcol_ref[...], arow_ref[...],
                               acol_ref[...], is_f, states, True)
    for g in range(SSD_GROUPS):
        st_scr[g] = new_states[g]
    row0 = pl.multiple_of(chunk * q, q)

    @pl.when(is_f)
    def _():
        o_ref[pl.ds(row0, q), :] = y + dskip_ref[...] * u[:, :D_SSD]

    @pl.when(jnp.logical_not(is_f))
    def _():
        o_ref[pl.ds(row0, q), :] += y


def _ssd(xbc_lat_src, lat_col_block, dt_lat, dtT_lat, xbc_ctx, dt_ctx, dtT_ctx, conv_w, conv_b, dt_bias, a_log,
         d_skip, nb, ns):
    q = SSD_Q
    nl = xbc_ctx.shape[0] // nb
    nh = SSD_HEADS
    brow = jnp.zeros((1, LANES), F32).at[0, :2 * nh].set(dt_bias.reshape(-1))
    bcol = brow.reshape(LANES, 1)
    arow = a_log.reshape(2, nh)
    acol = arow.T
    cb2 = conv_b.reshape(1, XBC_DIM)
    const2 = lambda *_: (0, 0)
    param_specs = [pl.BlockSpec((3, XBC_DIM), const2), pl.BlockSpec((1, XBC_DIM), const2),
                   pl.BlockSpec((1, LANES), const2), pl.BlockSpec((LANES, 1), const2),
                   pl.BlockSpec((2, nh), const2), pl.BlockSpec((nh, 2), const2)]
    params = [conv_w, cb2, brow, bcol, arow, acol]
    state_shape = (nb, 2, SSD_GROUPS, SSD_STATE, GROUP_W)

    h0 = pl.pallas_call(
        _ssd_ctx_kernel,
        out_shape=jax.ShapeDtypeStruct(state_shape, F32),
        grid=(nb, 2),
        in_specs=[pl.BlockSpec((nl, XBC_DIM), lambda b, d: (b, 0)),
                  pl.BlockSpec((nl, LANES), lambda b, d: (b, 0)),
                  pl.BlockSpec((LANES, nl), lambda b, d: (0, b))] + param_specs,
        out_specs=pl.BlockSpec((1, 1, SSD_GROUPS, SSD_STATE, GROUP_W), lambda b, d: (b, d, 0, 0, 0)),
        compiler_params=_cparams("parallel", "arbitrary"),
    )(xbc_ctx, dt_ctx, dtT_ctx, *params)

    nc = ns // q
    halo = BF16_SUBLANES
    halo_per_chunk = q // halo
    n_halo = nb * ns // halo

    def chunk_of(d, s):
        return jnp.where(d == 0, s, nc - 1 - s)

    def prev_map(b, d, s):
        return (jnp.maximum((b * nc + chunk_of(d, s)) * halo_per_chunk - 1, 0), lat_col_block)

    def next_map(b, d, s):
        return (jnp.minimum((b * nc + chunk_of(d, s) + 1) * halo_per_chunk, n_halo - 1), lat_col_block)

    zrows = ns * TOP_K // (2 * nc)
    return pl.pallas_call(
        functools.partial(_ssd_lat_kernel, n_chunks=nc),
        out_shape=[jax.ShapeDtypeStruct((nb * ns, D_SSD), F32),
                   jax.ShapeDtypeStruct((nb * ns * TOP_K, D_MODEL), BF16)],
        grid=(nb, 2, nc),
        in_specs=[pl.BlockSpec((q, XBC_DIM), lambda b, d, s: (b * nc + chunk_of(d, s), lat_col_block)),
                  pl.BlockSpec((halo, XBC_DIM), prev_map),
                  pl.BlockSpec((halo, XBC_DIM), next_map),
                  pl.BlockSpec((q, LANES), lambda b, d, s: (b * nc + chunk_of(d, s), 0)),
                  pl.BlockSpec((LANES, q), lambda b, d, s: (0, b * nc + chunk_of(d, s))),
                  pl.BlockSpec((1, 1, SSD_GROUPS, SSD_STATE, GROUP_W), lambda b, d, s: (b, d, 0, 0, 0))]
        + param_specs + [pl.BlockSpec((1, D_SSD), const2)],
        out_specs=[pl.BlockSpec((ns, D_SSD), lambda b, d, s: (b, 0)),
                   pl.BlockSpec((zrows, D_MODEL), lambda b, d, s: ((b * 2 + d) * nc + s, 0))],
        scratch_shapes=[pltpu.VMEM((SSD_GROUPS, SSD_STATE, GROUP_W), F32)],
        compiler_params=_cparams("parallel", "arbitrary", "arbitrary"),
    )(xbc_lat_src, xbc_lat_src, xbc_lat_src, dt_lat, dtT_lat, h0, *params,
      jnp.repeat(d_skip, SSD_HEADDIM).reshape(1, D_SSD))


def _merge_kernel(y_ref, z_ref, ga_ref, gs_ref, oa_ref, x_ref, gt1_ref, sh2_ref, sc2_ref, gssd_ref, gpost_ref,
                  gpre_ref, wba_ref, wbs_ref, wout_ref, wrhi_ref, wrlo_ref, x1_ref, h2_ref, lg_ref):
    gw = D_SSD // SSD_GROUPS
    sub = MERGE_SUB
    nsub = y_ref.shape[0] // sub

    def gated_norm(r):
        y = y_ref[r, :] * _silu(z_ref[r, :].astype(F32))
        y = jnp.concatenate(
            [y[:, g * gw:(g + 1) * gw]
             * lax.rsqrt(jnp.mean(y[:, g * gw:(g + 1) * gw] * y[:, g * gw:(g + 1) * gw], axis=-1, keepdims=True) + EPS)
             for g in range(SSD_GROUPS)], axis=1)
        return (y * gssd_ref[...]).astype(BF16)

    def branches(r, y):
        ys = jnp.dot(y, wbs_ref[...], preferred_element_type=F32)
        ya = jnp.dot(oa_ref[r, :], wba_ref[...], preferred_element_type=F32)
        return (_sigmoid(ga_ref[r, :].astype(F32)) * ya + _sigmoid(gs_ref[r, :].astype(F32)) * ys).astype(BF16)

    def residual(r, mixed):
        mix = jnp.dot(mixed, wout_ref[...], preferred_element_type=F32)
        x1 = x_ref[r, :] + gt1_ref[0] * _rms(mix, gpost_ref[...])
        x1_ref[r, :] = x1
        h2 = _rms(x1, gpre_ref[...]) * (1.0 + sc2_ref[0]) + sh2_ref[0]
        h2_hi, h2_lo = _split_bf16(h2)
        h2_ref[r, :] = h2_hi
        lg_ref[r, :] = (jnp.dot(h2_hi, wrhi_ref[...], preferred_element_type=F32)
                        + jnp.dot(h2_lo, wrhi_ref[...], preferred_element_type=F32)
                        + jnp.dot(h2_hi, wrlo_ref[...], preferred_element_type=F32))

    rows = [slice(t * sub, (t + 1) * sub) for t in range(nsub)]
    ys, mixed = {}, {}
    for phase in range(nsub + 2):
        if 0 <= phase - 2 < nsub:
            residual(rows[phase - 2], mixed.pop(phase - 2))
        if 0 <= phase - 1 < nsub:
            mixed[phase - 1] = branches(rows[phase - 1], ys.pop(phase - 1))
        if phase < nsub:
            ys[phase] = gated_norm(rows[phase])


def _merge(y_ssd, rest, oattn, x2d, gt1, sh2, sc2, g_ssd_norm, g_post_mix, g_pre_ffn, wba, wbs, wout, w_router,
           rows_per_batch):
    m, d = x2d.shape
    tm = MERGE_TM
    tpb = rows_per_batch // tm
    row = lambda c: pl.BlockSpec((tm, d), lambda i: (i, c))
    per_batch = pl.BlockSpec((1, 1, d), lambda i: (i // tpb, 0, 0))
    vec = pl.BlockSpec((1, d), lambda i: (0, 0))
    mat = pl.BlockSpec((d, d), lambda i: (0, 0))
    wr = jnp.zeros((d, LANES), F32).at[:, :N_EXPERTS].set(w_router)
    wr_hi, wr_lo = _split_bf16(wr)
    wr_spec = pl.BlockSpec((d, LANES), lambda i: (0, 0))
    return pl.pallas_call(
        _merge_kernel,
        out_shape=[jax.ShapeDtypeStruct((m, d), F32), jax.ShapeDtypeStruct((m, d), BF16),
                   jax.ShapeDtypeStruct((m, LANES), F32)],
        grid=(m // tm,),
        in_specs=[row(0), row(2), row(3), row(4), row(0), row(0), per_batch, per_batch, per_batch, vec, vec, vec,
                  mat, mat, mat, wr_spec, wr_spec],
        out_specs=[row(0), row(0), pl.BlockSpec((tm, LANES), lambda i: (i, 0))],
        compiler_params=_cparams("parallel"),
    )(y_ssd, rest, rest, rest, oattn, x2d, gt1, sh2, sc2, g_ssd_norm, g_post_mix, g_pre_ffn, wba, wbs, wout,
      wr_hi, wr_lo)


def _first_index(hit, iota, limit):
    return jnp.min(jnp.where(hit, iota, limit), axis=0, keepdims=True)


def _route_kernel(lg_ref, bias_ref, eidx_ref, wts_ref, rank_ref, cnt_ref, carry):
    i = pl.program_id(0)
    tl = lg_ref.shape[0]

    @pl.when(i == 0)
    def _():
        carry[...] = jnp.zeros_like(carry)

    scores = _sigmoid(lg_ref[...].T[:N_EXPERTS])
    sel = scores + bias_ref[...]
    epg = EXPERTS_PER_GROUP
    io8 = lax.broadcasted_iota(jnp.int32, (epg, tl), 0)
    neg_inf = -jnp.inf

    grp = []
    for g in range(N_EXPERT_GROUPS):
        blk = sel[g * epg:(g + 1) * epg]
        m1 = jnp.max(blk, axis=0, keepdims=True)
        rest = jnp.where(io8 == _first_index(blk == m1, io8, epg), neg_inf, blk)
        grp.append(m1 + jnp.max(rest, axis=0, keepdims=True))
    gsc = jnp.concatenate(grp, axis=0)
    gkeep = jnp.zeros_like(gsc)
    for _ in range(TOPK_GROUPS):
        pick = io8 == _first_index(gsc == jnp.max(gsc, axis=0, keepdims=True), io8, N_EXPERT_GROUPS)
        gkeep = jnp.where(pick, 1.0, gkeep)
        gsc = jnp.where(pick, neg_inf, gsc)
    cand = jnp.concatenate(
        [jnp.where(gkeep[g:g + 1] > 0.5, sel[g * epg:(g + 1) * epg], neg_inf) for g in range(N_EXPERT_GROUPS)], axis=0)

    ioe = lax.broadcasted_iota(jnp.int32, (N_EXPERTS, tl), 0)
    picks, pscore = [], []
    chosen = jnp.zeros_like(cand)
    for _ in range(TOP_K):
        e_k = _first_index(cand == jnp.max(cand, axis=0, keepdims=True), ioe, N_EXPERTS)
        hit = ioe == e_k
        picks.append(e_k)
        pscore.append(jnp.sum(jnp.where(hit, scores, 0.0), axis=0, keepdims=True))
        chosen = jnp.where(hit, 1.0, chosen)
        cand = jnp.where(hit, neg_inf, cand)
    wsum = pscore[0]
    for k in range(1, TOP_K):
        wsum = wsum + pscore[k]

    ti = lax.broadcasted_iota(jnp.int32, (tl, tl), 0)
    tj = lax.broadcasted_iota(jnp.int32, (tl, tl), 1)
    before = (ti < tj).astype(BF16)
    rank = jnp.dot(chosen.astype(BF16), before, preferred_element_type=F32) + carry[...]
    total = carry[...] + jnp.sum(chosen, axis=1, keepdims=True)
    carry[...] = total
    cnt_ref[...] = jnp.broadcast_to(total, cnt_ref.shape)

    eidx_ref[...] = jnp.concatenate(picks, axis=0)
    wts_ref[...] = jnp.concatenate([p / wsum * ROUTED_SCALE for p in pscore], axis=0)
    rank_ref[...] = jnp.concatenate(
        [jnp.sum(jnp.where(ioe == e_k, rank, 0.0), axis=0, keepdims=True) for e_k in picks], axis=0).astype(jnp.int32)


def _route(logits, router_bias):
    m = logits.shape[0]
    ne = N_EXPERTS
    tl = ROUTE_TL
    tok = lambda dt: jax.ShapeDtypeStruct((TOP_K, m), dt)
    tok_spec = pl.BlockSpec((TOP_K, tl), lambda i: (0, i))
    return pl.pallas_call(
        _route_kernel,
        out_shape=[tok(jnp.int32), tok(F32), tok(jnp.int32), jax.ShapeDtypeStruct((ne, LANES), F32)],
        grid=(m // tl,),
        in_specs=[pl.BlockSpec((tl, LANES), lambda i: (i, 0)), pl.BlockSpec((ne, 1), lambda i: (0, 0))],
        out_specs=[tok_spec, tok_spec, tok_spec, pl.BlockSpec((ne, LANES), lambda i: (0, 0))],
        scratch_shapes=[pltpu.VMEM((ne, 1), F32)],
        compiler_params=_cparams("arbitrary"),
    )(logits, router_bias.reshape(ne, 1))


def _expert_kernel(tile_ref, e_ref, lo_ref, hi_ref, x_ref, wg_ref, wu_ref, wd_ref, o_ref, wg_s, wu_s, wd_s):
    w = pl.program_id(0)
    lo, hi = lo_ref[w], hi_ref[w]
    tm = x_ref.shape[0]
    base = tile_ref[w] * tm

    @pl.when((w == 0) | (e_ref[w] != e_ref[jnp.maximum(w - 1, 0)]))
    def _():
        wg_s[...] = wg_ref[0].astype(BF16)
        wu_s[...] = wu_ref[0].astype(BF16)
        wd_s[...] = wd_ref[0].astype(BF16)

    @pl.when(hi > lo)
    def _():
        x = x_ref[...]
        gate = jnp.dot(x, wg_s[...], preferred_element_type=F32)
        up = jnp.dot(x, wu_s[...], preferred_element_type=F32)
        y = jnp.dot((_silu(gate) * up).astype(BF16), wd_s[...], preferred_element_type=F32).astype(o_ref.dtype)
        rid = base + lax.broadcasted_iota(jnp.int32, (tm, 1), 0)

        @pl.when(lo == base)
        def _():
            o_ref[...] = jnp.where(rid < hi, y, jnp.zeros_like(y))

        @pl.when(lo != base)
        def _():
            o_ref[...] = jnp.where((rid >= lo) & (rid < hi), y, o_ref[...])


def _drop_alias_ref(kernel_fn, pos):
    def body(*refs):
        return kernel_fn(*refs[:pos], *refs[pos + 1:])
    return body


def _experts(item_tile, item_e, item_lo, item_hi, xs, w_gate, w_up, w_down, y_all, chunk):
    rows_c, d = xs.shape
    tm = EXPERT_TM
    ff = w_gate.shape[-1]
    tile0 = chunk * (rows_c // tm)
    row_map = lambda w, t, e, lo, hi: (t[w], 0)
    args = [item_tile, item_e, item_lo, item_hi, xs, w_gate, w_up, w_down, y_all]
    carry = len(args) - 1
    return pl.pallas_call(
        _drop_alias_ref(_expert_kernel, carry),
        out_shape=jax.ShapeDtypeStruct(y_all.shape, y_all.dtype),
        grid_spec=pltpu.PrefetchScalarGridSpec(
            num_scalar_prefetch=4,
            grid=(item_tile.shape[0],),
            in_specs=[pl.BlockSpec((tm, d), row_map),
                      pl.BlockSpec((1, d, ff), lambda w, t, e, lo, hi: (e[w], 0, 0)),
                      pl.BlockSpec((1, d, ff), lambda w, t, e, lo, hi: (e[w], 0, 0)),
                      pl.BlockSpec((1, ff, d), lambda w, t, e, lo, hi: (e[w], 0, 0)),
                      pl.BlockSpec(memory_space=pl.ANY)],
            out_specs=pl.BlockSpec((tm, d), lambda w, t, e, lo, hi: (tile0 + t[w], 0)),
            scratch_shapes=[pltpu.VMEM((d, ff), BF16), pltpu.VMEM((d, ff), BF16), pltpu.VMEM((ff, d), BF16)]),
        input_output_aliases={carry: 0},
        compiler_params=_cparams("arbitrary"),
    )(*args)


def _final_kernel(x1_ref, h2_ref, yk_ref, wt_ref, gt2_ref, gpost_ref, wg_ref, wu_ref, wd_ref, o_ref):
    h2 = h2_ref[...]
    gate = jnp.dot(h2, wg_ref[...], preferred_element_type=F32)
    up = jnp.dot(h2, wu_ref[...], preferred_element_type=F32)
    f = jnp.dot((_silu(gate) * up).astype(BF16), wd_ref[...], preferred_element_type=F32)
    wt = wt_ref[...]
    for k in range(TOP_K):
        f = f + yk_ref[k].astype(F32) * wt[:, k:k + 1]
    o_ref[...] = x1_ref[...] + gt2_ref[0] * _rms(f, gpost_ref[...])


def _final(x_carry, h2, y_tok, wts_t, gt2, g_post_ffn, w_sh_gate, w_sh_up, w_sh_down, rows_per_batch, chunk):
    m, d = x_carry.shape
    rows_c = y_tok.shape[1]
    tm = min(FINAL_TM, rows_c)
    tpb = rows_per_batch // tm
    off = chunk * (rows_c // tm)
    row = pl.BlockSpec((tm, d), lambda i: (off + i, 0))
    ff = w_sh_gate.shape[1]
    return pl.pallas_call(
        _final_kernel,
        out_shape=jax.ShapeDtypeStruct((m, d), F32),
        grid=(rows_c // tm,),
        in_specs=[row, row, pl.BlockSpec((TOP_K, tm, d), lambda i: (0, i, 0)),
                  pl.BlockSpec((tm, TOP_K), lambda i: (off + i, 0)),
                  pl.BlockSpec((1, 1, d), lambda i: ((off + i) // tpb, 0, 0)),
                  pl.BlockSpec((1, d), lambda i: (0, 0)),
                  pl.BlockSpec((d, ff), lambda i: (0, 0)), pl.BlockSpec((d, ff), lambda i: (0, 0)),
                  pl.BlockSpec((ff, d), lambda i: (0, 0))],
        out_specs=row,
        input_output_aliases={0: 0},
        compiler_params=_cparams("parallel"),
    )(x_carry, h2, y_tok, wts_t, gt2, g_post_ffn, w_sh_gate, w_sh_up, w_sh_down)


def kernel(x, c, ctx, c_ctx, w_ada, b_ada, g_pre_mix, g_post_mix, g_pre_ffn, g_post_ffn, w_in, lam_q1, lam_k1, lam_q2, lam_k2, g_attn_subln, conv_w, conv_b, dt_bias, a_log, d_skip, g_ssd_norm, w_branch_attn, w_branch_ssd, w_out, w_router, router_bias, w_e_gate, w_e_up, w_e_down, w_sh_gate, w_sh_up, w_sh_down):
    nb, ns, d = x.shape
    nl = ctx.shape[1]
    m = nb * ns
    li = 0

    c_all = jnp.zeros((3 * SUBLANES, d), F32).at[:nb].set(c).at[nb].set(c_ctx)
    mod = _modulation(c_all, w_ada[li], b_ada[li])
    sh1, sc1, gt1, sh2, sc2, gt2 = (mod[:nb, k * d:(k + 1) * d].reshape(nb, 1, d) for k in range(6))
    sh1c, sc1c = (mod[nb:nb + 1, k * d:(k + 1) * d].reshape(1, 1, d) for k in range(2))

    qk_dim = N_HEADS * 2 * ATTN_DH
    v_dim = N_HEADS * ATTN_DV
    o_q, o_k, o_v, o_z = 0, qk_dim, 2 * qk_dim, 2 * qk_dim + v_dim
    o_xbc = o_z + D_SSD
    o_dt = o_xbc + XBC_DIM
    o_g = o_dt + 2 * SSD_HEADS
    wi = w_in[li]
    w_qkv = wi[:, o_q:o_z].astype(BF16)
    w_qk_rot = _rotate_half_columns(wi[:, o_q:o_v]).astype(BF16)
    w_kv = wi[:, o_k:o_z].astype(BF16)
    w_xbc = wi[:, o_xbc:o_dt].astype(BF16)
    w_rest = jnp.concatenate([wi[:, o_xbc:o_dt], wi[:, o_z:o_xbc], wi[:, o_g:]], axis=1).astype(BF16)
    w_dt = jnp.zeros((d, LANES), BF16).at[:, :2 * SSD_HEADS].set(wi[:, o_dt:o_g].astype(BF16))

    x2d = x.reshape(m, d)
    c2d = ctx.reshape(nb * nl, d)
    g1 = g_pre_mix[li].reshape(1, d)
    cos, sin = _rope_tables(ns)
    qkv = _project(x2d, sh1, sc1, g1, w_qkv, rows_per_mod=ns, rope=(w_qk_rot, cos, sin, qk_dim))
    rest, dt_lat, dtT_lat = _project(x2d, sh1, sc1, g1, w_rest, rows_per_mod=ns, w_dt=w_dt)
    kv_ctx = _project(c2d, sh1c, sc1c, g1, w_kv, rows_per_mod=nb * nl)
    xbc_ctx, dt_ctx, dtT_ctx = _project(c2d, sh1c, sc1c, g1, w_xbc, rows_per_mod=nb * nl, w_dt=w_dt)

    lam = (jnp.exp(jnp.sum(lam_q1[li] * lam_k1[li])) - jnp.exp(jnp.sum(lam_q2[li] * lam_k2[li])) + LAM_INIT)
    oattn = _diff_attention(lam.reshape(1).astype(F32), qkv, kv_ctx, g_attn_subln[li].reshape(1, ATTN_DV), nb, ns)

    y_ssd, y_sorted = _ssd(rest, 0, dt_lat, dtT_lat, xbc_ctx, dt_ctx, dtT_ctx, conv_w[li], conv_b[li], dt_bias[li],
                           a_log[li], d_skip[li], nb, ns)

    x1, h2, logits = _merge(
        y_ssd, rest, oattn, x2d, gt1, sh2, sc2, g_ssd_norm[li].reshape(1, d), g_post_mix[li].reshape(1, d),
        g_pre_ffn[li].reshape(1, d), w_branch_attn[li].astype(BF16), w_branch_ssd[li].astype(BF16),
        w_out[li].astype(BF16), w_router[li], ns)

    eidx, wts, rank, counts = _route(logits, router_bias[li])

    tm = EXPERT_TM
    n_rows = m * TOP_K
    n_tiles = n_rows // tm
    counts = counts[:, 0].astype(jnp.int32)
    ends = jnp.cumsum(counts)
    start = ends - counts
    eid = jnp.arange(N_EXPERTS, dtype=jnp.int32)
    dest = jnp.sum(jnp.where(eidx[None] == eid[:, None, None], start[:, None, None], 0), axis=0) + rank
    tok = jnp.arange(m, dtype=jnp.int32)
    slot_tok = jnp.sort((eidx * m + tok[None, :]).reshape(-1), stable=False) % m

    rows_c = n_rows // MOE_CHUNKS
    tiles_c = rows_c // tm
    for ch in range(MOE_CHUNKS):
        row0 = ch * rows_c
        xs = h2.at[slot_tok[row0:row0 + rows_c]].get(mode="promise_in_bounds")
        cuts = jnp.sort(jnp.concatenate([jnp.arange(tiles_c, dtype=jnp.int32) * tm,
                                         jnp.clip(start[1:] - row0, 0, rows_c)]))
        item_hi = jnp.concatenate([cuts[1:], jnp.full((1,), rows_c, jnp.int32)])
        item_tile = jnp.minimum(cuts // tm, tiles_c - 1)
        item_e = jnp.minimum(jnp.sum((ends[None, :] <= (cuts + row0)[:, None]).astype(jnp.int32), axis=1),
                             N_EXPERTS - 1)
        y_sorted = _experts(item_tile, item_e, cuts, item_hi, xs, w_e_gate[li], w_e_up[li], w_e_down[li],
                            y_sorted, ch)

    tok_c = m // COMBINE_CHUNKS
    wts_t = wts.T
    w_sh = (w_sh_gate[li].astype(BF16), w_sh_up[li].astype(BF16), w_sh_down[li].astype(BF16))
    out = x1
    for ch in range(COMBINE_CHUNKS):
        dest_c = dest[:, ch * tok_c:(ch + 1) * tok_c].reshape(-1)
        y_tok = y_sorted.at[dest_c].get(mode="promise_in_bounds").reshape(TOP_K, tok_c, d)
        out = _final(out, h2, y_tok, wts_t, gt2, g_post_ffn[li].reshape(1, d), *w_sh, ns, ch)
    return out.reshape(nb, ns, d)
```

```python
import functools
import math

import jax
import jax.numpy as jnp
from jax import lax
from jax.experimental import pallas as pl
from jax.experimental.pallas import tpu as pltpu

F32 = jnp.float32
BF16 = jnp.bfloat16
HIGHEST = lax.Precision.HIGHEST

D_MODEL = 1024
CTX_LEN = 256
GRID_W = 64
EPS = 1e-6

N_HEADS = 8
ATTN_DH = 64
ATTN_DV = 128
ATTN_SCALE = ATTN_DH ** -0.5
LOG2E = math.log2(math.e)
ROPE_THETA = 10000.0
ROPE_AXIS_DIM = ATTN_DH // 2
LAM_INIT = 0.8 - 0.6 * math.exp(-0.3 * 0)

D_SSD = 1024
SSD_HEADDIM = 64
SSD_HEADS = 16
SSD_GROUPS = 4
SSD_HPG = 4
SSD_STATE = 128
XBC_DIM = D_SSD + 2 * SSD_GROUPS * SSD_STATE
GROUP_W = SSD_HPG * SSD_HEADDIM

N_EXPERTS = 64
EXPERT_FF = 256
TOP_K = 8
N_EXPERT_GROUPS = 8
EXPERTS_PER_GROUP = N_EXPERTS // N_EXPERT_GROUPS
TOPK_GROUPS = 4
ROUTED_SCALE = 2.5

LANES = 128
SUBLANES = 8
BF16_SUBLANES = 16
VMEM_LIMIT = 52 * 1024 * 1024

PROJ_TM = 512
PROJ_TN = 512
ATTN_TQ = 2048
ATTN_TH = 256
ATTN_KB = 256
SSD_Q = 256
MERGE_TM = 512
MERGE_SUB = 256
ROUTE_TL = 512
EXPERT_TM = 1024
FINAL_TM = 512
MOE_CHUNKS = 4
COMBINE_CHUNKS = 16

NEG_BIG = -1e30

_NT = (((1,), (1,)), ((), ()))
_TN = (((0,), (0,)), ((), ()))


def _cparams(*sem):
    return pltpu.CompilerParams(dimension_semantics=sem, vmem_limit_bytes=VMEM_LIMIT)


def _sigmoid(v):
    return 0.5 * jnp.tanh(0.5 * v) + 0.5


def _silu(v):
    h = 0.5 * v
    return h + h * jnp.tanh(h)


def _softplus(v):
    return jnp.maximum(v, 0.0) + jnp.log1p(jnp.exp(-jnp.abs(v)))


def _rms(v, g):
    return v * lax.rsqrt(jnp.mean(v * v, axis=-1, keepdims=True) + EPS) * g


def _split_bf16(v):
    hi = v.astype(BF16)
    return hi, (v - hi.astype(F32)).astype(BF16)


def _mod_kernel(c_ref, w_ref, b_ref, o_ref):
    o_ref[...] = jnp.dot(_silu(c_ref[...]), w_ref[...], preferred_element_type=F32,
                         precision=HIGHEST) + b_ref[...]


def _modulation(c_all, w_ada, b_ada):
    rows, d = c_all.shape
    n = w_ada.shape[1]
    tn = 1024
    return pl.pallas_call(
        _mod_kernel,
        out_shape=jax.ShapeDtypeStruct((rows, n), F32),
        grid=(n // tn,),
        in_specs=[pl.BlockSpec((rows, d), lambda j: (0, 0)),
                  pl.BlockSpec((d, tn), lambda j: (0, j)),
                  pl.BlockSpec((1, tn), lambda j: (0, j))],
        out_specs=pl.BlockSpec((rows, tn), lambda j: (0, j)),
        compiler_params=_cparams("arbitrary"),
    )(c_all, w_ada, b_ada.reshape(1, n))


def _proj_kernel(*refs, n_rot, n_q, has_dt):
    it = iter(refs)
    x_ref, sh_ref, sc_ref, g_ref, w_ref = (next(it) for _ in range(5))
    if n_rot:
        wrot_ref, cos_ref, sin_ref = next(it), next(it), next(it)
    if has_dt:
        wdt_ref, wdtT_ref = next(it), next(it)
    o_ref = next(it)
    if has_dt:
        dt_ref, dtT_ref = next(it), next(it)

    h = (_rms(x_ref[...], g_ref[...]) * (1.0 + sc_ref[0]) + sh_ref[0]).astype(BF16)
    if has_dt:
        dt_ref[...] = jnp.dot(h, wdt_ref[...], preferred_element_type=F32)
        dtT_ref[...] = lax.dot_general(wdtT_ref[...], h, _NT, preferred_element_type=F32)
    tn = PROJ_TN
    for c in range(w_ref.shape[1] // tn):
        cs = slice(c * tn, (c + 1) * tn)
        u = jnp.dot(h, w_ref[:, cs], preferred_element_type=F32)
        if c * tn < n_rot:
            ur = jnp.dot(h, wrot_ref[:, cs], preferred_element_type=F32)
            scale = ATTN_SCALE * LOG2E if c * tn < n_q else 1.0
            cos = cos_ref[...] * scale
            sin = sin_ref[...] * scale
            for s in range(tn // LANES):
                sl = slice(s * LANES, (s + 1) * LANES)
                o_ref[:, c * tn + s * LANES:c * tn + (s + 1) * LANES] = (
                    u[:, sl] * cos + ur[:, sl] * sin).astype(o_ref.dtype)
        else:
            o_ref[:, cs] = u.astype(o_ref.dtype)


def _project(x2d, shift, scale, gain, w, *, rows_per_mod, rope=None, w_dt=None):
    m, d = x2d.shape
    n = w.shape[1]
    tm = min(PROJ_TM, rows_per_mod)
    tiles_per_mod = rows_per_mod // tm
    has_dt = w_dt is not None
    const = lambda i: (0, 0)
    in_specs = [pl.BlockSpec((tm, d), lambda i: (i, 0)),
                pl.BlockSpec((1, 1, d), lambda i: (i // tiles_per_mod, 0, 0)),
                pl.BlockSpec((1, 1, d), lambda i: (i // tiles_per_mod, 0, 0)),
                pl.BlockSpec((1, d), const),
                pl.BlockSpec((d, n), const)]
    args = [x2d, shift, scale, gain, w]
    out_shape = [jax.ShapeDtypeStruct((m, n), BF16)]
    out_specs = [pl.BlockSpec((tm, n), lambda i: (i, 0))]
    n_rot = n_q = 0
    if rope is not None:
        w_rot, cos, sin, n_q = rope
        n_rot = w_rot.shape[1]
        pos_tiles = cos.shape[0] // tm
        in_specs += [pl.BlockSpec((d, n_rot), const)] + [pl.BlockSpec((tm, LANES), lambda i: (i % pos_tiles, 0))] * 2
        args += [w_rot, cos, sin]
    if has_dt:
        in_specs += [pl.BlockSpec((d, LANES), const), pl.BlockSpec((LANES, d), const)]
        args += [w_dt, w_dt.T]
        out_shape += [jax.ShapeDtypeStruct((m, LANES), F32), jax.ShapeDtypeStruct((LANES, m), F32)]
        out_specs += [pl.BlockSpec((tm, LANES), lambda i: (i, 0)), pl.BlockSpec((LANES, tm), lambda i: (0, i))]
    res = pl.pallas_call(
        functools.partial(_proj_kernel, n_rot=n_rot, n_q=n_q, has_dt=has_dt),
        out_shape=out_shape,
        grid=(m // tm,),
        in_specs=in_specs,
        out_specs=out_specs,
        compiler_params=_cparams("parallel"),
    )(*args)
    return res if has_dt else res[0]


def _rope_tables(n_tok):
    rows = n_tok // GRID_W
    row = jnp.repeat(jnp.arange(rows, dtype=F32), GRID_W)
    col = jnp.broadcast_to(jnp.arange(GRID_W, dtype=F32)[None, :], (rows, GRID_W)).reshape(-1)
    inv_freq = ROPE_THETA ** (-jnp.arange(0, ROPE_AXIS_DIM, 2, dtype=F32) / ROPE_AXIS_DIM)
    ang = jnp.concatenate([row[:, None] * inv_freq, col[:, None] * inv_freq], axis=-1)
    ang = jnp.concatenate([ang, ang, ang, ang], axis=-1)
    return jnp.cos(ang), jnp.sin(ang)


def _rotate_half_columns(w):
    d, n = w.shape
    half = ATTN_DH // 2
    wc = w.reshape(d, n // ATTN_DH, 2, half)
    return jnp.stack([-wc[:, :, 1], wc[:, :, 0]], axis=2).reshape(d, n)


def _attn_kernel(lam_ref, q_ref, kc_ref, k_ref, vc_ref, v_ref, g_ref, o_ref, s_scr, e_scr):
    th, kb = ATTN_TH, ATTN_KB
    nsub = q_ref.shape[0] // th
    lam = lam_ref[0]
    blocks = ([(kc_ref, vc_ref, o) for o in range(0, kc_ref.shape[0], kb)]
              + [(k_ref, v_ref, o) for o in range(0, k_ref.shape[0], kb)])

    def stacked_q(t):
        q = q_ref[t * th:(t + 1) * th, :]
        lane = lax.broadcasted_iota(jnp.int32, q.shape, 1)
        zero = jnp.zeros_like(q)
        return jnp.concatenate([jnp.where(lane < ATTN_DH, q, zero), jnp.where(lane >= ATTN_DH, q, zero)], axis=0)

    def lane_fold(v, op):
        out = v[:, :LANES]
        for c in range(1, v.shape[1] // LANES):
            out = op(out, v[:, c * LANES:(c + 1) * LANES])
        return out

    def scores(slot, qq, j, mx):
        k_ref_j, _, off = blocks[j]
        s = lax.dot_general(qq, k_ref_j[off:off + kb, :], _NT, preferred_element_type=F32)
        s_scr[slot, :, j * kb:(j + 1) * kb] = s
        part = lane_fold(s, jnp.maximum)
        return part if mx is None else jnp.maximum(mx, part)

    def expo(slot, m, j, sm):
        e = jnp.exp2(s_scr[slot, :, j * kb:(j + 1) * kb] - m)
        e_scr[slot, :, j * kb:(j + 1) * kb] = e
        part = lane_fold(e, jnp.add)
        return part if sm is None else sm + part

    def pv(slot, a0, a1, j, acc):
        _, v_ref_j, off = blocks[j]
        e = e_scr[slot, :, j * kb:(j + 1) * kb]
        w = (e[:th] * a0 - e[th:] * a1).astype(BF16)
        o = jnp.dot(w, v_ref_j[off:off + kb, :], preferred_element_type=F32)
        return o if acc is None else acc + o

    st = [dict() for _ in range(nsub)]
    for phase in range(nsub + 2):
        ts, te, tp = phase, phase - 1, phase - 2
        if 0 <= ts < nsub:
            st[ts].update(qq=stacked_q(ts), mx=None)
        if 0 <= te < nsub:
            st[te].update(m=jnp.max(st[te]["mx"], axis=-1, keepdims=True), sm=None)
        if 0 <= tp < nsub:
            inv = 1.0 / jnp.sum(st[tp]["sm"], axis=-1, keepdims=True)
            st[tp].update(a0=inv[:th], a1=inv[th:] * lam, acc=None)
        for j in range(len(blocks)):
            if 0 <= ts < nsub:
                st[ts]["mx"] = scores(ts % 2, st[ts]["qq"], j, st[ts]["mx"])
            if 0 <= tp < nsub:
                st[tp]["acc"] = pv(tp % 2, st[tp]["a0"], st[tp]["a1"], j, st[tp]["acc"])
            if 0 <= te < nsub:
                st[te]["sm"] = expo(te % 2, st[te]["m"], j, st[te]["sm"])
        if 0 <= tp < nsub:
            o = _rms(st[tp]["acc"], g_ref[...]) * (1.0 - LAM_INIT)
            o_ref[tp * th:(tp + 1) * th, :] = o.astype(o_ref.dtype)


def _diff_attention(lam, qkv, kv_ctx, g_subln, nb, ns):
    tq = ATTN_TQ
    nq = ns // tq
    nl = kv_ctx.shape[0] // nb
    stage_buf = pltpu.VMEM((2, 2 * ATTN_TH, nl + ns), F32)
    return pl.pallas_call(
        _attn_kernel,
        out_shape=jax.ShapeDtypeStruct((nb * ns, N_HEADS * ATTN_DV), BF16),
        grid=(nb, N_HEADS, nq),
        in_specs=[pl.BlockSpec(memory_space=pltpu.SMEM),
                  pl.BlockSpec((tq, LANES), lambda b, h, i: (b * nq + i, h)),
                  pl.BlockSpec((nl, LANES), lambda b, h, i: (b, h)),
                  pl.BlockSpec((ns, LANES), lambda b, h, i: (b, N_HEADS + h)),
                  pl.BlockSpec((nl, LANES), lambda b, h, i: (b, N_HEADS + h)),
                  pl.BlockSpec((ns, LANES), lambda b, h, i: (b, 2 * N_HEADS + h)),
                  pl.BlockSpec((1, LANES), lambda b, h, i: (0, 0))],
        out_specs=pl.BlockSpec((tq, LANES), lambda b, h, i: (b * nq + i, h)),
        scratch_shapes=[stage_buf, stage_buf],
        compiler_params=_cparams("parallel", "parallel", "arbitrary"),
    )(lam, qkv, kv_ctx, qkv, kv_ctx, qkv, g_subln)


def _ssd_conv(x, prev_row, next_row, cw_ref, cb_ref):
    q = x.shape[0]
    ri = lax.broadcasted_iota(jnp.int32, (q, q), 0)
    ci = lax.broadcasted_iota(jnp.int32, (q, q), 1)
    up = jnp.dot((ci == ri - 1).astype(BF16), x, preferred_element_type=F32)
    dn = jnp.dot((ci == ri + 1).astype(BF16), x, preferred_element_type=F32)
    rid = lax.broadcasted_iota(jnp.int32, (SUBLANES, 1), 0)
    up = jnp.concatenate([jnp.where(rid == 0, prev_row, up[:SUBLANES]), up[SUBLANES:]], axis=0)
    dn = jnp.concatenate([dn[:q - SUBLANES], jnp.where(rid == SUBLANES - 1, next_row, dn[q - SUBLANES:])], axis=0)
    return _silu(cb_ref[...] + up * cw_ref[0:1, :] + x.astype(F32) * cw_ref[1:2, :] + dn * cw_ref[2:3, :])


def _split3(v):
    a = v.astype(BF16)
    r = v - a.astype(F32)
    b = r.astype(BF16)
    return a, b, (r - b.astype(F32)).astype(BF16)


def _split_dot(v, e_bf16):
    hi, lo = _split_bf16(v)
    return jnp.dot(jnp.concatenate([hi, lo], axis=1), jnp.concatenate([e_bf16, e_bf16], axis=0),
                   preferred_element_type=F32)


def _ssd_chunk(u, dt_blk, dtT_blk, bias_row, bias_col, alog_row, alog_col, is_f, states, want_y):
    q = u.shape[0]
    nh = SSD_HEADS
    xs = u[:, :D_SSD]
    bm = u[:, D_SSD:D_SSD + SSD_GROUPS * SSD_STATE]
    cm = u[:, D_SSD + SSD_GROUPS * SSD_STATE:]

    dt_all = _softplus(dt_blk + bias_row)
    dtT_all = _softplus(dtT_blk + bias_col)
    dt = jnp.where(is_f, dt_all[:, 0:nh], dt_all[:, nh:2 * nh])
    dtT = jnp.where(is_f, dtT_all[0:nh, :], dtT_all[nh:2 * nh, :])
    a_r = -jnp.exp(jnp.where(is_f, alog_row[0:1, :], alog_row[1:2, :])) * LOG2E
    a_c = -jnp.exp(jnp.where(is_f, alog_col[:, 0:1], alog_col[:, 1:2])) * LOG2E

    ri = lax.broadcasted_iota(jnp.int32, (q, q), 0)
    ci = lax.broadcasted_iota(jnp.int32, (q, q), 1)
    lower = (ri >= ci).astype(BF16)
    upper = (ri <= ci).astype(BF16)
    tmat = jnp.where(is_f, lower, upper)
    tmat_t = jnp.where(is_f, upper, lower)
    causal = tmat > 0.5
    acum = sum(jnp.dot(tmat, p, preferred_element_type=F32) for p in _split3(dt * a_r))
    acum_t = sum(jnp.dot(p, tmat_t, preferred_element_type=F32) for p in _split3(dtT * a_c))
    tot = jnp.where(is_f, acum[q - 1:q, :], acum[0:1, :])
    wgt = jnp.exp2(tot - acum) * dt

    hid = lax.broadcasted_iota(jnp.int32, (nh, D_SSD), 0)
    cid = lax.broadcasted_iota(jnp.int32, (nh, D_SSD), 1)
    expand = (cid // SSD_HEADDIM == hid).astype(BF16)
    pieces = [wgt, jnp.broadcast_to(jnp.exp2(tot), (SUBLANES, nh))]
    if want_y:
        pieces.append(jnp.exp2(acum))
    wide = _split_dot(jnp.concatenate(pieces, axis=0), expand)
    wgt_x = wide[0:q]
    dec_x = wide[q:q + 1]
    if want_y:
        ea_x = wide[q + SUBLANES:2 * q + SUBLANES]
        src_t = acum_t - jnp.log(dtT) * LOG2E

    ys, new_states = [], []
    for g in range(SSD_GROUPS):
        gsl = slice(g * GROUP_W, (g + 1) * GROUP_W)
        nsl = slice(g * SSD_STATE, (g + 1) * SSD_STATE)
        xg = xs[:, gsl]
        bm_g = bm[:, nsl].astype(BF16)
        upd = lax.dot_general(bm_g, (xg * wgt_x[:, gsl]).astype(BF16), _TN, preferred_element_type=F32)
        new_states.append(upd if states is None else states[g] * dec_x[:, gsl] + upd)
        if not want_y:
            continue
        cm_g = cm[:, nsl].astype(BF16)
        cb = lax.dot_general(cm_g, bm_g, _NT, preferred_element_type=F32)
        xb = xg.astype(BF16)
        y_heads = []
        for r in range(SSD_HPG):
            h = g * SSD_HPG + r
            seg = acum[:, h:h + 1] - src_t[h:h + 1, :]
            m_h = (cb * jnp.exp2(jnp.where(causal, seg, NEG_BIG))).astype(BF16)
            y_heads.append(jnp.dot(m_h, xb[:, r * SSD_HEADDIM:(r + 1) * SSD_HEADDIM], preferred_element_type=F32))
        y_g = jnp.concatenate(y_heads, axis=1)
        if states is not None:
            y_g = y_g + jnp.dot(cm_g, states[g].astype(BF16), preferred_element_type=F32) * ea_x[:, gsl]
        ys.append(y_g)
    return (jnp.concatenate(ys, axis=1) if want_y else None), new_states


def _ssd_ctx_kernel(x_ref, dt_ref, dtT_ref, cw_ref, cb_ref, brow_ref, bcol_ref, arow_ref, acol_ref, st_ref):
    is_f = pl.program_id(1) == 0
    zero_row = jnp.zeros((1, x_ref.shape[1]), F32)
    u = _ssd_conv(x_ref[...], zero_row, zero_row, cw_ref, cb_ref)
    _, new_states = _ssd_chunk(u, dt_ref[...], dtT_ref[...], brow_ref[...], bcol_ref[...], arow_ref[...],
                               acol_ref[...], is_f, None, False)
    for g in range(SSD_GROUPS):
        st_ref[0, 0, g] = new_states[g]


def _ssd_lat_kernel(x_ref, prev_ref, next_ref, dt_ref, dtT_ref, h0_ref, cw_ref, cb_ref, brow_ref, bcol_ref,
                    arow_ref, acol_ref, dskip_ref, o_ref, zero_ref, st_scr, *, n_chunks):
    d = pl.program_id(1)
    s = pl.program_id(2)
    is_f = d == 0
    chunk = jnp.where(is_f, s, n_chunks - 1 - s)
    q = x_ref.shape[0]
    zero_ref[...] = jnp.zeros(zero_ref.shape, zero_ref.dtype)

    @pl.when(s == 0)
    def _():
        st_scr[...] = h0_ref[0, 0]

    halo = prev_ref.shape[0]
    prev_row = jnp.where(chunk == 0, 0.0, prev_ref[halo - 1:halo, :].astype(F32))
    next_row = jnp.where(chunk == n_chunks - 1, 0.0, next_ref[0:1, :].astype(F32))
    u = _ssd_conv(x_ref[...], prev_row, next_row, cw_ref, cb_ref)
    states = [st_scr[g] for g in range(SSD_GROUPS)]
    y, new_states = _ssd_chunk(u, dt_ref[...], dtT_ref[...], brow_ref[...], bcol_ref[...], arow_ref[...],
                               acol_ref[...], is_f, states, True)
    for g in range(SSD_GROUPS):
        st_scr[g] = new_states[g]
    row0 = pl.multiple_of(chunk * q, q)

    @pl.when(is_f)
    def _():
        o_ref[pl.ds(row0, q), :] = y + dskip_ref[...] * u[:, :D_SSD]

    @pl.when(jnp.logical_not(is_f))
    def _():
        o_ref[pl.ds(row0, q), :] += y


def _ssd(xbc_lat_src, lat_col_block, dt_lat, dtT_lat, xbc_ctx, dt_ctx, dtT_ctx, conv_w, conv_b, dt_bias, a_log,
         d_skip, nb, ns):
    q = SSD_Q
    nl = xbc_ctx.shape[0] // nb
    nh = SSD_HEADS
    brow = jnp.zeros((1, LANES), F32).at[0, :2 * nh].set(dt_bias.reshape(-1))
    bcol = brow.reshape(LANES, 1)
    arow = a_log.reshape(2, nh)
    acol = arow.T
    cb2 = conv_b.reshape(1, XBC_DIM)
    const2 = lambda *_: (0, 0)
    param_specs = [pl.BlockSpec((3, XBC_DIM), const2), pl.BlockSpec((1, XBC_DIM), const2),
                   pl.BlockSpec((1, LANES), const2), pl.BlockSpec((LANES, 1), const2),
                   pl.BlockSpec((2, nh), const2), pl.BlockSpec((nh, 2), const2)]
    params = [conv_w, cb2, brow, bcol, arow, acol]
    state_shape = (nb, 2, SSD_GROUPS, SSD_STATE, GROUP_W)

    h0 = pl.pallas_call(
        _ssd_ctx_kernel,
        out_shape=jax.ShapeDtypeStruct(state_shape, F32),
        grid=(nb, 2),
        in_specs=[pl.BlockSpec((nl, XBC_DIM), lambda b, d: (b, 0)),
                  pl.BlockSpec((nl, LANES), lambda b, d: (b, 0)),
                  pl.BlockSpec((LANES, nl), lambda b, d: (0, b))] + param_specs,
        out_specs=pl.BlockSpec((1, 1, SSD_GROUPS, SSD_STATE, GROUP_W), lambda b, d: (b, d, 0, 0, 0)),
        compiler_params=_cparams("parallel", "arbitrary"),
    )(xbc_ctx, dt_ctx, dtT_ctx, *params)

    nc = ns // q
    halo = BF16_SUBLANES
    halo_per_chunk = q // halo
    n_halo = nb * ns // halo

    def chunk_of(d, s):
        return jnp.where(d == 0, s, nc - 1 - s)

    def prev_map(b, d, s):
        return (jnp.maximum((b * nc + chunk_of(d, s)) * halo_per_chunk - 1, 0), lat_col_block)

    def next_map(b, d, s):
        return (jnp.minimum((b * nc + chunk_of(d, s) + 1) * halo_per_chunk, n_halo - 1), lat_col_block)

    zrows = ns * TOP_K // (2 * nc)
    return pl.pallas_call(
        functools.partial(_ssd_lat_kernel, n_chunks=nc),
        out_shape=[jax.ShapeDtypeStruct((nb * ns, D_SSD), F32),
                   jax.ShapeDtypeStruct((nb * ns * TOP_K, D_MODEL), BF16)],
        grid=(nb, 2, nc),
        in_specs=[pl.BlockSpec((q, XBC_DIM), lambda b, d, s: (b * nc + chunk_of(d, s), lat_col_block)),
                  pl.BlockSpec((halo, XBC_DIM), prev_map),
                  pl.BlockSpec((halo, XBC_DIM), next_map),
                  pl.BlockSpec((q, LANES), lambda b, d, s: (b * nc + chunk_of(d, s), 0)),
                  pl.BlockSpec((LANES, q), lambda b, d, s: (0, b * nc + chunk_of(d, s))),
                  pl.BlockSpec((1, 1, SSD_GROUPS, SSD_STATE, GROUP_W), lambda b, d, s: (b, d, 0, 0, 0))]
        + param_specs + [pl.BlockSpec((1, D_SSD), const2)],
        out_specs=[pl.BlockSpec((ns, D_SSD), lambda b, d, s: (b, 0)),
                   pl.BlockSpec((zrows, D_MODEL), lambda b, d, s: ((b * 2 + d) * nc + s, 0))],
        scratch_shapes=[pltpu.VMEM((SSD_GROUPS, SSD_STATE, GROUP_W), F32)],
        compiler_params=_cparams("parallel", "arbitrary", "arbitrary"),
    )(xbc_lat_src, xbc_lat_src, xbc_lat_src, dt_lat, dtT_lat, h0, *params,
      jnp.repeat(d_skip, SSD_HEADDIM).reshape(1, D_SSD))


def _merge_kernel(y_ref, z_ref, ga_ref, gs_ref, oa_ref, x_ref, gt1_ref, sh2_ref, sc2_ref, gssd_ref, gpost_ref,
                  gpre_ref, wba_ref, wbs_ref, wout_ref, wrhi_ref, wrlo_ref, x1_ref, h2_ref, lg_ref):
    gw = D_SSD // SSD_GROUPS
    sub = MERGE_SUB
    nsub = y_ref.shape[0] // sub

    def gated_norm(r):
        y = y_ref[r, :] * _silu(z_ref[r, :].astype(F32))
        y = jnp.concatenate(
            [y[:, g * gw:(g + 1) * gw]
             * lax.rsqrt(jnp.mean(y[:, g * gw:(g + 1) * gw] * y[:, g * gw:(g + 1) * gw], axis=-1, keepdims=True) + EPS)
             for g in range(SSD_GROUPS)], axis=1)
        return (y * gssd_ref[...]).astype(BF16)

    def branches(r, y):
        ys = jnp.dot(y, wbs_ref[...], preferred_element_type=F32)
        ya = jnp.dot(oa_ref[r, :], wba_ref[...], preferred_element_type=F32)
        return (_sigmoid(ga_ref[r, :].astype(F32)) * ya + _sigmoid(gs_ref[r, :].astype(F32)) * ys).astype(BF16)

    def residual(r, mixed):
        mix = jnp.dot(mixed, wout_ref[...], preferred_element_type=F32)
        x1 = x_ref[r, :] + gt1_ref[0] * _rms(mix, gpost_ref[...])
        x1_ref[r, :] = x1
        h2 = _rms(x1, gpre_ref[...]) * (1.0 + sc2_ref[0]) + sh2_ref[0]
        h2_hi, h2_lo = _split_bf16(h2)
        h2_ref[r, :] = h2_hi
        lg_ref[r, :] = (jnp.dot(h2_hi, wrhi_ref[...], preferred_element_type=F32)
                        + jnp.dot(h2_lo, wrhi_ref[...], preferred_element_type=F32)
                        + jnp.dot(h2_hi, wrlo_ref[...], preferred_element_type=F32))

    rows = [slice(t * sub, (t + 1) * sub) for t in range(nsub)]
    ys, mixed = {}, {}
    for phase in range(nsub + 2):
        if 0 <= phase - 2 < nsub:
            residual(rows[phase - 2], mixed.pop(phase - 2))
        if 0 <= phase - 1 < nsub:
            mixed[phase - 1] = branches(rows[phase - 1], ys.pop(phase - 1))
        if phase < nsub:
            ys[phase] = gated_norm(rows[phase])


def _merge(y_ssd, rest, oattn, x2d, gt1, sh2, sc2, g_ssd_norm, g_post_mix, g_pre_ffn, wba, wbs, wout, w_router,
           rows_per_batch):
    m, d = x2d.shape
    tm = MERGE_TM
    tpb = rows_per_batch // tm
    row = lambda c: pl.BlockSpec((tm, d), lambda i: (i, c))
    per_batch = pl.BlockSpec((1, 1, d), lambda i: (i // tpb, 0, 0))
    vec = pl.BlockSpec((1, d), lambda i: (0, 0))
    mat = pl.BlockSpec((d, d), lambda i: (0, 0))
    wr = jnp.zeros((d, LANES), F32).at[:, :N_EXPERTS].set(w_router)
    wr_hi, wr_lo = _split_bf16(wr)
    wr_spec = pl.BlockSpec((d, LANES), lambda i: (0, 0))
    return pl.pallas_call(
        _merge_kernel,
        out_shape=[jax.ShapeDtypeStruct((m, d), F32), jax.ShapeDtypeStruct((m, d), BF16),
                   jax.ShapeDtypeStruct((m, LANES), F32)],
        grid=(m // tm,),
        in_specs=[row(0), row(2), row(3), row(4), row(0), row(0), per_batch, per_batch, per_batch, vec, vec, vec,
                  mat, mat, mat, wr_spec, wr_spec],
        out_specs=[row(0), row(0), pl.BlockSpec((tm, LANES), lambda i: (i, 0))],
        compiler_params=_cparams("parallel"),
    )(y_ssd, rest, rest, rest, oattn, x2d, gt1, sh2, sc2, g_ssd_norm, g_post_mix, g_pre_ffn, wba, wbs, wout,
      wr_hi, wr_lo)


def _first_index(hit, iota, limit):
    return jnp.min(jnp.where(hit, iota, limit), axis=0, keepdims=True)


def _route_kernel(lg_ref, bias_ref, eidx_ref, wts_ref, rank_ref, cnt_ref, carry):
    i = pl.program_id(0)
    tl = lg_ref.shape[0]

    @pl.when(i == 0)
    def _():
        carry[...] = jnp.zeros_like(carry)

    scores = _sigmoid(lg_ref[...].T[:N_EXPERTS])
    sel = scores + bias_ref[...]
    epg = EXPERTS_PER_GROUP
    io8 = lax.broadcasted_iota(jnp.int32, (epg, tl), 0)
    neg_inf = -jnp.inf

    grp = []
    for g in range(N_EXPERT_GROUPS):
        blk = sel[g * epg:(g + 1) * epg]
        m1 = jnp.max(blk, axis=0, keepdims=True)
        rest = jnp.where(io8 == _first_index(blk == m1, io8, epg), neg_inf, blk)
        grp.append(m1 + jnp.max(rest, axis=0, keepdims=True))
    gsc = jnp.concatenate(grp, axis=0)
    gkeep = jnp.zeros_like(gsc)
    for _ in range(TOPK_GROUPS):
        pick = io8 == _first_index(gsc == jnp.max(gsc, axis=0, keepdims=True), io8, N_EXPERT_GROUPS)
        gkeep = jnp.where(pick, 1.0, gkeep)
        gsc = jnp.where(pick, neg_inf, gsc)
    cand = jnp.concatenate(
        [jnp.where(gkeep[g:g + 1] > 0.5, sel[g * epg:(g + 1) * epg], neg_inf) for g in range(N_EXPERT_GROUPS)], axis=0)

    ioe = lax.broadcasted_iota(jnp.int32, (N_EXPERTS, tl), 0)
    picks, pscore = [], []
    chosen = jnp.zeros_like(cand)
    for _ in range(TOP_K):
        e_k = _first_index(cand == jnp.max(cand, axis=0, keepdims=True), ioe, N_EXPERTS)
        hit = ioe == e_k
        picks.append(e_k)
        pscore.append(jnp.sum(jnp.where(hit, scores, 0.0), axis=0, keepdims=True))
        chosen = jnp.where(hit, 1.0, chosen)
        cand = jnp.where(hit, neg_inf, cand)
    wsum = pscore[0]
    for k in range(1, TOP_K):
        wsum = wsum + pscore[k]

    ti = lax.broadcasted_iota(jnp.int32, (tl, tl), 0)
    tj = lax.broadcasted_iota(jnp.int32, (tl, tl), 1)
    before = (ti < tj).astype(BF16)
    rank = jnp.dot(chosen.astype(BF16), before, preferred_element_type=F32) + carry[...]
    total = carry[...] + jnp.sum(chosen, axis=1, keepdims=True)
    carry[...] = total
    cnt_ref[...] = jnp.broadcast_to(total, cnt_ref.shape)

    eidx_ref[...] = jnp.concatenate(picks, axis=0)
    wts_ref[...] = jnp.concatenate([p / wsum * ROUTED_SCALE for p in pscore], axis=0)
    rank_ref[...] = jnp.concatenate(
        [jnp.sum(jnp.where(ioe == e_k, rank, 0.0), axis=0, keepdims=True) for e_k in picks], axis=0).astype(jnp.int32)


def _route(logits, router_bias):
    m = logits.shape[0]
    ne = N_EXPERTS
    tl = ROUTE_TL
    tok = lambda dt: jax.ShapeDtypeStruct((TOP_K, m), dt)
    tok_spec = pl.BlockSpec((TOP_K, tl), lambda i: (0, i))
    return pl.pallas_call(
        _route_kernel,
        out_shape=[tok(jnp.int32), tok(F32), tok(jnp.int32), jax.ShapeDtypeStruct((ne, LANES), F32)],
        grid=(m // tl,),
        in_specs=[pl.BlockSpec((tl, LANES), lambda i: (i, 0)), pl.BlockSpec((ne, 1), lambda i: (0, 0))],
        out_specs=[tok_spec, tok_spec, tok_spec, pl.BlockSpec((ne, LANES), lambda i: (0, 0))],
        scratch_shapes=[pltpu.VMEM((ne, 1), F32)],
        compiler_params=_cparams("arbitrary"),
    )(logits, router_bias.reshape(ne, 1))


def _expert_kernel(tile_ref, e_ref, lo_ref, hi_ref, x_ref, wg_ref, wu_ref, wd_ref, o_ref, wg_s, wu_s, wd_s):
    w = pl.program_id(0)
    lo, hi = lo_ref[w], hi_ref[w]
    tm = x_ref.shape[0]
    base = tile_ref[w] * tm

    @pl.when((w == 0) | (e_ref[w] != e_ref[jnp.maximum(w - 1, 0)]))
    def _():
        wg_s[...] = wg_ref[0].astype(BF16)
        wu_s[...] = wu_ref[0].astype(BF16)
        wd_s[...] = wd_ref[0].astype(BF16)

    @pl.when(hi > lo)
    def _():
        x = x_ref[...]
        gate = jnp.dot(x, wg_s[...], preferred_element_type=F32)
        up = jnp.dot(x, wu_s[...], preferred_element_type=F32)
        y = jnp.dot((_silu(gate) * up).astype(BF16), wd_s[...], preferred_element_type=F32).astype(o_ref.dtype)
        rid = base + lax.broadcasted_iota(jnp.int32, (tm, 1), 0)

        @pl.when(lo == base)
        def _():
            o_ref[...] = jnp.where(rid < hi, y, jnp.zeros_like(y))

        @pl.when(lo != base)
        def _():
            o_ref[...] = jnp.where((rid >= lo) & (rid < hi), y, o_ref[...])


def _drop_alias_ref(kernel_fn, pos):
    def body(*refs):
        return kernel_fn(*refs[:pos], *refs[pos + 1:])
    return body


def _experts(item_tile, item_e, item_lo, item_hi, xs, w_gate, w_up, w_down, y_all, chunk):
    rows_c, d = xs.shape
    tm = EXPERT_TM
    ff = w_gate.shape[-1]
    tile0 = chunk * (rows_c // tm)
    row_map = lambda w, t, e, lo, hi: (t[w], 0)
    args = [item_tile, item_e, item_lo, item_hi, xs, w_gate, w_up, w_down, y_all]
    carry = len(args) - 1
    return pl.pallas_call(
        _drop_alias_ref(_expert_kernel, carry),
        out_shape=jax.ShapeDtypeStruct(y_all.shape, y_all.dtype),
        grid_spec=pltpu.PrefetchScalarGridSpec(
            num_scalar_prefetch=4,
            grid=(item_tile.shape[0],),
            in_specs=[pl.BlockSpec((tm, d), row_map),
                      pl.BlockSpec((1, d, ff), lambda w, t, e, lo, hi: (e[w], 0, 0)),
                      pl.BlockSpec((1, d, ff), lambda w, t, e, lo, hi: (e[w], 0, 0)),
                      pl.BlockSpec((1, ff, d), lambda w, t, e, lo, hi: (e[w], 0, 0)),
                      pl.BlockSpec(memory_space=pl.ANY)],
            out_specs=pl.BlockSpec((tm, d), lambda w, t, e, lo, hi: (tile0 + t[w], 0)),
            scratch_shapes=[pltpu.VMEM((d, ff), BF16), pltpu.VMEM((d, ff), BF16), pltpu.VMEM((ff, d), BF16)]),
        input_output_aliases={carry: 0},
        compiler_params=_cparams("arbitrary"),
    )(*args)


def _shared_kernel(anchor_ref, h2_ref, wg_ref, wu_ref, wd_ref, o_ref):
    del anchor_ref
    h2 = h2_ref[...]
    gate = jnp.dot(h2, wg_ref[...], preferred_element_type=F32)
    up = jnp.dot(h2, wu_ref[...], preferred_element_type=F32)
    o_ref[...] = jnp.dot((_silu(gate) * up).astype(BF16), wd_ref[...], preferred_element_type=F32)


def _shared_ffn(anchor, h2, w_sh_gate, w_sh_up, w_sh_down):
    m, d = h2.shape
    tm = min(FINAL_TM, m)
    ff = w_sh_gate.shape[1]
    row = pl.BlockSpec((tm, d), lambda i: (i, 0))
    return pl.pallas_call(
        _shared_kernel,
        out_shape=jax.ShapeDtypeStruct((m, d), F32),
        grid=(m // tm,),
        in_specs=[pl.BlockSpec(memory_space=pltpu.SMEM), row,
                  pl.BlockSpec((d, ff), lambda i: (0, 0)), pl.BlockSpec((d, ff), lambda i: (0, 0)),
                  pl.BlockSpec((ff, d), lambda i: (0, 0))],
        out_specs=row,
        compiler_params=_cparams("parallel"),
    )(anchor, h2, w_sh_gate, w_sh_up, w_sh_down)


def _final_kernel(x1_ref, fs_ref, yk_ref, wt_ref, gt2_ref, gpost_ref, o_ref):
    f = fs_ref[...]
    wt = wt_ref[...]
    for k in range(TOP_K):
        f = f + yk_ref[k].astype(F32) * wt[:, k:k + 1]
    o_ref[...] = x1_ref[...] + gt2_ref[0] * _rms(f, gpost_ref[...])


def _final(x_carry, f_shared, y_tok, wts_t, gt2, g_post_ffn, rows_per_batch, chunk):
    m, d = x_carry.shape
    rows_c = y_tok.shape[1]
    tm = min(FINAL_TM, rows_c)
    tpb = rows_per_batch // tm
    off = chunk * (rows_c // tm)
    row = pl.BlockSpec((tm, d), lambda i: (off + i, 0))
    return pl.pallas_call(
        _final_kernel,
        out_shape=jax.ShapeDtypeStruct((m, d), F32),
        grid=(rows_c // tm,),
        in_specs=[row, row, pl.BlockSpec((TOP_K, tm, d), lambda i: (0, i, 0)),
                  pl.BlockSpec((tm, TOP_K), lambda i: (off + i, 0)),
                  pl.BlockSpec((1, 1, d), lambda i: ((off + i) // tpb, 0, 0)),
                  pl.BlockSpec((1, d), lambda i: (0, 0))],
        out_specs=row,
        input_output_aliases={0: 0},
        compiler_params=_cparams("parallel"),
    )(x_carry, f_shared, y_tok, wts_t, gt2, g_post_ffn)


def kernel(x, c, ctx, c_ctx, w_ada, b_ada, g_pre_mix, g_post_mix, g_pre_ffn, g_post_ffn, w_in, lam_q1, lam_k1, lam_q2, lam_k2, g_attn_subln, conv_w, conv_b, dt_bias, a_log, d_skip, g_ssd_norm, w_branch_attn, w_branch_ssd, w_out, w_router, router_bias, w_e_gate, w_e_up, w_e_down, w_sh_gate, w_sh_up, w_sh_down):
    nb, ns, d = x.shape
    nl = ctx.shape[1]
    m = nb * ns
    li = 0

    c_all = jnp.zeros((3 * SUBLANES, d), F32).at[:nb].set(c).at[nb].set(c_ctx)
    mod = _modulation(c_all, w_ada[li], b_ada[li])
    sh1, sc1, gt1, sh2, sc2, gt2 = (mod[:nb, k * d:(k + 1) * d].reshape(nb, 1, d) for k in range(6))
    sh1c, sc1c = (mod[nb:nb + 1, k * d:(k + 1) * d].reshape(1, 1, d) for k in range(2))

    qk_dim = N_HEADS * 2 * ATTN_DH
    v_dim = N_HEADS * ATTN_DV
    o_q, o_k, o_v, o_z = 0, qk_dim, 2 * qk_dim, 2 * qk_dim + v_dim
    o_xbc = o_z + D_SSD
    o_dt = o_xbc + XBC_DIM
    o_g = o_dt + 2 * SSD_HEADS
    wi = w_in[li]
    w_qkv = wi[:, o_q:o_z].astype(BF16)
    w_qk_rot = _rotate_half_columns(wi[:, o_q:o_v]).astype(BF16)
    w_kv = wi[:, o_k:o_z].astype(BF16)
    w_xbc = wi[:, o_xbc:o_dt].astype(BF16)
    w_rest = jnp.concatenate([wi[:, o_xbc:o_dt], wi[:, o_z:o_xbc], wi[:, o_g:]], axis=1).astype(BF16)
    w_dt = jnp.zeros((d, LANES), BF16).at[:, :2 * SSD_HEADS].set(wi[:, o_dt:o_g].astype(BF16))

    x2d = x.reshape(m, d)
    c2d = ctx.reshape(nb * nl, d)
    g1 = g_pre_mix[li].reshape(1, d)
    cos, sin = _rope_tables(ns)
    qkv = _project(x2d, sh1, sc1, g1, w_qkv, rows_per_mod=ns, rope=(w_qk_rot, cos, sin, qk_dim))
    rest, dt_lat, dtT_lat = _project(x2d, sh1, sc1, g1, w_rest, rows_per_mod=ns, w_dt=w_dt)
    kv_ctx = _project(c2d, sh1c, sc1c, g1, w_kv, rows_per_mod=nb * nl)
    xbc_ctx, dt_ctx, dtT_ctx = _project(c2d, sh1c, sc1c, g1, w_xbc, rows_per_mod=nb * nl, w_dt=w_dt)

    lam = (jnp.exp(jnp.sum(lam_q1[li] * lam_k1[li])) - jnp.exp(jnp.sum(lam_q2[li] * lam_k2[li])) + LAM_INIT)
    oattn = _diff_attention(lam.reshape(1).astype(F32), qkv, kv_ctx, g_attn_subln[li].reshape(1, ATTN_DV), nb, ns)

    y_ssd, y_sorted = _ssd(rest, 0, dt_lat, dtT_lat, xbc_ctx, dt_ctx, dtT_ctx, conv_w[li], conv_b[li], dt_bias[li],
                           a_log[li], d_skip[li], nb, ns)

    x1, h2, logits = _merge(
        y_ssd, rest, oattn, x2d, gt1, sh2, sc2, g_ssd_norm[li].reshape(1, d), g_post_mix[li].reshape(1, d),
        g_pre_ffn[li].reshape(1, d), w_branch_attn[li].astype(BF16), w_branch_ssd[li].astype(BF16),
        w_out[li].astype(BF16), w_router[li], ns)

    eidx, wts, rank, counts = _route(logits, router_bias[li])

    tm = EXPERT_TM
    n_rows = m * TOP_K
    n_tiles = n_rows // tm
    counts = counts[:, 0].astype(jnp.int32)
    ends = jnp.cumsum(counts)
    start = ends - counts
    eid = jnp.arange(N_EXPERTS, dtype=jnp.int32)
    dest = jnp.sum(jnp.where(eidx[None] == eid[:, None, None], start[:, None, None], 0), axis=0) + rank
    tok = jnp.arange(m, dtype=jnp.int32)
    slot_tok = jnp.sort((eidx * m + tok[None, :]).reshape(-1), stable=False) % m

    rows_c = n_rows // MOE_CHUNKS
    tiles_c = rows_c // tm
    for ch in range(MOE_CHUNKS):
        row0 = ch * rows_c
        xs = h2.at[slot_tok[row0:row0 + rows_c]].get(mode="promise_in_bounds")
        cuts = jnp.sort(jnp.concatenate([jnp.arange(tiles_c, dtype=jnp.int32) * tm,
                                         jnp.clip(start[1:] - row0, 0, rows_c)]))
        item_hi = jnp.concatenate([cuts[1:], jnp.full((1,), rows_c, jnp.int32)])
        item_tile = jnp.minimum(cuts // tm, tiles_c - 1)
        item_e = jnp.minimum(jnp.sum((ends[None, :] <= (cuts + row0)[:, None]).astype(jnp.int32), axis=1),
                             N_EXPERTS - 1)
        y_sorted = _experts(item_tile, item_e, cuts, item_hi, xs, w_e_gate[li], w_e_up[li], w_e_down[li],
                            y_sorted, ch)

    tok_c = m // COMBINE_CHUNKS
    wts_t = wts.T
    f_shared = _shared_ffn(slot_tok[:SUBLANES], h2, w_sh_gate[li].astype(BF16), w_sh_up[li].astype(BF16),
                           w_sh_down[li].astype(BF16))
    out = x1
    for ch in range(COMBINE_CHUNKS):
        dest_c = dest[:, ch * tok_c:(ch + 1) * tok_c].reshape(-1)
        y_tok = y_sorted.at[dest_c].get(mode="promise_in_bounds").reshape(TOP_K, tok_c, d)
        out = _final(out, f_shared, y_tok, wts_t, gt2, g_post_ffn[li].reshape(1, d), ns, ch)
    return out.reshape(nb, ns, d)
```
